```python
import jax, jax.numpy as jnp
from jax import lax
import numpy as np


D_MODEL = 1024
BATCH = 8
SEQ = 2048
DEPTH = 2

N_META = 16
Q_BLOCK = 128
SB_HEADS = 4
SB_HEAD_DIM = 64
SB_WIDTH = SB_HEADS * SB_HEAD_DIM
MLA_HEADS = 8
MLA_NOPE_DIM = 64
MLA_ROPE_DIM = 32
MLA_V_DIM = 64
MLA_Q_RANK = 256
MLA_KV_RANK = 128
MLA_WIDTH = MLA_HEADS * MLA_V_DIM
ROPE_THETA = 10000.0
CONV_CH = 256
CONV_K = 31
MIX_WIDTH = SB_WIDTH + MLA_WIDTH + CONV_CH
IN_SIZES = (SB_WIDTH, SB_WIDTH, SB_WIDTH, MLA_Q_RANK, MLA_KV_RANK, MLA_ROPE_DIM, CONV_CH, CONV_CH)
N_IN = sum(IN_SIZES)
N_EXPERTS = 32
TOP_K = 4
D_FF = D_MODEL
SWIGLU_LIMIT = 7.0
SWIGLU_ALPHA = 1.702
EXPERT_BLOCK = 128
DEEPNORM_ALPHA = (2 * DEPTH) ** 0.25
DEEPNORM_BETA = (8 * DEPTH) ** -0.25
LN_EPS = 1e-5
RMS_EPS = 1e-6

kernel_name = "hymba_style_sb_mla_conformer_moe_deepnorm"


def layer_norm(x, g, b):
    xf = x.astype(jnp.float32)
    mu = jnp.mean(xf, axis=-1, keepdims=True)
    var = jnp.mean(jnp.square(xf - mu), axis=-1, keepdims=True)
    y = (xf - mu) * lax.rsqrt(var + LN_EPS) * g.astype(jnp.float32) + b.astype(jnp.float32)
    return y.astype(x.dtype)


def rms_norm(x, g):
    xf = x.astype(jnp.float32)
    y = xf * lax.rsqrt(jnp.mean(jnp.square(xf), axis=-1, keepdims=True) + RMS_EPS) * g.astype(jnp.float32)
    return y.astype(x.dtype)


def rope_tables(length, dtype):
    inv = 1.0 / (ROPE_THETA ** (jnp.arange(0, MLA_ROPE_DIM, 2, dtype=jnp.float32) / MLA_ROPE_DIM))
    ang = jnp.arange(length, dtype=jnp.float32)[:, None] * inv[None, :]
    return jnp.cos(ang).astype(dtype), jnp.sin(ang).astype(dtype)


def rotary(x, cos, sin):
    x1, x2 = jnp.split(x, 2, axis=-1)
    c = cos[None, :, None, :]
    s = sin[None, :, None, :]
    return jnp.concatenate([x1 * c - x2 * s, x1 * s + x2 * c], axis=-1)


def sweep_query_blocks(attend, q):
    b, nh, total, dq = q.shape
    s = total - N_META
    nb = s // Q_BLOCK
    out_meta = attend(q[:, :, :N_META], jnp.arange(N_META))
    q_blocks = q[:, :, N_META:].reshape(b, nh, nb, Q_BLOCK, dq).transpose(2, 0, 1, 3, 4)
    pos_blocks = (N_META + jnp.arange(s)).reshape(nb, Q_BLOCK)
    out_blocks = lax.map(lambda qp: attend(qp[0], qp[1]), (q_blocks, pos_blocks))
    dv = out_blocks.shape[-1]
    out_real = out_blocks.transpose(1, 2, 0, 3, 4).reshape(b, nh, s, dv)
    return jnp.concatenate([out_meta, out_real], axis=2)


def stick_breaking_attend(q, k, v, q_pos):
    k_pos = jnp.arange(k.shape[2])
    z = jnp.einsum('bhqd,bhkd->bhqk', q, k).astype(jnp.float32) * (SB_HEAD_DIM ** -0.5)
    mask = k_pos[None, :] < q_pos[:, None]
    log_stay = jnp.where(mask, jax.nn.log_sigmoid(-z), 0.0)
    log_stay_after = lax.cumsum(log_stay, axis=3, reverse=True) - log_stay
    w = jnp.where(mask, jnp.exp(jax.nn.log_sigmoid(z) + log_stay_after), 0.0)
    return jnp.einsum('bhqk,bhkd->bhqd', w.astype(v.dtype), v)


def causal_softmax_attend(q, k, v, q_pos):
    k_pos = jnp.arange(k.shape[2])
    s = jnp.einsum('bhqd,bhkd->bhqk', q, k).astype(jnp.float32) * ((MLA_NOPE_DIM + MLA_ROPE_DIM) ** -0.5)
    s = jnp.where(k_pos[None, :] <= q_pos[:, None], s, -jnp.inf)
    p = jax.nn.softmax(s, axis=-1)
    return jnp.einsum('bhqk,bhkd->bhqd', p.astype(v.dtype), v)


def causal_depthwise_conv(u, w, b):
    out = lax.conv_general_dilated(u, w[:, None, :].astype(u.dtype), window_strides=(1,),
                                   padding=[(CONV_K - 1, 0)],
                                   dimension_numbers=('NWC', 'WIO', 'NWC'),
                                   feature_group_count=u.shape[-1])
    return out + b.astype(u.dtype)


def hybrid_mixer(h, w_in, q_norm_g, w_uq, kv_norm_g, w_ukv, conv_w, conv_b, conv_ln_g, conv_ln_b,
                 grp_norm_g, w_out, cos, sin):
    B, L, _ = h.shape
    proj = h @ w_in
    sb_q, sb_k, sb_v, c_q, c_kv, k_pe, conv_a, conv_g = jnp.split(
        proj, np.cumsum(IN_SIZES)[:-1].tolist(), axis=-1)

    def heads(t, n):
        return t.reshape(B, L, n, -1).transpose(0, 2, 1, 3)

    q, k, v = heads(sb_q, SB_HEADS), heads(sb_k, SB_HEADS), heads(sb_v, SB_HEADS)
    sb_out = sweep_query_blocks(lambda qb, pos: stick_breaking_attend(qb, k, v, pos), q)
    sb_out = sb_out.transpose(0, 2, 1, 3).reshape(B, L, SB_WIDTH)

    qm = (rms_norm(c_q, q_norm_g) @ w_uq).reshape(B, L, MLA_HEADS, MLA_NOPE_DIM + MLA_ROPE_DIM)
    qm = jnp.concatenate([qm[..., :MLA_NOPE_DIM], rotary(qm[..., MLA_NOPE_DIM:], cos, sin)], axis=-1)
    kvm = (rms_norm(c_kv, kv_norm_g) @ w_ukv).reshape(B, L, MLA_HEADS, MLA_NOPE_DIM + MLA_V_DIM)
    k_nope, vm = kvm[..., :MLA_NOPE_DIM], kvm[..., MLA_NOPE_DIM:]
    k_rot = rotary(k_pe[:, :, None, :], cos, sin)
    km = jnp.concatenate([k_nope, jnp.broadcast_to(k_rot, (B, L, MLA_HEADS, MLA_ROPE_DIM))], axis=-1)
    qm, km, vm = qm.transpose(0, 2, 1, 3), km.transpose(0, 2, 1, 3), vm.transpose(0, 2, 1, 3)
    mla_out = sweep_query_blocks(lambda qb, pos: causal_softmax_attend(qb, km, vm, pos), qm)
    mla_out = mla_out.transpose(0, 2, 1, 3).reshape(B, L, MLA_WIDTH)

    u = conv_a * jax.nn.sigmoid(conv_g)
    c = causal_depthwise_conv(u, conv_w, conv_b)
    c = jax.nn.silu(layer_norm(c, conv_ln_g, conv_ln_b))

    g_sb, g_mla, g_conv = jnp.split(grp_norm_g, [SB_WIDTH, SB_WIDTH + MLA_WIDTH])
    y = jnp.concatenate([rms_norm(sb_out, g_sb), rms_norm(mla_out, g_mla), rms_norm(c, g_conv)], axis=-1)
    return y @ w_out


def moe_ffn(h, router_w, router_b, w_gate_up, b_gate_up, w_down, b_down):
    B, L, D = h.shape
    xt = h.reshape(-1, D)
    T = xt.shape[0]
    logits = (xt @ router_w + router_b).astype(jnp.float32)
    top_val, top_idx = lax.top_k(logits, TOP_K)
    gate = jax.nn.softmax(top_val, axis=-1)
    n_assign = T * TOP_K
    flat_e = top_idx.reshape(-1)
    flat_tok = jnp.arange(n_assign) // TOP_K
    flat_g = gate.reshape(-1)
    order = jnp.argsort(flat_e)
    e_sorted = flat_e[order]
    counts = jnp.bincount(flat_e, length=N_EXPERTS)
    padded = (counts + EXPERT_BLOCK - 1) // EXPERT_BLOCK * EXPERT_BLOCK
    start = jnp.cumsum(counts) - counts
    pend = jnp.cumsum(padded)
    pstart = pend - padded
    dest = pstart[e_sorted] + (jnp.arange(n_assign) - start[e_sorted])
    n_rows = -(-n_assign // EXPERT_BLOCK) * EXPERT_BLOCK + N_EXPERTS * EXPERT_BLOCK
    n_blk = n_rows // EXPERT_BLOCK
    row_tok = jnp.zeros((n_rows,), jnp.int32).at[dest].set(flat_tok[order])
    row_g = jnp.zeros((n_rows,), jnp.float32).at[dest].set(flat_g[order])
    blk_e = jnp.minimum(jnp.searchsorted(pend, jnp.arange(n_blk) * EXPERT_BLOCK, side='right'), N_EXPERTS - 1)
    xs = xt[row_tok].reshape(n_blk, EXPERT_BLOCK, D)

    def expert_block(args):
        xb, e = args
        gu = xb @ w_gate_up[e] + b_gate_up[e]
        g, up = gu[:, :D_FF], gu[:, D_FF:]
        g = jnp.minimum(g, SWIGLU_LIMIT)
        up = jnp.clip(up, -SWIGLU_LIMIT, SWIGLU_LIMIT)
        act = (up + 1.0) * (g * jax.nn.sigmoid(SWIGLU_ALPHA * g))
        return act @ w_down[e] + b_down[e]

    ys = lax.map(expert_block, (xs, blk_e)).reshape(n_rows, D)
    out = jnp.zeros_like(xt).at[row_tok].add(ys * row_g[:, None].astype(ys.dtype))
    return out.reshape(B, L, D)


def setup_inputs(seed: int = 0) -> dict:
    key = jax.random.key(seed)
    ks = jax.random.split(key, 25)
    f32 = jnp.float32

    def nrm(k, shape, scale):
        return jax.random.normal(k, shape, f32) * scale

    def gain(k, shape):
        return 1.0 + 0.05 * jax.random.normal(k, shape, f32)

    def bias(k, shape, scale=0.02):
        return scale * jax.random.normal(k, shape, f32)

    return {
        "x": nrm(ks[0], (BATCH, SEQ, D_MODEL), 1.0),
        "meta_tokens": nrm(ks[1], (N_META, D_MODEL), 1.0),
        "ln_in_g": gain(ks[2], (D_MODEL,)),
        "ln_in_b": bias(ks[3], (D_MODEL,)),
        "w_in": nrm(ks[4], (DEPTH, D_MODEL, N_IN), D_MODEL ** -0.5),
        "q_norm_g": gain(ks[5], (DEPTH, MLA_Q_RANK)),
        "w_uq": nrm(ks[6], (DEPTH, MLA_Q_RANK, MLA_HEADS * (MLA_NOPE_DIM + MLA_ROPE_DIM)), MLA_Q_RANK ** -0.5),
        "kv_norm_g": gain(ks[7], (DEPTH, MLA_KV_RANK)),
        "w_ukv": nrm(ks[8], (DEPTH, MLA_KV_RANK, MLA_HEADS * (MLA_NOPE_DIM + MLA_V_DIM)), MLA_KV_RANK ** -0.5),
        "conv_w": nrm(ks[9], (DEPTH, CONV_K, CONV_CH), CONV_K ** -0.5),
        "conv_b": bias(ks[10], (DEPTH, CONV_CH)),
        "conv_ln_g": gain(ks[11], (DEPTH, CONV_CH)),
        "conv_ln_b": bias(ks[12], (DEPTH, CONV_CH)),
        "grp_norm_g": gain(ks[13], (DEPTH, MIX_WIDTH)),
        "w_out": nrm(ks[14], (DEPTH, MIX_WIDTH, D_MODEL), MIX_WIDTH ** -0.5 * DEEPNORM_BETA),
        "ln_mix_g": gain(ks[15], (DEPTH, D_MODEL)),
        "ln_mix_b": bias(ks[16], (DEPTH, D_MODEL)),
        "router_w": nrm(ks[17], (DEPTH, D_MODEL, N_EXPERTS), D_MODEL ** -0.5),
        "router_b": bias(ks[18], (DEPTH, N_EXPERTS), 0.01),
        "w_gate_up": nrm(ks[19], (DEPTH, N_EXPERTS, D_MODEL, 2 * D_FF), D_MODEL ** -0.5),
        "b_gate_up": bias(ks[20], (DEPTH, N_EXPERTS, 2 * D_FF)),
        "w_down": nrm(ks[21], (DEPTH, N_EXPERTS, D_FF, D_MODEL), D_FF ** -0.5 * DEEPNORM_BETA),
        "b_down": bias(ks[22], (DEPTH, N_EXPERTS, D_MODEL)),
        "ln_ffn_g": gain(ks[23], (DEPTH, D_MODEL)),
        "ln_ffn_b": bias(ks[24], (DEPTH, D_MODEL)),
    }


def reference(x, meta_tokens, ln_in_g, ln_in_b, w_in, q_norm_g, w_uq, kv_norm_g, w_ukv, conv_w, conv_b,
              conv_ln_g, conv_ln_b, grp_norm_g, w_out, ln_mix_g, ln_mix_b, router_w, router_b,
              w_gate_up, b_gate_up, w_down, b_down, ln_ffn_g, ln_ffn_b):
    B = x.shape[0]
    meta = jnp.broadcast_to(meta_tokens[None].astype(x.dtype), (B, N_META, D_MODEL))
    h = layer_norm(jnp.concatenate([meta, x], axis=1), ln_in_g, ln_in_b)
    cos, sin = rope_tables(h.shape[1], h.dtype)
    for l in range(DEPTH):
        mix = hybrid_mixer(h, w_in[l], q_norm_g[l], w_uq[l], kv_norm_g[l], w_ukv[l], conv_w[l], conv_b[l],
                           conv_ln_g[l], conv_ln_b[l], grp_norm_g[l], w_out[l], cos, sin)
        h = layer_norm(DEEPNORM_ALPHA * h + mix, ln_mix_g[l], ln_mix_b[l])
        ffn = moe_ffn(h, router_w[l], router_b[l], w_gate_up[l], b_gate_up[l], w_down[l], b_down[l])
        h = layer_norm(DEEPNORM_ALPHA * h + ffn, ln_ffn_g[l], ln_ffn_b[l])
    return h[:, N_META:, :]
```

```python
import functools

import jax
import jax.numpy as jnp
import numpy as np
from jax import lax
from jax.experimental import pallas as pl
from jax.experimental.pallas import tpu as pltpu

F32 = jnp.float32
BF16 = jnp.bfloat16

N_META = 16
SB_HEADS = 4
SB_HEAD_DIM = 64
SB_WIDTH = SB_HEADS * SB_HEAD_DIM
MLA_HEADS = 8
MLA_NOPE_DIM = 64
MLA_ROPE_DIM = 32
MLA_V_DIM = 64
MLA_Q_RANK = 256
MLA_KV_RANK = 128
MLA_WIDTH = MLA_HEADS * MLA_V_DIM
ROPE_THETA = 10000.0
CONV_CH = 256
CONV_K = 31
N_EXPERTS = 32
TOP_K = 4
SWIGLU_LIMIT = 7.0
SWIGLU_ALPHA = 1.702
LN_EPS = 1e-5
RMS_EPS = 1e-6

LANES = 128
HEAD_PAIR = LANES // SB_HEAD_DIM
ROPE_HALF = MLA_ROPE_DIM // 2
VMEM_LIMIT = 48 * 1024 * 1024
NEG_BIG = -1e30

IN_SB = 3 * SB_WIDTH
IN_CQ = IN_SB
IN_CKV = IN_CQ + MLA_Q_RANK
IN_KPE = IN_CKV + MLA_KV_RANK
IN_CA = IN_KPE + LANES
IN_CG = IN_CA + CONV_CH
IN_TOTAL = IN_CG + CONV_CH
MQ_W = MLA_HEADS * MLA_NOPE_DIM + 2 * LANES
MK_W = MLA_HEADS * MLA_NOPE_DIM + LANES


def _row_tile(n, cap, mult=16):
    best = None
    for t in range(mult, min(n, cap) + 1, mult):
        if n % t == 0:
            best = t
    assert best is not None, (n, cap, mult)
    return best


def _layer_norm(x, g, b):
    mu = jnp.mean(x, axis=-1, keepdims=True)
    xc = x - mu
    var = jnp.mean(xc * xc, axis=-1, keepdims=True)
    return xc * lax.rsqrt(var + LN_EPS) * g + b


def _rms_norm(x, g):
    return x * lax.rsqrt(jnp.mean(x * x, axis=-1, keepdims=True) + RMS_EPS) * g


def _sigmoid(x):
    return 1.0 / (1.0 + jnp.exp(-x))


def _ln_kernel(x_ref, g_ref, b_ref, o_ref):
    o_ref[...] = _layer_norm(x_ref[...], g_ref[...], b_ref[...])


def _ln_call(x2d, g, b, tm):
    t, d = x2d.shape
    return pl.pallas_call(
        _ln_kernel,
        out_shape=jax.ShapeDtypeStruct((t, d), F32),
        grid=(t // tm,),
        in_specs=[pl.BlockSpec((tm, d), lambda i: (i, 0)),
                  pl.BlockSpec((1, d), lambda i: (0, 0)),
                  pl.BlockSpec((1, d), lambda i: (0, 0))],
        out_specs=pl.BlockSpec((tm, d), lambda i: (i, 0)),
        compiler_params=pltpu.CompilerParams(dimension_semantics=("parallel",),
                                             vmem_limit_bytes=VMEM_LIMIT),
        name="ln_in",
    )(x2d, g.reshape(1, d), b.reshape(1, d))


def _inproj_kernel(h_ref, w_ref, qg_ref, wuq_ref, kvg_ref, wukv_ref, cos_ref, sin_ref,
                   sb_ref, mq_ref, mk_ref, mv_ref, cu_ref):
    h = h_ref[0].astype(BF16)
    proj = jnp.dot(h, w_ref[...], preferred_element_type=F32)
    sb_ref[0] = proj[:, :IN_SB].astype(BF16)
    cos = cos_ref[...]
    sin = sin_ref[...]

    def rot(x):
        return x * cos + pltpu.roll(x, LANES // 2, 1) * sin

    cq = _rms_norm(proj[:, IN_CQ:IN_CKV], qg_ref[...])
    qm = jnp.dot(cq.astype(BF16), wuq_ref[...], preferred_element_type=F32)
    qm = qm * ((MLA_NOPE_DIM + MLA_ROPE_DIM) ** -0.5)
    nope_w = MLA_HEADS * MLA_NOPE_DIM
    mq_ref[0, :, :nope_w] = qm[:, :nope_w].astype(BF16)
    mq_ref[0, :, nope_w:nope_w + LANES] = rot(qm[:, nope_w:nope_w + LANES]).astype(BF16)
    mq_ref[0, :, nope_w + LANES:] = rot(qm[:, nope_w + LANES:]).astype(BF16)

    ckv = _rms_norm(proj[:, IN_CKV:IN_KPE], kvg_ref[...])
    kv = jnp.dot(ckv.astype(BF16), wukv_ref[...], preferred_element_type=F32)
    mk_ref[0, :, :nope_w] = kv[:, :nope_w].astype(BF16)
    mk_ref[0, :, nope_w:] = rot(proj[:, IN_KPE:IN_CA]).astype(BF16)
    mv_ref[0] = kv[:, nope_w:].astype(BF16)

    cu_ref[0] = proj[:, IN_CA:IN_CG] * _sigmoid(proj[:, IN_CG:IN_TOTAL])


def _inproj_call(h3, w_in_p, qg, wuq_p, kvg, wukv_p, cos_t, sin_t, tm):
    b, lp, d = h3.shape
    nt = lp // tm
    const2 = lambda bi, ti: (0, 0)
    tok3 = lambda bi, ti: (bi, ti, 0)
    outs = (jax.ShapeDtypeStruct((b, lp, IN_SB), BF16),
            jax.ShapeDtypeStruct((b, lp, MQ_W), BF16),
            jax.ShapeDtypeStruct((b, lp, MK_W), BF16),
            jax.ShapeDtypeStruct((b, lp, MLA_WIDTH), BF16),
            jax.ShapeDtypeStruct((b, lp, CONV_CH), F32))
    return pl.pallas_call(
        _inproj_kernel,
        out_shape=outs,
        grid=(b, nt),
        in_specs=[pl.BlockSpec((1, tm, d), tok3),
                  pl.BlockSpec(w_in_p.shape, const2),
                  pl.BlockSpec((1, MLA_Q_RANK), const2),
                  pl.BlockSpec(wuq_p.shape, const2),
                  pl.BlockSpec((1, MLA_KV_RANK), const2),
                  pl.BlockSpec(wukv_p.shape, const2),
                  pl.BlockSpec((tm, LANES), lambda bi, ti: (ti, 0)),
                  pl.BlockSpec((tm, LANES), lambda bi, ti: (ti, 0))],
        out_specs=tuple(pl.BlockSpec((1, tm, s.shape[2]), tok3) for s in outs),
        compiler_params=pltpu.CompilerParams(dimension_semantics=("parallel", "parallel"),
                                             vmem_limit_bytes=VMEM_LIMIT),
        name="in_proj",
    )(h3, w_in_p, qg.reshape(1, -1), wuq_p, kvg.reshape(1, -1), wukv_p, cos_t, sin_t)


def _sb_kernel(q_ref, k_ref, v_ref, o_ref, *, tq, seq):
    i = pl.program_id(2)
    nq = seq // tq
    lane = lax.broadcasted_iota(jnp.int32, (1, LANES), 1)
    head_sel = (lane < SB_HEAD_DIM, lane >= SB_HEAD_DIM)

    def tri(w):
        r = lax.broadcasted_iota(jnp.int32, (w, w), 0)
        c = lax.broadcasted_iota(jnp.int32, (w, w), 1)
        return jnp.where(r >= c, 1.0, 0.0).astype(BF16)

    def tile(q, kt, vt, carry, mask, u):
        acc, csum = carry
        pvs, new_c = [], []
        for hd in range(HEAD_PAIR):
            qh = jnp.where(head_sel[hd], q, jnp.zeros_like(q))
            z = lax.dot_general(qh, kt, (((1,), (1,)), ((), ())), preferred_element_type=F32)
            ls = jnp.minimum(-z, 0.0) - jnp.log(1.0 + jnp.exp(-jnp.abs(z)))
            if mask is not None:
                ls = jnp.where(mask, ls, 0.0)
            hi = ls.astype(BF16)
            lo = (ls - hi.astype(F32)).astype(BF16)
            cum = (jnp.dot(hi, u, preferred_element_type=F32)
                   + jnp.dot(lo, u, preferred_element_type=F32))
            w = jnp.exp(z + cum + csum[hd])
            if mask is not None:
                w = jnp.where(mask, w, 0.0)
            pvs.append(jnp.dot(w.astype(BF16), vt, preferred_element_type=F32))
            new_c.append(csum[hd] + cum[:, 0:1])
        return acc + jnp.where(head_sel[0], pvs[0], pvs[1]), tuple(new_c)

    def strict_mask(rows, cols):
        r = lax.broadcasted_iota(jnp.int32, (rows, cols), 0)
        c = lax.broadcasted_iota(jnp.int32, (rows, cols), 1)
        return c < r

    @pl.when(i < nq)
    def _():
        q = q_ref[0]
        u = tri(tq)
        zero_c = jnp.zeros((tq, 1), F32)
        carry = (jnp.zeros((tq, LANES), F32), (zero_c, zero_c))
        d0 = pl.multiple_of(i * tq, tq)
        carry = tile(q, k_ref[0, pl.ds(d0, tq), :], v_ref[0, pl.ds(d0, tq), :], carry,
                     strict_mask(tq, tq), u)

        def body(t, c):
            j0 = pl.multiple_of((i - 1 - t) * tq, tq)
            return tile(q, k_ref[0, pl.ds(j0, tq), :], v_ref[0, pl.ds(j0, tq), :], c, None, u)

        carry = lax.fori_loop(0, i, body, carry)
        carry = tile(q, k_ref[0, seq:seq + N_META, :], v_ref[0, seq:seq + N_META, :], carry,
                     None, tri(N_META))
        o_ref[0] = carry[0].astype(o_ref.dtype)

    @pl.when(i == nq)
    def _():
        q = q_ref[0, 0:N_META, :]
        zero_c = jnp.zeros((N_META, 1), F32)
        carry = (jnp.zeros((N_META, LANES), F32), (zero_c, zero_c))
        carry = tile(q, k_ref[0, seq:seq + N_META, :], v_ref[0, seq:seq + N_META, :], carry,
                     strict_mask(N_META, N_META), tri(N_META))
        o_ref[0, 0:N_META, :] = carry[0].astype(o_ref.dtype)


def _sb_call(sbqkv, tq):
    b, lp, _ = sbqkv.shape
    seq = lp - N_META
    nq = seq // tq
    npair = SB_HEADS // HEAD_PAIR
    return pl.pallas_call(
        functools.partial(_sb_kernel, tq=tq, seq=seq),
        out_shape=jax.ShapeDtypeStruct((b, lp, SB_WIDTH), BF16),
        grid=(b, npair, nq + 1),
        in_specs=[pl.BlockSpec((1, tq, LANES), lambda bi, p, i: (bi, i, p)),
                  pl.BlockSpec((1, lp, LANES), lambda bi, p, i: (bi, 0, npair + p)),
                  pl.BlockSpec((1, lp, LANES), lambda bi, p, i: (bi, 0, 2 * npair + p))],
        out_specs=pl.BlockSpec((1, tq, LANES), lambda bi, p, i: (bi, i, p)),
        compiler_params=pltpu.CompilerParams(
            dimension_semantics=("parallel", "parallel", "arbitrary"),
            vmem_limit_bytes=VMEM_LIMIT),
        name="sb_attn",
    )(sbqkv, sbqkv, sbqkv)


def _mla_kernel(qn_ref, qr_ref, kn_ref, kr_ref, v_ref, o_ref, *, tq, seq):
    p = pl.program_id(1)
    i = pl.program_id(2)
    nq = seq // tq
    lane = lax.broadcasted_iota(jnp.int32, (1, LANES), 1)
    lane2 = lax.broadcasted_iota(jnp.int32, (1, 2 * LANES), 1)
    out_sel = lane < MLA_V_DIM
    grp_head = (p % 2) * HEAD_PAIR
    head_sel = []
    for hd in range(HEAD_PAIR):
        r0 = LANES + (grp_head + hd) * ROPE_HALF
        r1 = r0 + LANES // 2
        nope = (lane2 >= hd * MLA_NOPE_DIM) & (lane2 < (hd + 1) * MLA_NOPE_DIM)
        rope = ((lane2 >= r0) & (lane2 < r0 + ROPE_HALF)) | ((lane2 >= r1) & (lane2 < r1 + ROPE_HALF))
        head_sel.append(nope | rope)

    def tile(qs, kt, vt, carry, mask):
        acc, ms, ls = carry
        pvs, new_m, new_l, alphas = [], [], [], []
        for hd in range(HEAD_PAIR):
            s = lax.dot_general(qs[hd], kt, (((1,), (1,)), ((), ())), preferred_element_type=F32)
            if mask is not None:
                s = jnp.where(mask, s, NEG_BIG)
            m_new = jnp.maximum(ms[hd], jnp.max(s, axis=1, keepdims=True))
            alpha = jnp.exp(ms[hd] - m_new)
            pexp = jnp.exp(s - m_new)
            new_l.append(alpha * ls[hd] + jnp.sum(pexp, axis=1, keepdims=True))
            new_m.append(m_new)
            alphas.append(alpha)
            pvs.append(jnp.dot(pexp.astype(BF16), vt, preferred_element_type=F32))
        acc = (acc * jnp.where(out_sel, alphas[0], alphas[1])
               + jnp.where(out_sel, pvs[0], pvs[1]))
        return acc, tuple(new_m), tuple(new_l)

    def incl_mask(rows, cols):
        r = lax.broadcasted_iota(jnp.int32, (rows, cols), 0)
        c = lax.broadcasted_iota(jnp.int32, (rows, cols), 1)
        return c <= r

    def finish(carry):
        acc, _, ls = carry
        return acc * jnp.where(out_sel, 1.0 / ls[0], 1.0 / ls[1])

    def kcat(rows):
        return jnp.concatenate([kn_ref[0, rows, :], kr_ref[0, rows, :]], axis=-1)

    def init(rows):
        neg = jnp.full((rows, 1), NEG_BIG, F32)
        zero = jnp.zeros((rows, 1), F32)
        return (jnp.zeros((rows, LANES), F32), (neg, neg), (zero, zero))

    @pl.when(i < nq)
    def _():
        q = jnp.concatenate([qn_ref[0], qr_ref[0]], axis=-1)
        qs = [jnp.where(head_sel[hd], q, jnp.zeros_like(q)) for hd in range(HEAD_PAIR)]
        d0 = pl.multiple_of(i * tq, tq)
        carry = tile(qs, kcat(pl.ds(d0, tq)), v_ref[0, pl.ds(d0, tq), :], init(tq),
                     incl_mask(tq, tq))

        def body(t, c):
            j0 = pl.multiple_of(t * tq, tq)
            return tile(qs, kcat(pl.ds(j0, tq)), v_ref[0, pl.ds(j0, tq), :], c, None)

        carry = lax.fori_loop(0, i, body, carry)
        meta = pl.ds(seq, N_META)
        carry = tile(qs, kcat(meta), v_ref[0, meta, :], carry, None)
        o_ref[0] = finish(carry).astype(o_ref.dtype)

    @pl.when(i == nq)
    def _():
        q = jnp.concatenate([qn_ref[0, 0:N_META, :], qr_ref[0, 0:N_META, :]], axis=-1)
        qs = [jnp.where(head_sel[hd], q, jnp.zeros_like(q)) for hd in range(HEAD_PAIR)]
        meta = pl.ds(seq, N_META)
        carry = tile(qs, kcat(meta), v_ref[0, meta, :], init(N_META),
                     incl_mask(N_META, N_META))
        o_ref[0, 0:N_META, :] = finish(carry).astype(o_ref.dtype)


def _mla_call(mq, mk, mv, tq):
    b, lp, _ = mq.shape
    seq = lp - N_META
    nq = seq // tq
    npair = MLA_HEADS // HEAD_PAIR
    rope_blk = MLA_HEADS * MLA_NOPE_DIM // LANES
    return pl.pallas_call(
        functools.partial(_mla_kernel, tq=tq, seq=seq),
        out_shape=jax.ShapeDtypeStruct((b, lp, MLA_WIDTH), BF16),
        grid=(b, npair, nq + 1),
        in_specs=[pl.BlockSpec((1, tq, LANES), lambda bi, p, i: (bi, i, p)),
                  pl.BlockSpec((1, tq, LANES), lambda bi, p, i: (bi, i, rope_blk + p // 2)),
                  pl.BlockSpec((1, lp, LANES), lambda bi, p, i: (bi, 0, p)),
                  pl.BlockSpec((1, lp, LANES), lambda bi, p, i: (bi, 0, rope_blk)),
                  pl.BlockSpec((1, lp, LANES), lambda bi, p, i: (bi, 0, p))],
        out_specs=pl.BlockSpec((1, tq, LANES), lambda bi, p, i: (bi, i, p)),
        compiler_params=pltpu.CompilerParams(
            dimension_semantics=("parallel", "parallel", "arbitrary"),
            vmem_limit_bytes=VMEM_LIMIT),
        name="mla_attn",
    )(mq, mq, mk, mk, mv)


CONV_PAD = 32
CONV_ROWS = 128
SUBLANES = 8
CONV_WIN_EXTRA = CONV_PAD


def _conv_kernel(u_ref, w_ref, b_ref, g_ref, beta_ref, o_ref, buf_ref, *, seq):
    buf_ref[0:CONV_PAD, :] = jnp.zeros((CONV_PAD, CONV_CH), F32)
    buf_ref[CONV_PAD:CONV_PAD + N_META, :] = u_ref[0, seq:seq + N_META, :]
    buf_ref[CONV_PAD + N_META:CONV_PAD + N_META + seq, :] = u_ref[0, 0:seq, :]
    w = w_ref[...]
    lead = CONV_PAD - (CONV_K - 1)

    def finish(acc):
        y = _layer_norm(acc + b_ref[...], g_ref[...], beta_ref[...])
        return (y * _sigmoid(y)).astype(o_ref.dtype)

    def conv_rows(first_pos, rows):
        win = buf_ref[pl.ds(first_pos, rows + CONV_WIN_EXTRA), :]
        acc = jnp.zeros((rows, CONV_CH), F32)
        for sh in range(SUBLANES):
            offs = [o for o in range(lead, lead + CONV_K) if o % SUBLANES == sh]
            shifted = win[sh:sh + rows + offs[-1] - sh, :]
            for o in offs:
                k = o - lead
                acc = acc + shifted[o - sh:o - sh + rows, :] * w[k:k + 1, :]
        return finish(acc)

    def body(c, _):
        r0 = pl.multiple_of(c * CONV_ROWS, CONV_ROWS)
        o_ref[0, pl.ds(r0, CONV_ROWS), :] = conv_rows(pl.multiple_of(r0 + N_META, SUBLANES), CONV_ROWS)
        return 0

    lax.fori_loop(0, seq // CONV_ROWS, body, 0)
    o_ref[0, seq:seq + N_META, :] = conv_rows(0, N_META)


def _conv_call(cu, conv_w, conv_b, ln_g, ln_b):
    b, lp, _ = cu.shape
    seq = lp - N_META
    const2 = lambda bi: (0, 0)
    return pl.pallas_call(
        functools.partial(_conv_kernel, seq=seq),
        out_shape=jax.ShapeDtypeStruct((b, lp, CONV_CH), BF16),
        grid=(b,),
        in_specs=[pl.BlockSpec((1, lp, CONV_CH), lambda bi: (bi, 0, 0)),
                  pl.BlockSpec((CONV_K, CONV_CH), const2),
                  pl.BlockSpec((1, CONV_CH), const2),
                  pl.BlockSpec((1, CONV_CH), const2),
                  pl.BlockSpec((1, CONV_CH), const2)],
        out_specs=pl.BlockSpec((1, lp, CONV_CH), lambda bi: (bi, 0, 0)),
        scratch_shapes=[pltpu.VMEM((CONV_PAD + lp, CONV_CH), F32)],
        compiler_params=pltpu.CompilerParams(dimension_semantics=("parallel",),
                                             vmem_limit_bytes=VMEM_LIMIT),
        name="conv",
    )(cu, conv_w, conv_b.reshape(1, -1), ln_g.reshape(1, -1), ln_b.reshape(1, -1))


def _mix_kernel(sb_ref, mla_ref, cv_ref, h_ref, gg_ref, wo_ref, lg_ref, lb_ref, rw_ref, rb_ref,
                h1_ref, idx_ref, gate_ref, rank_ref, cnt_ref, carry_ref, *, alpha, tm):
    step = pl.program_id(0)

    @pl.when(step == 0)
    def _():
        carry_ref[...] = jnp.zeros_like(carry_ref)

    gg = gg_ref[...]
    y = jnp.concatenate(
        [_rms_norm(sb_ref[...].astype(F32), gg[:, :SB_WIDTH]),
         _rms_norm(mla_ref[...].astype(F32), gg[:, SB_WIDTH:SB_WIDTH + MLA_WIDTH]),
         _rms_norm(cv_ref[...].astype(F32), gg[:, SB_WIDTH + MLA_WIDTH:])], axis=-1)
    mix = jnp.dot(y.astype(BF16), wo_ref[...], preferred_element_type=F32)
    h1 = _layer_norm(alpha * h_ref[...] + mix, lg_ref[...], lb_ref[...])
    h1_ref[...] = h1

    logits = jnp.dot(h1.astype(BF16), rw_ref[...], preferred_element_type=F32) + rb_ref[...]
    eiota = lax.broadcasted_iota(jnp.int32, (tm, N_EXPERTS), 1)
    lane = lax.broadcasted_iota(jnp.int32, (tm, LANES), 1)
    vals = logits
    sels, tops, idxs = [], [], []
    for _ in range(TOP_K):
        m = jnp.max(vals, axis=1, keepdims=True)
        idx = jnp.min(jnp.where(vals == m, eiota, N_EXPERTS), axis=1, keepdims=True)
        sel = eiota == idx
        vals = jnp.where(sel, -jnp.inf, vals)
        sels.append(sel)
        tops.append(m)
        idxs.append(idx)
    exps = [jnp.exp(t - tops[0]) for t in tops]
    denom = exps[0] + exps[1] + exps[2] + exps[3]

    chosen = jnp.zeros((tm, N_EXPERTS), F32)
    for sel in sels:
        chosen = chosen + jnp.where(sel, 1.0, 0.0)
    r = lax.broadcasted_iota(jnp.int32, (tm, tm), 0)
    c = lax.broadcasted_iota(jnp.int32, (tm, tm), 1)
    below = jnp.where(c < r, 1.0, 0.0).astype(BF16)
    earlier = jnp.dot(below, chosen.astype(BF16), preferred_element_type=F32) + carry_ref[...]

    idx_out = jnp.zeros((tm, LANES), jnp.int32)
    gate_out = jnp.zeros((tm, LANES), F32)
    rank_out = jnp.zeros((tm, LANES), jnp.int32)
    for k in range(TOP_K):
        rank_k = jnp.sum(jnp.where(sels[k], earlier, 0.0), axis=1, keepdims=True)
        idx_out = jnp.where(lane == k, idxs[k], idx_out)
        gate_out = jnp.where(lane == k, exps[k] / denom, gate_out)
        rank_out = jnp.where(lane == k, rank_k.astype(jnp.int32), rank_out)
    idx_ref[...] = idx_out
    gate_ref[...] = gate_out
    rank_ref[...] = rank_out

    carry_ref[...] = carry_ref[...] + jnp.sum(chosen, axis=0, keepdims=True)
    cnt_ref[...] = carry_ref[...]


def _mix_call(sb_out, mla_out, conv_out, h2d, grp_g, w_out_b, ln_g, ln_b, router_w_b, router_b,
              alpha, tm):
    t, d = h2d.shape
    row = lambda i: (i, 0)
    const2 = lambda i: (0, 0)
    outs = (jax.ShapeDtypeStruct((t, d), F32),
            jax.ShapeDtypeStruct((t, LANES), jnp.int32),
            jax.ShapeDtypeStruct((t, LANES), F32),
            jax.ShapeDtypeStruct((t, LANES), jnp.int32),
            jax.ShapeDtypeStruct((1, N_EXPERTS), F32))
    return pl.pallas_call(
        functools.partial(_mix_kernel, alpha=alpha, tm=tm),
        out_shape=outs,
        grid=(t // tm,),
        in_specs=[pl.BlockSpec((tm, SB_WIDTH), row),
                  pl.BlockSpec((tm, MLA_WIDTH), row),
                  pl.BlockSpec((tm, CONV_CH), row),
                  pl.BlockSpec((tm, d), row),
                  pl.BlockSpec((1, d), const2),
                  pl.BlockSpec((d, d), const2),
                  pl.BlockSpec((1, d), const2),
                  pl.BlockSpec((1, d), const2),
                  pl.BlockSpec((d, N_EXPERTS), const2),
                  pl.BlockSpec((1, N_EXPERTS), const2)],
        out_specs=(pl.BlockSpec((tm, d), row),
                   pl.BlockSpec((tm, LANES), row),
                   pl.BlockSpec((tm, LANES), row),
                   pl.BlockSpec((tm, LANES), row),
                   pl.BlockSpec((1, N_EXPERTS), const2)),
        scratch_shapes=[pltpu.VMEM((1, N_EXPERTS), F32)],
        compiler_params=pltpu.CompilerParams(dimension_semantics=("arbitrary",),
                                             vmem_limit_bytes=VMEM_LIMIT),
        name="mix_router",
    )(sb_out, mla_out, conv_out, h2d, grp_g.reshape(1, d), w_out_b, ln_g.reshape(1, d),
      ln_b.reshape(1, d), router_w_b, router_b.reshape(1, N_EXPERTS))


def _dispatch_kernel(pos_ref, h_ref, xs_in_ref, xs_ref, sem, *, tm):
    del xs_in_ref
    base = pl.program_id(0) * (tm * TOP_K)

    def row_copy(r, k):
        dst = pos_ref[base + r * TOP_K + k]
        return pltpu.make_async_copy(h_ref.at[pl.ds(r, 1), :], xs_ref.at[pl.ds(dst, 1), :], sem)

    def start(r, _):
        for k in range(TOP_K):
            row_copy(r, k).start()
        return 0

    def wait(r, _):
        for k in range(TOP_K):
            row_copy(r, k).wait()
        return 0

    lax.fori_loop(0, tm, start, 0)
    lax.fori_loop(0, tm, wait, 0)


def _dispatch_call(pos_flat, h2d, n_rows, tm):
    t, d = h2d.shape
    xs0 = jnp.zeros((n_rows, d), F32)
    return pl.pallas_call(
        functools.partial(_dispatch_kernel, tm=tm),
        out_shape=jax.ShapeDtypeStruct((n_rows, d), F32),
        grid_spec=pltpu.PrefetchScalarGridSpec(
            num_scalar_prefetch=1,
            grid=(t // tm,),
            in_specs=[pl.BlockSpec((tm, d), lambda i, pos: (i, 0)),
                      pl.BlockSpec(memory_space=pl.ANY)],
            out_specs=pl.BlockSpec(memory_space=pl.ANY),
            scratch_shapes=[pltpu.SemaphoreType.DMA(())]),
        input_output_aliases={2: 0},
        compiler_params=pltpu.CompilerParams(dimension_semantics=("arbitrary",),
                                             vmem_limit_bytes=VMEM_LIMIT),
        name="moe_dispatch",
    )(pos_flat, h2d, xs0)


def _expert_kernel(te_ref, nv_ref, x_ref, wgu_ref, bgu_ref, wd_ref, bd_ref, y_ref, *, d_ff):
    @pl.when(pl.program_id(0) < nv_ref[0])
    def _():
        x = x_ref[...].astype(BF16)
        gu = jnp.dot(x, wgu_ref[0], preferred_element_type=F32) + bgu_ref[0]
        g = jnp.minimum(gu[:, :d_ff], SWIGLU_LIMIT)
        up = jnp.clip(gu[:, d_ff:], -SWIGLU_LIMIT, SWIGLU_LIMIT)
        act = (up + 1.0) * (g * _sigmoid(SWIGLU_ALPHA * g))
        y_ref[...] = jnp.dot(act.astype(BF16), wd_ref[0], preferred_element_type=F32) + bd_ref[0]

    @pl.when(pl.program_id(0) >= nv_ref[0])
    def _():
        y_ref[...] = jnp.zeros_like(y_ref)


def _expert_call(tile_expert, n_valid, xs, wgu_b, bgu, wd_b, bd, tm):
    n_rows, d = xs.shape
    d_ff = wd_b.shape[1]
    n_tiles = n_rows // tm
    xrow = lambda i, te, nv: (jnp.minimum(i, nv[0] - 1), 0)
    wsel = lambda i, te, nv: (te[i], 0, 0)
    return pl.pallas_call(
        functools.partial(_expert_kernel, d_ff=d_ff),
        out_shape=jax.ShapeDtypeStruct((n_rows, d), F32),
        grid_spec=pltpu.PrefetchScalarGridSpec(
            num_scalar_prefetch=2,
            grid=(n_tiles,),
            in_specs=[pl.BlockSpec((tm, d), xrow),
                      pl.BlockSpec((1, d, 2 * d_ff), wsel),
                      pl.BlockSpec((1, 1, 2 * d_ff), wsel),
                      pl.BlockSpec((1, d_ff, d), wsel),
                      pl.BlockSpec((1, 1, d), wsel)],
            out_specs=pl.BlockSpec((tm, d), lambda i, te, nv: (i, 0))),
        compiler_params=pltpu.CompilerParams(dimension_semantics=("arbitrary",),
                                             vmem_limit_bytes=VMEM_LIMIT),
        name="moe_experts",
    )(tile_expert, n_valid, xs, wgu_b, bgu.reshape(N_EXPERTS, 1, -1), wd_b,
      bd.reshape(N_EXPERTS, 1, -1))


def _combine_kernel(pos_ref, ys_ref, gate_ref, h_ref, lg_ref, lb_ref, o_ref, buf_ref, sem,
                    *, alpha, tm):
    base = pl.program_id(0) * (tm * TOP_K)

    def row_copy(r, k):
        src = pos_ref[base + r * TOP_K + k]
        return pltpu.make_async_copy(ys_ref.at[pl.ds(src, 1), :], buf_ref.at[k, pl.ds(r, 1), :], sem)

    def start(r, _):
        for k in range(TOP_K):
            row_copy(r, k).start()
        return 0

    def wait(r, _):
        for k in range(TOP_K):
            row_copy(r, k).wait()
        return 0

    lax.fori_loop(0, tm, start, 0)
    lax.fori_loop(0, tm, wait, 0)
    gate = gate_ref[...]
    ffn = buf_ref[0] * gate[:, 0:1]
    for k in range(1, TOP_K):
        ffn = ffn + buf_ref[k] * gate[:, k:k + 1]
    o_ref[...] = _layer_norm(alpha * h_ref[...] + ffn, lg_ref[...], lb_ref[...])


def _combine_call(pos_flat, ys, gates, h2d, ln_g, ln_b, alpha, tm):
    t, d = h2d.shape
    row = lambda i, pos: (i, 0)
    const2 = lambda i, pos: (0, 0)
    return pl.pallas_call(
        functools.partial(_combine_kernel, alpha=alpha, tm=tm),
        out_shape=jax.ShapeDtypeStruct((t, d), F32),
        grid_spec=pltpu.PrefetchScalarGridSpec(
            num_scalar_prefetch=1,
            grid=(t // tm,),
            in_specs=[pl.BlockSpec(memory_space=pl.ANY),
                      pl.BlockSpec((tm, LANES), row),
                      pl.BlockSpec((tm, d), row),
                      pl.BlockSpec((1, d), const2),
                      pl.BlockSpec((1, d), const2)],
            out_specs=pl.BlockSpec((tm, d), row),
            scratch_shapes=[pltpu.VMEM((TOP_K, tm, d), F32),
                            pltpu.SemaphoreType.DMA(())]),
        compiler_params=pltpu.CompilerParams(dimension_semantics=("arbitrary",),
                                             vmem_limit_bytes=VMEM_LIMIT),
        name="moe_combine",
    )(pos_flat, ys, gates, h2d, ln_g.reshape(1, d), ln_b.reshape(1, d))


def _in_proj_columns():
    o_cq = 3 * SB_WIDTH
    o_ckv = o_cq + MLA_Q_RANK
    o_kpe = o_ckv + MLA_KV_RANK
    o_ca = o_kpe + MLA_ROPE_DIM
    o_cg = o_ca + CONV_CH
    reps = LANES // 2 // ROPE_HALF
    kpe = np.concatenate([np.tile(o_kpe + np.arange(ROPE_HALF), reps),
                          np.tile(o_kpe + ROPE_HALF + np.arange(ROPE_HALF), reps)])
    return np.concatenate([np.arange(o_cq), o_cq + np.arange(MLA_Q_RANK),
                           o_ckv + np.arange(MLA_KV_RANK), kpe,
                           o_ca + np.arange(CONV_CH), o_cg + np.arange(CONV_CH)])


def _uq_columns():
    per = MLA_NOPE_DIM + MLA_ROPE_DIM
    nope = np.concatenate([h * per + np.arange(MLA_NOPE_DIM) for h in range(MLA_HEADS)])
    rope = []
    grp = LANES // 2 // ROPE_HALF
    for g0 in range(0, MLA_HEADS, grp):
        for half in range(2):
            for h in range(g0, g0 + grp):
                rope.append(h * per + MLA_NOPE_DIM + half * ROPE_HALF + np.arange(ROPE_HALF))
    return np.concatenate([nope] + rope)


def _ukv_columns():
    per = MLA_NOPE_DIM + MLA_V_DIM
    kn = np.concatenate([h * per + np.arange(MLA_NOPE_DIM) for h in range(MLA_HEADS)])
    vv = np.concatenate([h * per + MLA_NOPE_DIM + np.arange(MLA_V_DIM) for h in range(MLA_HEADS)])
    return np.concatenate([kn, vv])


def _rope_tables(seq):
    lp = seq + N_META
    inv = 1.0 / (ROPE_THETA ** (jnp.arange(0, MLA_ROPE_DIM, 2, dtype=F32) / MLA_ROPE_DIM))
    pos = jnp.concatenate([N_META + jnp.arange(seq, dtype=F32), jnp.arange(N_META, dtype=F32)])
    ang = pos[:, None] * inv[None, :]
    reps = LANES // 2 // ROPE_HALF
    cos = jnp.tile(jnp.cos(ang), (1, 2 * reps))
    sin = jnp.tile(jnp.sin(ang), (1, reps))
    assert cos.shape == (lp, LANES)
    return cos, jnp.concatenate([-sin, sin], axis=1)


ATTN_TQ = 256
EXPERT_TM = 256
ROUTE_TM = 128


def kernel(x, meta_tokens, ln_in_g, ln_in_b, w_in, q_norm_g, w_uq, kv_norm_g, w_ukv, conv_w, conv_b,
           conv_ln_g, conv_ln_b, grp_norm_g, w_out, ln_mix_g, ln_mix_b, router_w, router_b,
           w_gate_up, b_gate_up, w_down, b_down, ln_ffn_g, ln_ffn_b):
    b, seq, d = x.shape
    depth = w_in.shape[0]
    lp = seq + N_META
    t = b * lp
    alpha = float((2 * depth) ** 0.25)
    tq = min(ATTN_TQ, seq)
    assert seq % tq == 0 and tq % LANES == 0 and seq % CONV_ROWS == 0
    tm_tok = _row_tile(t, 512, LANES)
    tm_seq = _row_tile(lp, 1024)
    tm_route = _row_tile(t, ROUTE_TM, 8)
    n_assign = t * TOP_K
    n_tiles = -(-n_assign // EXPERT_TM) + N_EXPERTS
    n_rows = n_tiles * EXPERT_TM

    meta = jnp.broadcast_to(meta_tokens[None].astype(x.dtype), (b, N_META, d))
    xin = jnp.concatenate([x, meta], axis=1).reshape(t, d)
    h = _ln_call(xin, ln_in_g, ln_in_b, tm_tok)
    cos_t, sin_t = _rope_tables(seq)

    in_cols = _in_proj_columns()
    in_scale = np.ones((IN_TOTAL,), np.float32)
    in_scale[:SB_WIDTH] = SB_HEAD_DIM ** -0.5
    uq_cols, ukv_cols = _uq_columns(), _ukv_columns()

    for l in range(depth):
        w_in_p = (w_in[l][:, in_cols] * in_scale).astype(BF16)
        sbqkv, mq, mk, mv, cu = _inproj_call(
            h.reshape(b, lp, d), w_in_p, q_norm_g[l], w_uq[l][:, uq_cols].astype(BF16),
            kv_norm_g[l], w_ukv[l][:, ukv_cols].astype(BF16), cos_t, sin_t, tm_seq)
        sb_out = _sb_call(sbqkv, tq)
        mla_out = _mla_call(mq, mk, mv, tq)
        conv_out = _conv_call(cu, conv_w[l], conv_b[l], conv_ln_g[l], conv_ln_b[l])
        h1, idx, gates, rank, counts = _mix_call(
            sb_out.reshape(t, -1), mla_out.reshape(t, -1), conv_out.reshape(t, -1), h,
            grp_norm_g[l], w_out[l].astype(BF16), ln_mix_g[l], ln_mix_b[l],
            router_w[l].astype(BF16), router_b[l], alpha, tm_tok)

        cnt = counts[0].astype(jnp.int32)
        tiles_e = (cnt + EXPERT_TM - 1) // EXPERT_TM
        tile_end = jnp.cumsum(tiles_e)
        tile_start = tile_end - tiles_e
        pos = tile_start[idx[:, :TOP_K]] * EXPERT_TM + rank[:, :TOP_K]
        pos_flat = pos.reshape(-1).astype(jnp.int32)
        n_valid = tile_end[-1:].astype(jnp.int32)
        tile_ids = jnp.minimum(jnp.arange(n_tiles, dtype=jnp.int32), n_valid[0] - 1)
        tile_expert = jnp.minimum(jnp.searchsorted(tile_end, tile_ids, side="right"),
                                  N_EXPERTS - 1).astype(jnp.int32)

        xs = _dispatch_call(pos_flat, h1, n_rows, tm_route)
        ys = _expert_call(tile_expert, n_valid, xs, w_gate_up[l].astype(BF16), b_gate_up[l],
                          w_down[l].astype(BF16), b_down[l], EXPERT_TM)
        h = _combine_call(pos_flat, ys, gates, h1, ln_ffn_g[l], ln_ffn_b[l], alpha, tm_route)

    return h.reshape(b, lp, d)[:, :seq, :]
```

```python
import functools

import jax
import jax.numpy as jnp
import numpy as np
from jax import lax
from jax.experimental import pallas as pl
from jax.experimental.pallas import tpu as pltpu

F32 = jnp.float32
BF16 = jnp.bfloat16

N_META = 16
SB_HEADS = 4
SB_HEAD_DIM = 64
SB_WIDTH = SB_HEADS * SB_HEAD_DIM
MLA_HEADS = 8
MLA_NOPE_DIM = 64
MLA_ROPE_DIM = 32
MLA_V_DIM = 64
MLA_Q_RANK = 256
MLA_KV_RANK = 128
MLA_WIDTH = MLA_HEADS * MLA_V_DIM
ROPE_THETA = 10000.0
CONV_CH = 256
CONV_K = 31
N_EXPERTS = 32
TOP_K = 4
SWIGLU_LIMIT = 7.0
SWIGLU_ALPHA = 1.702
LN_EPS = 1e-5
RMS_EPS = 1e-6

LANES = 128
HEAD_PAIR = LANES // SB_HEAD_DIM
ROPE_HALF = MLA_ROPE_DIM // 2
VMEM_LIMIT = 48 * 1024 * 1024
NEG_BIG = -1e30
LOG2E = 1.4426950408889634

IN_SB = 3 * SB_WIDTH
IN_CQ = IN_SB
IN_CKV = IN_CQ + MLA_Q_RANK
IN_KPE = IN_CKV + MLA_KV_RANK
IN_CA = IN_KPE + LANES
IN_CG = IN_CA + CONV_CH
IN_TOTAL = IN_CG + CONV_CH
MQ_W = MLA_HEADS * MLA_NOPE_DIM + 2 * LANES
MK_W = MLA_HEADS * MLA_NOPE_DIM + LANES


def _row_tile(n, cap, mult=16):
    best = None
    for t in range(mult, min(n, cap) + 1, mult):
        if n % t == 0:
            best = t
    assert best is not None, (n, cap, mult)
    return best


def _layer_norm(x, g, b):
    mu = jnp.mean(x, axis=-1, keepdims=True)
    xc = x - mu
    var = jnp.mean(xc * xc, axis=-1, keepdims=True)
    return xc * lax.rsqrt(var + LN_EPS) * g + b


def _rms_norm(x, g):
    return x * lax.rsqrt(jnp.mean(x * x, axis=-1, keepdims=True) + RMS_EPS) * g


def _sigmoid(x):
    return 1.0 / (1.0 + jnp.exp(-x))


def _ln_kernel(x_ref, g_ref, b_ref, o_ref):
    o_ref[...] = _layer_norm(x_ref[...], g_ref[...], b_ref[...])


def _ln_call(x2d, g, b, tm):
    t, d = x2d.shape
    return pl.pallas_call(
        _ln_kernel,
        out_shape=jax.ShapeDtypeStruct((t, d), F32),
        grid=(t // tm,),
        in_specs=[pl.BlockSpec((tm, d), lambda i: (i, 0)),
                  pl.BlockSpec((1, d), lambda i: (0, 0)),
                  pl.BlockSpec((1, d), lambda i: (0, 0))],
        out_specs=pl.BlockSpec((tm, d), lambda i: (i, 0)),
        compiler_params=pltpu.CompilerParams(dimension_semantics=("parallel",),
                                             vmem_limit_bytes=VMEM_LIMIT),
        name="ln_in",
    )(x2d, g.reshape(1, d), b.reshape(1, d))


def _inproj_kernel(h_ref, w_ref, qg_ref, wuq_ref, kvg_ref, wukv_ref, cos_ref, sin_ref,
                   sbqk_ref, sbv_ref, mq_ref, mk_ref, mv_ref, cu_ref):
    h = h_ref[0].astype(BF16)
    proj = jnp.dot(h, w_ref[...], preferred_element_type=F32)
    sbqk_ref[0, :, :SB_WIDTH] = (proj[:, :SB_WIDTH] * (SB_HEAD_DIM ** -0.5 * LOG2E)).astype(BF16)
    sbqk_ref[0, :, SB_WIDTH:] = proj[:, SB_WIDTH:2 * SB_WIDTH].astype(BF16)
    sbv_ref[0] = proj[:, 2 * SB_WIDTH:IN_SB].astype(BF16)
    cos = cos_ref[...]
    sin = sin_ref[...]

    def rot(x):
        return x * cos + pltpu.roll(x, LANES // 2, 1) * sin

    cq = _rms_norm(proj[:, IN_CQ:IN_CKV], qg_ref[...])
    qm = jnp.dot(cq.astype(BF16), wuq_ref[...], preferred_element_type=F32)
    qm = qm * ((MLA_NOPE_DIM + MLA_ROPE_DIM) ** -0.5 * LOG2E)
    nope_w = MLA_HEADS * MLA_NOPE_DIM
    mq_ref[0, :, :nope_w] = qm[:, :nope_w].astype(BF16)
    mq_ref[0, :, nope_w:nope_w + LANES] = rot(qm[:, nope_w:nope_w + LANES]).astype(BF16)
    mq_ref[0, :, nope_w + LANES:] = rot(qm[:, nope_w + LANES:]).astype(BF16)

    ckv = _rms_norm(proj[:, IN_CKV:IN_KPE], kvg_ref[...])
    kv = jnp.dot(ckv.astype(BF16), wukv_ref[...], preferred_element_type=F32)
    mk_ref[0, :, :nope_w] = kv[:, :nope_w].astype(BF16)
    mk_ref[0, :, nope_w:] = rot(proj[:, IN_KPE:IN_CA]).astype(BF16)
    mv_ref[0] = kv[:, nope_w:].astype(BF16)

    cu_ref[0] = proj[:, IN_CA:IN_CG] * _sigmoid(proj[:, IN_CG:IN_TOTAL])


def _inproj_call(h3, w_in_p, qg, wuq_p, kvg, wukv_p, cos_t, sin_t, tm):
    b, lp, d = h3.shape
    nt = lp // tm
    const2 = lambda bi, ti: (0, 0)
    tok3 = lambda bi, ti: (bi, ti, 0)
    outs = (jax.ShapeDtypeStruct((b, lp, 2 * SB_WIDTH), BF16),
            jax.ShapeDtypeStruct((b, lp, SB_WIDTH), BF16),
            jax.ShapeDtypeStruct((b, lp, MQ_W), BF16),
            jax.ShapeDtypeStruct((b, lp, MK_W), BF16),
            jax.ShapeDtypeStruct((b, lp, MLA_WIDTH), BF16),
            jax.ShapeDtypeStruct((b, lp, CONV_CH), F32))
    return pl.pallas_call(
        _inproj_kernel,
        out_shape=outs,
        grid=(b, nt),
        in_specs=[pl.BlockSpec((1, tm, d), tok3),
                  pl.BlockSpec(w_in_p.shape, const2),
                  pl.BlockSpec((1, MLA_Q_RANK), const2),
                  pl.BlockSpec(wuq_p.shape, const2),
                  pl.BlockSpec((1, MLA_KV_RANK), const2),
                  pl.BlockSpec(wukv_p.shape, const2),
                  pl.BlockSpec((tm, LANES), lambda bi, ti: (ti, 0)),
                  pl.BlockSpec((tm, LANES), lambda bi, ti: (ti, 0))],
        out_specs=tuple(pl.BlockSpec((1, tm, s.shape[2]), tok3) for s in outs),
        compiler_params=pltpu.CompilerParams(dimension_semantics=("parallel", "parallel"),
                                             vmem_limit_bytes=VMEM_LIMIT),
        name="in_proj",
    )(h3, w_in_p, qg.reshape(1, -1), wuq_p, kvg.reshape(1, -1), wukv_p, cos_t, sin_t)


def _log_stay2(z):
    nz = -z
    return jnp.minimum(nz, 0.0) - jnp.log2(1.0 + jnp.exp2(jnp.minimum(z, nz)))


def _split_bf16(x):
    hi = x.astype(BF16)
    return hi, (x - hi.astype(F32)).astype(BF16)


def _sb_kernel(q_ref, k_ref, vt_ref, vmeta_ref, o_ref, acc_ref, c_ref, *, tq, seq):
    i = pl.program_id(1)
    nq = seq // tq
    npair = SB_HEADS // HEAD_PAIR
    lane = lax.broadcasted_iota(jnp.int32, (1, LANES), 1)
    head_sel = (lane < SB_HEAD_DIM, lane >= SB_HEAD_DIM)
    top_rows = lax.broadcasted_iota(jnp.int32, (LANES, 1), 0) < SB_HEAD_DIM
    meta = pl.ds(seq, N_META)

    def iota2(rows, cols):
        return (lax.broadcasted_iota(jnp.int32, (rows, cols), 0),
                lax.broadcasted_iota(jnp.int32, (rows, cols), 1))

    def tri_down(w):
        r, c = iota2(w, w)
        return jnp.where(c >= r, 1.0, 0.0).astype(BF16)

    def tile_all(q, krows, mask, ut, first):
        zs = []
        for p in range(npair):
            cols = slice(p * LANES, (p + 1) * LANES)
            kt = k_ref[0, krows, cols]
            for hd in range(HEAD_PAIR):
                qh = jnp.where(head_sel[hd], q[:, cols], jnp.zeros((1, 1), BF16))
                zs.append(lax.dot_general(kt, qh, (((1,), (1,)), ((), ())),
                                          preferred_element_type=F32))
        splits = []
        for h in range(SB_HEADS):
            ls = _log_stay2(zs[h])
            if mask is not None:
                ls = jnp.where(mask, ls, 0.0)
            splits.append(_split_bf16(ls))
        cums = [jnp.dot(ut, hi, preferred_element_type=F32) + jnp.dot(ut, lo, preferred_element_type=F32)
                for hi, lo in splits]
        ws = []
        for h in range(SB_HEADS):
            lw = zs[h] + cums[h] if first else zs[h] + cums[h] + c_ref[h]
            w = jnp.exp2(lw)
            if mask is not None:
                w = jnp.where(mask, w, 0.0)
            ws.append(w.astype(BF16))
            c_ref[h] = cums[h][0:1, :] if first else c_ref[h] + cums[h][0:1, :]
        for p in range(npair):
            vt = vt_ref[0, p * LANES:(p + 1) * LANES, krows]
            res = [jnp.dot(vt, ws[p * HEAD_PAIR + hd], preferred_element_type=F32)
                   for hd in range(HEAD_PAIR)]
            contrib = jnp.where(top_rows, res[0], res[1])
            acc_ref[p] = contrib if first else acc_ref[p] + contrib

    @pl.when(i < nq)
    def _():
        q = q_ref[0]
        ut = tri_down(tq)
        r, c = iota2(tq, tq)
        tile_all(q, pl.ds(pl.multiple_of(i * tq, tq), tq), r < c, ut, True)

        def body(t, carry):
            tile_all(q, pl.ds(pl.multiple_of((i - 1 - t) * tq, tq), tq), None, ut, False)
            return carry

        lax.fori_loop(0, i, body, 0)
        tile_all(q, meta, None, tri_down(N_META), False)
        for p in range(npair):
            o_ref[0, :, p * LANES:(p + 1) * LANES] = acc_ref[p].T.astype(o_ref.dtype)

    @pl.when(i == nq)
    def _():
        r, c = iota2(N_META, N_META)
        mask = c < r
        ut = jnp.where(r >= c, 1.0, 0.0).astype(BF16)
        for p in range(npair):
            cols = slice(p * LANES, (p + 1) * LANES)
            q = q_ref[0, 0:N_META, cols]
            kt = k_ref[0, meta, cols]
            vm = vmeta_ref[0, :, cols]
            res = []
            for hd in range(HEAD_PAIR):
                qh = jnp.where(head_sel[hd], q, jnp.zeros((1, 1), BF16))
                z = lax.dot_general(qh, kt, (((1,), (1,)), ((), ())), preferred_element_type=F32)
                hi, lo = _split_bf16(jnp.where(mask, _log_stay2(z), 0.0))
                cum = (jnp.dot(hi, ut, preferred_element_type=F32)
                       + jnp.dot(lo, ut, preferred_element_type=F32))
                w = jnp.where(mask, jnp.exp2(z + cum), 0.0)
                res.append(jnp.dot(w.astype(BF16), vm, preferred_element_type=F32))
            o_ref[0, 0:N_META, cols] = jnp.where(head_sel[0], res[0], res[1]).astype(o_ref.dtype)


def _sb_call(sbqk, sbv_t, sbv_meta, tq):
    b, lp, _ = sbqk.shape
    seq = lp - N_META
    nq = seq // tq
    return pl.pallas_call(
        functools.partial(_sb_kernel, tq=tq, seq=seq),
        out_shape=jax.ShapeDtypeStruct((b, lp, SB_WIDTH), BF16),
        grid=(b, nq + 1),
        in_specs=[pl.BlockSpec((1, tq, SB_WIDTH), lambda bi, i: (bi, i, 0)),
                  pl.BlockSpec((1, lp, SB_WIDTH), lambda bi, i: (bi, 0, 1)),
                  pl.BlockSpec((1, SB_WIDTH, lp), lambda bi, i: (bi, 0, 0)),
                  pl.BlockSpec((1, N_META, SB_WIDTH), lambda bi, i: (bi, 0, 0))],
        out_specs=pl.BlockSpec((1, tq, SB_WIDTH), lambda bi, i: (bi, i, 0)),
        scratch_shapes=[pltpu.VMEM((SB_HEADS // HEAD_PAIR, LANES, tq), F32),
                        pltpu.VMEM((SB_HEADS, 1, tq), F32)],
        compiler_params=pltpu.CompilerParams(
            dimension_semantics=("parallel", "arbitrary"),
            vmem_limit_bytes=VMEM_LIMIT),
        name="sb_attn",
    )(sbqk, sbqk, sbv_t, sbv_meta)


def _mla_kernel(q_ref, k_ref, vt_ref, vmeta_ref, o_ref, acc_ref, m_ref, *, tq, seq):
    i = pl.program_id(1)
    nq = seq // tq
    npair = MLA_HEADS // HEAD_PAIR
    nope_w = MLA_HEADS * MLA_NOPE_DIM
    grp = LANES // 2 // ROPE_HALF
    lane = lax.broadcasted_iota(jnp.int32, (1, LANES), 1)
    lane2 = lax.broadcasted_iota(jnp.int32, (1, 2 * LANES), 1)
    top_rows = lax.broadcasted_iota(jnp.int32, (LANES, 1), 0) < MLA_V_DIM
    meta = pl.ds(seq, N_META)
    one = jnp.ones((1, 1), BF16)

    def head_lanes(h):
        n0 = (h % HEAD_PAIR) * MLA_NOPE_DIM
        r0 = LANES + (h % grp) * ROPE_HALF
        r1 = r0 + LANES // 2
        return (((lane2 >= n0) & (lane2 < n0 + MLA_NOPE_DIM))
                | ((lane2 >= r0) & (lane2 < r0 + ROPE_HALF))
                | ((lane2 >= r1) & (lane2 < r1 + ROPE_HALF)))

    def qcat(q, p):
        rope0 = nope_w + (p * HEAD_PAIR // grp) * LANES
        return jnp.concatenate([q[:, p * LANES:(p + 1) * LANES], q[:, rope0:rope0 + LANES]], axis=-1)

    def kcat(krows, p):
        return jnp.concatenate([k_ref[0, krows, p * LANES:(p + 1) * LANES],
                                k_ref[0, krows, nope_w:nope_w + LANES]], axis=-1)

    def iota2(rows, cols):
        return (lax.broadcasted_iota(jnp.int32, (rows, cols), 0),
                lax.broadcasted_iota(jnp.int32, (rows, cols), 1))

    def tile_all(q, krows, mask, first):
        scores = []
        for p in range(npair):
            kc = kcat(krows, p)
            qc = qcat(q, p)
            for hd in range(HEAD_PAIR):
                qh = jnp.where(head_lanes(p * HEAD_PAIR + hd), qc, jnp.zeros((1, 1), BF16))
                scores.append(lax.dot_general(kc, qh, (((1,), (1,)), ((), ())),
                                              preferred_element_type=F32))
        pexps, alphas = [], []
        for h in range(MLA_HEADS):
            s = scores[h] if mask is None else jnp.where(mask, scores[h], NEG_BIG)
            mx = jnp.max(s, axis=0, keepdims=True)
            if first:
                m_new = mx
                alphas.append(None)
            else:
                m_old = m_ref[h]
                m_new = jnp.maximum(m_old, mx)
                alphas.append(jnp.exp2(m_old - m_new))
            m_ref[h] = m_new
            pexps.append(jnp.exp2(s - m_new).astype(BF16))
        for h in range(MLA_HEADS):
            p, hd = divmod(h, HEAD_PAIR)
            vt = vt_ref[0, p * LANES:(p + 1) * LANES, krows]
            vte = jnp.where(top_rows, vt, one) if hd == 0 else jnp.where(top_rows, one, vt)
            pv = jnp.dot(vte, pexps[h], preferred_element_type=F32)
            acc_ref[h] = pv if first else acc_ref[h] * alphas[h] + pv

    @pl.when(i < nq)
    def _():
        q = q_ref[0]
        r, c = iota2(tq, tq)
        tile_all(q, pl.ds(pl.multiple_of(i * tq, tq), tq), r <= c, True)

        def body(t, carry):
            tile_all(q, pl.ds(pl.multiple_of(t * tq, tq), tq), None, False)
            return carry

        lax.fori_loop(0, i, body, 0)
        tile_all(q, meta, None, False)
        for p in range(npair):
            a = acc_ref[p * HEAD_PAIR]
            b = acc_ref[p * HEAD_PAIR + 1]
            out_t = jnp.where(top_rows, a * (1.0 / a[MLA_V_DIM:MLA_V_DIM + 1, :]), b * (1.0 / b[0:1, :]))
            o_ref[0, :, p * LANES:(p + 1) * LANES] = out_t.T.astype(o_ref.dtype)

    @pl.when(i == nq)
    def _():
        q = q_ref[0, 0:N_META, :]
        r, c = iota2(N_META, N_META)
        for p in range(npair):
            kc = kcat(meta, p)
            qc = qcat(q, p)
            vm = vmeta_ref[0, :, p * LANES:(p + 1) * LANES]
            res = []
            for hd in range(HEAD_PAIR):
                qh = jnp.where(head_lanes(p * HEAD_PAIR + hd), qc, jnp.zeros((1, 1), BF16))
                s = lax.dot_general(qh, kc, (((1,), (1,)), ((), ())), preferred_element_type=F32)
                s = jnp.where(c <= r, s, NEG_BIG)
                pexp = jnp.exp2(s - jnp.max(s, axis=1, keepdims=True))
                pv = jnp.dot(pexp.astype(BF16), vm, preferred_element_type=F32)
                res.append(pv * (1.0 / jnp.sum(pexp, axis=1, keepdims=True)))
            o_ref[0, 0:N_META, p * LANES:(p + 1) * LANES] = jnp.where(
                lane < MLA_V_DIM, res[0], res[1]).astype(o_ref.dtype)


def _mla_call(mq, mk, mv_t, mv_meta, tq):
    b, lp, _ = mq.shape
    seq = lp - N_META
    nq = seq // tq
    return pl.pallas_call(
        functools.partial(_mla_kernel, tq=tq, seq=seq),
        out_shape=jax.ShapeDtypeStruct((b, lp, MLA_WIDTH), BF16),
        grid=(b, nq + 1),
        in_specs=[pl.BlockSpec((1, tq, MQ_W), lambda bi, i: (bi, i, 0)),
                  pl.BlockSpec((1, lp, MK_W), lambda bi, i: (bi, 0, 0)),
                  pl.BlockSpec((1, MLA_WIDTH, lp), lambda bi, i: (bi, 0, 0)),
                  pl.BlockSpec((1, N_META, MLA_WIDTH), lambda bi, i: (bi, 0, 0))],
        out_specs=pl.BlockSpec((1, tq, MLA_WIDTH), lambda bi, i: (bi, i, 0)),
        scratch_shapes=[pltpu.VMEM((MLA_HEADS, LANES, tq), F32),
                        pltpu.VMEM((MLA_HEADS, 1, tq), F32)],
        compiler_params=pltpu.CompilerParams(
            dimension_semantics=("parallel", "arbitrary"),
            vmem_limit_bytes=VMEM_LIMIT),
        name="mla_attn",
    )(mq, mk, mv_t, mv_meta)


CONV_PAD = 32
CONV_ROWS = 128
SUBLANES = 8
CONV_WIN_EXTRA = CONV_PAD


def _conv_kernel(u_ref, w_ref, b_ref, g_ref, beta_ref, o_ref, buf_ref, *, seq):
    buf_ref[0:CONV_PAD, :] = jnp.zeros((CONV_PAD, CONV_CH), F32)
    buf_ref[CONV_PAD:CONV_PAD + N_META, :] = u_ref[0, seq:seq + N_META, :]
    buf_ref[CONV_PAD + N_META:CONV_PAD + N_META + seq, :] = u_ref[0, 0:seq, :]
    w = w_ref[...]
    lead = CONV_PAD - (CONV_K - 1)

    def finish(acc):
        y = _layer_norm(acc + b_ref[...], g_ref[...], beta_ref[...])
        return (y * _sigmoid(y)).astype(o_ref.dtype)

    def conv_rows(first_pos, rows):
        win = buf_ref[pl.ds(first_pos, rows + CONV_WIN_EXTRA), :]
        acc = jnp.zeros((rows, CONV_CH), F32)
        for sh in range(SUBLANES):
            offs = [o for o in range(lead, lead + CONV_K) if o % SUBLANES == sh]
            shifted = win[sh:sh + rows + offs[-1] - sh, :]
            for o in offs:
                k = o - lead
                acc = acc + shifted[o - sh:o - sh + rows, :] * w[k:k + 1, :]
        return finish(acc)

    def body(c, _):
        r0 = pl.multiple_of(c * CONV_ROWS, CONV_ROWS)
        o_ref[0, pl.ds(r0, CONV_ROWS), :] = conv_rows(pl.multiple_of(r0 + N_META, SUBLANES), CONV_ROWS)
        return 0

    lax.fori_loop(0, seq // CONV_ROWS, body, 0)
    o_ref[0, seq:seq + N_META, :] = conv_rows(0, N_META)


def _conv_call(cu, conv_w, conv_b, ln_g, ln_b):
    b, lp, _ = cu.shape
    seq = lp - N_META
    const2 = lambda bi: (0, 0)
    return pl.pallas_call(
        functools.partial(_conv_kernel, seq=seq),
        out_shape=jax.ShapeDtypeStruct((b, lp, CONV_CH), BF16),
        grid=(b,),
        in_specs=[pl.BlockSpec((1, lp, CONV_CH), lambda bi: (bi, 0, 0)),
                  pl.BlockSpec((CONV_K, CONV_CH), const2),
                  pl.BlockSpec((1, CONV_CH), const2),
                  pl.BlockSpec((1, CONV_CH), const2),
                  pl.BlockSpec((1, CONV_CH), const2)],
        out_specs=pl.BlockSpec((1, lp, CONV_CH), lambda bi: (bi, 0, 0)),
        scratch_shapes=[pltpu.VMEM((CONV_PAD + lp, CONV_CH), F32)],
        compiler_params=pltpu.CompilerParams(dimension_semantics=("parallel",),
                                             vmem_limit_bytes=VMEM_LIMIT),
        name="conv",
    )(cu, conv_w, conv_b.reshape(1, -1), ln_g.reshape(1, -1), ln_b.reshape(1, -1))


def _mix_kernel(sb_ref, mla_ref, cv_ref, h_ref, gg_ref, wo_ref, lg_ref, lb_ref, rw_ref, rb_ref,
                h1_ref, idx_ref, gate_ref, rank_ref, cnt_ref, carry_ref, *, alpha, tm):
    step = pl.program_id(0)

    @pl.when(step == 0)
    def _():
        carry_ref[...] = jnp.zeros_like(carry_ref)

    gg = gg_ref[...]
    y = jnp.concatenate(
        [_rms_norm(sb_ref[...].astype(F32), gg[:, :SB_WIDTH]),
         _rms_norm(mla_ref[...].astype(F32), gg[:, SB_WIDTH:SB_WIDTH + MLA_WIDTH]),
         _rms_norm(cv_ref[...].astype(F32), gg[:, SB_WIDTH + MLA_WIDTH:])], axis=-1)
    mix = jnp.dot(y.astype(BF16), wo_ref[...], preferred_element_type=F32)
    h1 = _layer_norm(alpha * h_ref[...] + mix, lg_ref[...], lb_ref[...])
    h1_ref[...] = h1

    logits = jnp.dot(h1.astype(BF16), rw_ref[...], preferred_element_type=F32) + rb_ref[...]
    eiota = lax.broadcasted_iota(jnp.int32, (tm, N_EXPERTS), 1)
    lane = lax.broadcasted_iota(jnp.int32, (tm, LANES), 1)
    vals = logits
    sels, tops, idxs = [], [], []
    for _ in range(TOP_K):
        m = jnp.max(vals, axis=1, keepdims=True)
        idx = jnp.min(jnp.where(vals == m, eiota, N_EXPERTS), axis=1, keepdims=True)
        sel = eiota == idx
        vals = jnp.where(sel, -jnp.inf, vals)
        sels.append(sel)
        tops.append(m)
        idxs.append(idx)
    exps = [jnp.exp(t - tops[0]) for t in tops]
    denom = exps[0] + exps[1] + exps[2] + exps[3]

    chosen = jnp.zeros((tm, N_EXPERTS), F32)
    for sel in sels:
        chosen = chosen + jnp.where(sel, 1.0, 0.0)
    r = lax.broadcasted_iota(jnp.int32, (tm, tm), 0)
    c = lax.broadcasted_iota(jnp.int32, (tm, tm), 1)
    below = jnp.where(c < r, 1.0, 0.0).astype(BF16)
    earlier = jnp.dot(below, chosen.astype(BF16), preferred_element_type=F32) + carry_ref[...]

    idx_out = jnp.zeros((tm, LANES), jnp.int32)
    gate_out = jnp.zeros((tm, LANES), F32)
    rank_out = jnp.zeros((tm, LANES), jnp.int32)
    for k in range(TOP_K):
        rank_k = jnp.sum(jnp.where(sels[k], earlier, 0.0), axis=1, keepdims=True)
        idx_out = jnp.where(lane == k, idxs[k], idx_out)
        gate_out = jnp.where(lane == k, exps[k] / denom, gate_out)
        rank_out = jnp.where(lane == k, rank_k.astype(jnp.int32), rank_out)
    idx_ref[...] = idx_out
    gate_ref[...] = gate_out
    rank_ref[...] = rank_out

    carry_ref[...] = carry_ref[...] + jnp.sum(chosen, axis=0, keepdims=True)
    cnt_ref[...] = carry_ref[...]


def _mix_call(sb_out, mla_out, conv_out, h2d, grp_g, w_out_b, ln_g, ln_b, router_w_b, router_b,
              alpha, tm):
    t, d = h2d.shape
    row = lambda i: (i, 0)
    const2 = lambda i: (0, 0)
    outs = (jax.ShapeDtypeStruct((t, d), F32),
            jax.ShapeDtypeStruct((t, LANES), jnp.int32),
            jax.ShapeDtypeStruct((t, LANES), F32),
            jax.ShapeDtypeStruct((t, LANES), jnp.int32),
            jax.ShapeDtypeStruct((1, N_EXPERTS), F32))
    return pl.pallas_call(
        functools.partial(_mix_kernel, alpha=alpha, tm=tm),
        out_shape=outs,
        grid=(t // tm,),
        in_specs=[pl.BlockSpec((tm, SB_WIDTH), row),
                  pl.BlockSpec((tm, MLA_WIDTH), row),
                  pl.BlockSpec((tm, CONV_CH), row),
                  pl.BlockSpec((tm, d), row),
                  pl.BlockSpec((1, d), const2),
                  pl.BlockSpec((d, d), const2),
                  pl.BlockSpec((1, d), const2),
                  pl.BlockSpec((1, d), const2),
                  pl.BlockSpec((d, N_EXPERTS), const2),
                  pl.BlockSpec((1, N_EXPERTS), const2)],
        out_specs=(pl.BlockSpec((tm, d), row),
                   pl.BlockSpec((tm, LANES), row),
                   pl.BlockSpec((tm, LANES), row),
                   pl.BlockSpec((tm, LANES), row),
                   pl.BlockSpec((1, N_EXPERTS), const2)),
        scratch_shapes=[pltpu.VMEM((1, N_EXPERTS), F32)],
        compiler_params=pltpu.CompilerParams(dimension_semantics=("arbitrary",),
                                             vmem_limit_bytes=VMEM_LIMIT),
        name="mix_router",
    )(sb_out, mla_out, conv_out, h2d, grp_g.reshape(1, d), w_out_b, ln_g.reshape(1, d),
      ln_b.reshape(1, d), router_w_b, router_b.reshape(1, N_EXPERTS))


def _dispatch_kernel(pos_ref, h_ref, xs_in_ref, xs_ref, sem, *, tm):
    del xs_in_ref
    base = pl.program_id(0) * (tm * TOP_K)

    def row_copy(r, k):
        dst = pos_ref[base + r * TOP_K + k]
        return pltpu.make_async_copy(h_ref.at[pl.ds(r, 1), :], xs_ref.at[pl.ds(dst, 1), :], sem)

    def start(r, _):
        for k in range(TOP_K):
            row_copy(r, k).start()
        return 0

    def wait(r, _):
        for k in range(TOP_K):
            row_copy(r, k).wait()
        return 0

    lax.fori_loop(0, tm, start, 0)
    lax.fori_loop(0, tm, wait, 0)


def _dispatch_call(pos_flat, h2d, n_rows, tm):
    t, d = h2d.shape
    xs0 = jnp.zeros((n_rows, d), F32)
    return pl.pallas_call(
        functools.partial(_dispatch_kernel, tm=tm),
        out_shape=jax.ShapeDtypeStruct((n_rows, d), F32),
        grid_spec=pltpu.PrefetchScalarGridSpec(
            num_scalar_prefetch=1,
            grid=(t // tm,),
            in_specs=[pl.BlockSpec((tm, d), lambda i, pos: (i, 0)),
                      pl.BlockSpec(memory_space=pl.ANY)],
            out_specs=pl.BlockSpec(memory_space=pl.ANY),
            scratch_shapes=[pltpu.SemaphoreType.DMA(())]),
        input_output_aliases={2: 0},
        compiler_params=pltpu.CompilerParams(dimension_semantics=("arbitrary",),
                                             vmem_limit_bytes=VMEM_LIMIT),
        name="moe_dispatch",
    )(pos_flat, h2d, xs0)


def _expert_kernel(te_ref, nv_ref, x_ref, wgu_ref, bgu_ref, wd_ref, bd_ref, y_ref, *, d_ff):
    @pl.when(pl.program_id(0) < nv_ref[0])
    def _():
        x = x_ref[...].astype(BF16)
        gu = jnp.dot(x, wgu_ref[0], preferred_element_type=F32) + bgu_ref[0]
        g = jnp.minimum(gu[:, :d_ff], SWIGLU_LIMIT)
        up = jnp.clip(gu[:, d_ff:], -SWIGLU_LIMIT, SWIGLU_LIMIT)
        act = (up + 1.0) * (g * _sigmoid(SWIGLU_ALPHA * g))
        y_ref[...] = jnp.dot(act.astype(BF16), wd_ref[0], preferred_element_type=F32) + bd_ref[0]

    @pl.when(pl.program_id(0) >= nv_ref[0])
    def _():
        y_ref[...] = jnp.zeros_like(y_ref)


def _expert_call(tile_expert, n_valid, xs, wgu_b, bgu, wd_b, bd, tm):
    n_rows, d = xs.shape
    d_ff = wd_b.shape[1]
    n_tiles = n_rows // tm
    xrow = lambda i, te, nv: (jnp.minimum(i, nv[0] - 1), 0)
    wsel = lambda i, te, nv: (te[i], 0, 0)
    return pl.pallas_call(
        functools.partial(_expert_kernel, d_ff=d_ff),
        out_shape=jax.ShapeDtypeStruct((n_rows, d), F32),
        grid_spec=pltpu.PrefetchScalarGridSpec(
            num_scalar_prefetch=2,
            grid=(n_tiles,),
            in_specs=[pl.BlockSpec((tm, d), xrow),
                      pl.BlockSpec((1, d, 2 * d_ff), wsel),
                      pl.BlockSpec((1, 1, 2 * d_ff), wsel),
                      pl.BlockSpec((1, d_ff, d), wsel),
                      pl.BlockSpec((1, 1, d), wsel)],
            out_specs=pl.BlockSpec((tm, d), lambda i, te, nv: (i, 0))),
        compiler_params=pltpu.CompilerParams(dimension_semantics=("arbitrary",),
                                             vmem_limit_bytes=VMEM_LIMIT),
        name="moe_experts",
    )(tile_expert, n_valid, xs, wgu_b, bgu.reshape(N_EXPERTS, 1, -1), wd_b,
      bd.reshape(N_EXPERTS, 1, -1))


def _combine_kernel(pos_ref, ys_ref, gate_ref, h_ref, lg_ref, lb_ref, o_ref, buf_ref, sem,
                    *, alpha, tm):
    base = pl.program_id(0) * (tm * TOP_K)

    def row_copy(r, k):
        src = pos_ref[base + r * TOP_K + k]
        return pltpu.make_async_copy(ys_ref.at[pl.ds(src, 1), :], buf_ref.at[k, pl.ds(r, 1), :], sem)

    def start(r, _):
        for k in range(TOP_K):
            row_copy(r, k).start()
        return 0

    def wait(r, _):
        for k in range(TOP_K):
            row_copy(r, k).wait()
        return 0

    lax.fori_loop(0, tm, start, 0)
    lax.fori_loop(0, tm, wait, 0)
    gate = gate_ref[...]
    ffn = buf_ref[0] * gate[:, 0:1]
    for k in range(1, TOP_K):
        ffn = ffn + buf_ref[k] * gate[:, k:k + 1]
    o_ref[...] = _layer_norm(alpha * h_ref[...] + ffn, lg_ref[...], lb_ref[...])


def _combine_call(pos_flat, ys, gates, h2d, ln_g, ln_b, alpha, tm):
    t, d = h2d.shape
    row = lambda i, pos: (i, 0)
    const2 = lambda i, pos: (0, 0)
    return pl.pallas_call(
        functools.partial(_combine_kernel, alpha=alpha, tm=tm),
        out_shape=jax.ShapeDtypeStruct((t, d), F32),
        grid_spec=pltpu.PrefetchScalarGridSpec(
            num_scalar_prefetch=1,
            grid=(t // tm,),
            in_specs=[pl.BlockSpec(memory_space=pl.ANY),
                      pl.BlockSpec((tm, LANES), row),
                      pl.BlockSpec((tm, d), row),
                      pl.BlockSpec((1, d), const2),
                      pl.BlockSpec((1, d), const2)],
            out_specs=pl.BlockSpec((tm, d), row),
            scratch_shapes=[pltpu.VMEM((TOP_K, tm, d), F32),
                            pltpu.SemaphoreType.DMA(())]),
        compiler_params=pltpu.CompilerParams(dimension_semantics=("arbitrary",),
                                             vmem_limit_bytes=VMEM_LIMIT),
        name="moe_combine",
    )(pos_flat, ys, gates, h2d, ln_g.reshape(1, d), ln_b.reshape(1, d))


def _in_proj_columns():
    o_cq = 3 * SB_WIDTH
    o_ckv = o_cq + MLA_Q_RANK
    o_kpe = o_ckv + MLA_KV_RANK
    o_ca = o_kpe + MLA_ROPE_DIM
    o_cg = o_ca + CONV_CH
    reps = LANES // 2 // ROPE_HALF
    kpe = np.concatenate([np.tile(o_kpe + np.arange(ROPE_HALF), reps),
                          np.tile(o_kpe + ROPE_HALF + np.arange(ROPE_HALF), reps)])
    return np.concatenate([np.arange(o_cq), o_cq + np.arange(MLA_Q_RANK),
                           o_ckv + np.arange(MLA_KV_RANK), kpe,
                           o_ca + np.arange(CONV_CH), o_cg + np.arange(CONV_CH)])


def _uq_columns():
    per = MLA_NOPE_DIM + MLA_ROPE_DIM
    nope = np.concatenate([h * per + np.arange(MLA_NOPE_DIM) for h in range(MLA_HEADS)])
    rope = []
    grp = LANES // 2 // ROPE_HALF
    for g0 in range(0, MLA_HEADS, grp):
        for half in range(2):
            for h in range(g0, g0 + grp):
                rope.append(h * per + MLA_NOPE_DIM + half * ROPE_HALF + np.arange(ROPE_HALF))
    return np.concatenate([nope] + rope)


def _ukv_columns():
    per = MLA_NOPE_DIM + MLA_V_DIM
    kn = np.concatenate([h * per + np.arange(MLA_NOPE_DIM) for h in range(MLA_HEADS)])
    vv = np.concatenate([h * per + MLA_NOPE_DIM + np.arange(MLA_V_DIM) for h in range(MLA_HEADS)])
    return np.concatenate([kn, vv])


def _rope_tables(seq):
    lp = seq + N_META
    inv = 1.0 / (ROPE_THETA ** (jnp.arange(0, MLA_ROPE_DIM, 2, dtype=F32) / MLA_ROPE_DIM))
    pos = jnp.concatenate([N_META + jnp.arange(seq, dtype=F32), jnp.arange(N_META, dtype=F32)])
    ang = pos[:, None] * inv[None, :]
    reps = LANES // 2 // ROPE_HALF
    cos = jnp.tile(jnp.cos(ang), (1, 2 * reps))
    sin = jnp.tile(jnp.sin(ang), (1, reps))
    assert cos.shape == (lp, LANES)
    return cos, jnp.concatenate([-sin, sin], axis=1)


ATTN_TQ = 256
EXPERT_TM = 256
ROUTE_TM = 128


def kernel(x, meta_tokens, ln_in_g, ln_in_b, w_in, q_norm_g, w_uq, kv_norm_g, w_ukv, conv_w, conv_b,
           conv_ln_g, conv_ln_b, grp_norm_g, w_out, ln_mix_g, ln_mix_b, router_w, router_b,
           w_gate_up, b_gate_up, w_down, b_down, ln_ffn_g, ln_ffn_b):
    b, seq, d = x.shape
    depth = w_in.shape[0]
    lp = seq + N_META
    t = b * lp
    alpha = float((2 * depth) ** 0.25)
    tq = min(ATTN_TQ, seq)
    assert seq % tq == 0 and tq % LANES == 0 and seq % CONV_ROWS == 0
    tm_tok = _row_tile(t, 512, LANES)
    tm_seq = _row_tile(lp, 1024)
    tm_route = _row_tile(t, ROUTE_TM, 8)
    n_assign = t * TOP_K
    n_tiles = -(-n_assign // EXPERT_TM) + N_EXPERTS
    n_rows = n_tiles * EXPERT_TM

    meta = jnp.broadcast_to(meta_tokens[None].astype(x.dtype), (b, N_META, d))
    xin = jnp.concatenate([x, meta], axis=1).reshape(t, d)
    h = _ln_call(xin, ln_in_g, ln_in_b, tm_tok)
    cos_t, sin_t = _rope_tables(seq)

    in_cols = _in_proj_columns()
    uq_cols, ukv_cols = _uq_columns(), _ukv_columns()

    for l in range(depth):
        w_in_p = w_in[l][:, in_cols].astype(BF16)
        sbqk, sbv, mq, mk, mv, cu = _inproj_call(
            h.reshape(b, lp, d), w_in_p, q_norm_g[l], w_uq[l][:, uq_cols].astype(BF16),
            kv_norm_g[l], w_ukv[l][:, ukv_cols].astype(BF16), cos_t, sin_t, tm_seq)
        sb_out = _sb_call(sbqk, sbv.transpose(0, 2, 1), sbv[:, seq:, :], tq)
        mla_out = _mla_call(mq, mk, mv.transpose(0, 2, 1), mv[:, seq:, :], tq)
        conv_out = _conv_call(cu, conv_w[l], conv_b[l], conv_ln_g[l], conv_ln_b[l])
        h1, idx, gates, rank, counts = _mix_call(
            sb_out.reshape(t, -1), mla_out.reshape(t, -1), conv_out.reshape(t, -1), h,
            grp_norm_g[l], w_out[l].astype(BF16), ln_mix_g[l], ln_mix_b[l],
            router_w[l].astype(BF16), router_b[l], alpha, tm_tok)

        cnt = counts[0].astype(jnp.int32)
        tiles_e = (cnt + EXPERT_TM - 1) // EXPERT_TM
        tile_end = jnp.cumsum(tiles_e)
        tile_start = tile_end - tiles_e
        pos = tile_start[idx[:, :TOP_K]] * EXPERT_TM + rank[:, :TOP_K]
        pos_flat = pos.reshape(-1).astype(jnp.int32)
        n_valid = tile_end[-1:].astype(jnp.int32)
        tile_ids = jnp.minimum(jnp.arange(n_tiles, dtype=jnp.int32), n_valid[0] - 1)
        tile_expert = jnp.minimum(jnp.sum(tile_end[None, :] <= tile_ids[:, None], axis=1),
                                  N_EXPERTS - 1).astype(jnp.int32)

        xs = _dispatch_call(pos_flat, h1, n_rows, tm_route)
        ys = _expert_call(tile_expert, n_valid, xs, w_gate_up[l].astype(BF16), b_gate_up[l],
                          w_down[l].astype(BF16), b_down[l], EXPERT_TM)
        h = _combine_call(pos_flat, ys, gates, h1, ln_ffn_g[l], ln_ffn_b[l], alpha, tm_route)

    return h.reshape(b, lp, d)[:, :seq, :]
```

```python
import functools

import jax
import jax.numpy as jnp
import numpy as np
from jax import lax
from jax.experimental import pallas as pl
from jax.experimental.pallas import tpu as pltpu

F32 = jnp.float32
BF16 = jnp.bfloat16

N_META = 16
SB_HEADS = 4
SB_HEAD_DIM = 64
SB_WIDTH = SB_HEADS * SB_HEAD_DIM
MLA_HEADS = 8
MLA_NOPE_DIM = 64
MLA_ROPE_DIM = 32
MLA_V_DIM = 64
MLA_Q_RANK = 256
MLA_KV_RANK = 128
MLA_WIDTH = MLA_HEADS * MLA_V_DIM
ROPE_THETA = 10000.0
CONV_CH = 256
CONV_K = 31
N_EXPERTS = 32
TOP_K = 4
SWIGLU_LIMIT = 7.0
SWIGLU_ALPHA = 1.702
LN_EPS = 1e-5
RMS_EPS = 1e-6

LANES = 128
HEAD_PAIR = LANES // SB_HEAD_DIM
ROPE_HALF = MLA_ROPE_DIM // 2
VMEM_LIMIT = 48 * 1024 * 1024
EXPERT_VMEM_LIMIT = 56 * 1024 * 1024
NEG_BIG = -1e30
LOG2E = 1.4426950408889634

IN_SB = 3 * SB_WIDTH
IN_CQ = IN_SB
IN_CKV = IN_CQ + MLA_Q_RANK
IN_KPE = IN_CKV + MLA_KV_RANK
IN_CA = IN_KPE + LANES
IN_CG = IN_CA + CONV_CH
IN_TOTAL = IN_CG + CONV_CH
MQ_W = MLA_HEADS * MLA_NOPE_DIM + 2 * LANES
MK_W = MLA_HEADS * MLA_NOPE_DIM + LANES


def _row_tile(n, cap, mult=16):
    best = None
    for t in range(mult, min(n, cap) + 1, mult):
        if n % t == 0:
            best = t
    assert best is not None, (n, cap, mult)
    return best


def _layer_norm(x, g, b):
    mu = jnp.mean(x, axis=-1, keepdims=True)
    xc = x - mu
    var = jnp.mean(xc * xc, axis=-1, keepdims=True)
    return xc * lax.rsqrt(var + LN_EPS) * g + b


def _rms_norm(x, g):
    return x * lax.rsqrt(jnp.mean(x * x, axis=-1, keepdims=True) + RMS_EPS) * g


def _sigmoid(x):
    return 1.0 / (1.0 + jnp.exp(-x))


def _ln_kernel(x_ref, g_ref, b_ref, o_ref):
    o_ref[...] = _layer_norm(x_ref[...], g_ref[...], b_ref[...])


def _ln_call(x2d, g, b, tm):
    t, d = x2d.shape
    return pl.pallas_call(
        _ln_kernel,
        out_shape=jax.ShapeDtypeStruct((t, d), F32),
        grid=(t // tm,),
        in_specs=[pl.BlockSpec((tm, d), lambda i: (i, 0)),
                  pl.BlockSpec((1, d), lambda i: (0, 0)),
                  pl.BlockSpec((1, d), lambda i: (0, 0))],
        out_specs=pl.BlockSpec((tm, d), lambda i: (i, 0)),
        compiler_params=pltpu.CompilerParams(dimension_semantics=("parallel",),
                                             vmem_limit_bytes=VMEM_LIMIT),
        name="ln_in",
    )(x2d, g.reshape(1, d), b.reshape(1, d))


def _inproj_kernel(h_ref, w_ref, qg_ref, wuq_ref, kvg_ref, wukv_ref, cos_ref, sin_ref,
                   sbqk_ref, sbv_ref, mq_ref, mk_ref, mv_ref, cu_ref):
    h = h_ref[0].astype(BF16)
    proj = jnp.dot(h, w_ref[...], preferred_element_type=F32)
    sbqk_ref[0, :, :SB_WIDTH] = (proj[:, :SB_WIDTH] * (SB_HEAD_DIM ** -0.5 * LOG2E)).astype(BF16)
    sbqk_ref[0, :, SB_WIDTH:] = proj[:, SB_WIDTH:2 * SB_WIDTH].astype(BF16)
    sbv_ref[0] = proj[:, 2 * SB_WIDTH:IN_SB].astype(BF16)
    cos = cos_ref[...]
    sin = sin_ref[...]

    def rot(x):
        return x * cos + pltpu.roll(x, LANES // 2, 1) * sin

    cq = _rms_norm(proj[:, IN_CQ:IN_CKV], qg_ref[...])
    qm = jnp.dot(cq.astype(BF16), wuq_ref[...], preferred_element_type=F32)
    qm = qm * ((MLA_NOPE_DIM + MLA_ROPE_DIM) ** -0.5 * LOG2E)
    nope_w = MLA_HEADS * MLA_NOPE_DIM
    mq_ref[0, :, :nope_w] = qm[:, :nope_w].astype(BF16)
    mq_ref[0, :, nope_w:nope_w + LANES] = rot(qm[:, nope_w:nope_w + LANES]).astype(BF16)
    mq_ref[0, :, nope_w + LANES:] = rot(qm[:, nope_w + LANES:]).astype(BF16)

    ckv = _rms_norm(proj[:, IN_CKV:IN_KPE], kvg_ref[...])
    kv = jnp.dot(ckv.astype(BF16), wukv_ref[...], preferred_element_type=F32)
    mk_ref[0, :, :nope_w] = kv[:, :nope_w].astype(BF16)
    mk_ref[0, :, nope_w:] = rot(proj[:, IN_KPE:IN_CA]).astype(BF16)
    mv_ref[0] = kv[:, nope_w:].astype(BF16)

    cu_ref[0] = proj[:, IN_CA:IN_CG] * _sigmoid(proj[:, IN_CG:IN_TOTAL])


def _inproj_call(h3, w_in_p, qg, wuq_p, kvg, wukv_p, cos_t, sin_t, tm):
    b, lp, d = h3.shape
    nt = lp // tm
    const2 = lambda bi, ti: (0, 0)
    tok3 = lambda bi, ti: (bi, ti, 0)
    outs = (jax.ShapeDtypeStruct((b, lp, 2 * SB_WIDTH), BF16),
            jax.ShapeDtypeStruct((b, lp, SB_WIDTH), BF16),
            jax.ShapeDtypeStruct((b, lp, MQ_W), BF16),
            jax.ShapeDtypeStruct((b, lp, MK_W), BF16),
            jax.ShapeDtypeStruct((b, lp, MLA_WIDTH), BF16),
            jax.ShapeDtypeStruct((b, lp, CONV_CH), F32))
    return pl.pallas_call(
        _inproj_kernel,
        out_shape=outs,
        grid=(b, nt),
        in_specs=[pl.BlockSpec((1, tm, d), tok3),
                  pl.BlockSpec(w_in_p.shape, const2),
                  pl.BlockSpec((1, MLA_Q_RANK), const2),
                  pl.BlockSpec(wuq_p.shape, const2),
                  pl.BlockSpec((1, MLA_KV_RANK), const2),
                  pl.BlockSpec(wukv_p.shape, const2),
                  pl.BlockSpec((tm, LANES), lambda bi, ti: (ti, 0)),
                  pl.BlockSpec((tm, LANES), lambda bi, ti: (ti, 0))],
        out_specs=tuple(pl.BlockSpec((1, tm, s.shape[2]), tok3) for s in outs),
        compiler_params=pltpu.CompilerParams(dimension_semantics=("parallel", "parallel"),
                                             vmem_limit_bytes=VMEM_LIMIT),
        name="in_proj",
    )(h3, w_in_p, qg.reshape(1, -1), wuq_p, kvg.reshape(1, -1), wukv_p, cos_t, sin_t)


def _log_stay2(z):
    nz = -z
    return jnp.minimum(nz, 0.0) - jnp.log2(1.0 + jnp.exp2(jnp.minimum(z, nz)))


def _split_bf16(x):
    hi = x.astype(BF16)
    return hi, (x - hi.astype(F32)).astype(BF16)


def _sb_kernel(q_ref, k_ref, vt_ref, vmeta_ref, o_ref, acc_ref, c_ref, *, tq, seq):
    i = pl.program_id(1)
    nq = seq // tq
    npair = SB_HEADS // HEAD_PAIR
    lane = lax.broadcasted_iota(jnp.int32, (1, LANES), 1)
    head_sel = (lane < SB_HEAD_DIM, lane >= SB_HEAD_DIM)
    top_rows = lax.broadcasted_iota(jnp.int32, (LANES, 1), 0) < SB_HEAD_DIM
    meta = pl.ds(seq, N_META)

    def iota2(rows, cols):
        return (lax.broadcasted_iota(jnp.int32, (rows, cols), 0),
                lax.broadcasted_iota(jnp.int32, (rows, cols), 1))

    def tri_down(w):
        r, c = iota2(w, w)
        return jnp.where(c >= r, 1.0, 0.0).astype(BF16)

    def tile_all(q, krows, mask, ut, first):
        zs = []
        for p in range(npair):
            cols = slice(p * LANES, (p + 1) * LANES)
            kt = k_ref[0, krows, cols]
            for hd in range(HEAD_PAIR):
                qh = jnp.where(head_sel[hd], q[:, cols], jnp.zeros((1, 1), BF16))
                zs.append(lax.dot_general(kt, qh, (((1,), (1,)), ((), ())),
                                          preferred_element_type=F32))
        splits = []
        for h in range(SB_HEADS):
            ls = _log_stay2(zs[h])
            if mask is not None:
                ls = jnp.where(mask, ls, 0.0)
            splits.append(_split_bf16(ls))
        cums = [jnp.dot(ut, hi, preferred_element_type=F32) + jnp.dot(ut, lo, preferred_element_type=F32)
                for hi, lo in splits]
        ws = []
        for h in range(SB_HEADS):
            lw = zs[h] + cums[h] if first else zs[h] + cums[h] + c_ref[h]
            w = jnp.exp2(lw)
            if mask is not None:
                w = jnp.where(mask, w, 0.0)
            ws.append(w.astype(BF16))
            c_ref[h] = cums[h][0:1, :] if first else c_ref[h] + cums[h][0:1, :]
        for p in range(npair):
            vt = vt_ref[0, p * LANES:(p + 1) * LANES, krows]
            res = [jnp.dot(vt, ws[p * HEAD_PAIR + hd], preferred_element_type=F32)
                   for hd in range(HEAD_PAIR)]
            contrib = jnp.where(top_rows, res[0], res[1])
            acc_ref[p] = contrib if first else acc_ref[p] + contrib

    @pl.when(i < nq)
    def _():
        q = q_ref[0]
        ut = tri_down(tq)
        r, c = iota2(tq, tq)
        tile_all(q, pl.ds(pl.multiple_of(i * tq, tq), tq), r < c, ut, True)

        def body(t, carry):
            tile_all(q, pl.ds(pl.multiple_of((i - 1 - t) * tq, tq), tq), None, ut, False)
            return carry

        lax.fori_loop(0, i, body, 0)
        tile_all(q, meta, None, tri_down(N_META), False)
        for p in range(npair):
            o_ref[0, :, p * LANES:(p + 1) * LANES] = acc_ref[p].T.astype(o_ref.dtype)

    @pl.when(i == nq)
    def _():
        r, c = iota2(N_META, N_META)
        mask = c < r
        ut = jnp.where(r >= c, 1.0, 0.0).astype(BF16)
        for p in range(npair):
            cols = slice(p * LANES, (p + 1) * LANES)
            q = q_ref[0, 0:N_META, cols]
            kt = k_ref[0, meta, cols]
            vm = vmeta_ref[0, :, cols]
            res = []
            for hd in range(HEAD_PAIR):
                qh = jnp.where(head_sel[hd], q, jnp.zeros((1, 1), BF16))
                z = lax.dot_general(qh, kt, (((1,), (1,)), ((), ())), preferred_element_type=F32)
                hi, lo = _split_bf16(jnp.where(mask, _log_stay2(z), 0.0))
                cum = (jnp.dot(hi, ut, preferred_element_type=F32)
                       + jnp.dot(lo, ut, preferred_element_type=F32))
                w = jnp.where(mask, jnp.exp2(z + cum), 0.0)
                res.append(jnp.dot(w.astype(BF16), vm, preferred_element_type=F32))
            o_ref[0, 0:N_META, cols] = jnp.where(head_sel[0], res[0], res[1]).astype(o_ref.dtype)


def _sb_call(sbqk, sbv_t, sbv_meta, tq):
    b, lp, _ = sbqk.shape
    seq = lp - N_META
    nq = seq // tq
    return pl.pallas_call(
        functools.partial(_sb_kernel, tq=tq, seq=seq),
        out_shape=jax.ShapeDtypeStruct((b, lp, SB_WIDTH), BF16),
        grid=(b, nq + 1),
        in_specs=[pl.BlockSpec((1, tq, SB_WIDTH), lambda bi, i: (bi, i, 0)),
                  pl.BlockSpec((1, lp, SB_WIDTH), lambda bi, i: (bi, 0, 1)),
                  pl.BlockSpec((1, SB_WIDTH, lp), lambda bi, i: (bi, 0, 0)),
                  pl.BlockSpec((1, N_META, SB_WIDTH), lambda bi, i: (bi, 0, 0))],
        out_specs=pl.BlockSpec((1, tq, SB_WIDTH), lambda bi, i: (bi, i, 0)),
        scratch_shapes=[pltpu.VMEM((SB_HEADS // HEAD_PAIR, LANES, tq), F32),
                        pltpu.VMEM((SB_HEADS, 1, tq), F32)],
        compiler_params=pltpu.CompilerParams(
            dimension_semantics=("parallel", "arbitrary"),
            vmem_limit_bytes=VMEM_LIMIT),
        name="sb_attn",
    )(sbqk, sbqk, sbv_t, sbv_meta)


def _mla_kernel(q_ref, k_ref, vt_ref, vmeta_ref, o_ref, acc_ref, m_ref, *, tq, seq):
    i = pl.program_id(1)
    nq = seq // tq
    npair = MLA_HEADS // HEAD_PAIR
    nope_w = MLA_HEADS * MLA_NOPE_DIM
    grp = LANES // 2 // ROPE_HALF
    lane = lax.broadcasted_iota(jnp.int32, (1, LANES), 1)
    lane2 = lax.broadcasted_iota(jnp.int32, (1, 2 * LANES), 1)
    top_rows = lax.broadcasted_iota(jnp.int32, (LANES, 1), 0) < MLA_V_DIM
    meta = pl.ds(seq, N_META)
    one = jnp.ones((1, 1), BF16)

    def head_lanes(h):
        n0 = (h % HEAD_PAIR) * MLA_NOPE_DIM
        r0 = LANES + (h % grp) * ROPE_HALF
        r1 = r0 + LANES // 2
        return (((lane2 >= n0) & (lane2 < n0 + MLA_NOPE_DIM))
                | ((lane2 >= r0) & (lane2 < r0 + ROPE_HALF))
                | ((lane2 >= r1) & (lane2 < r1 + ROPE_HALF)))

    def qcat(q, p):
        rope0 = nope_w + (p * HEAD_PAIR // grp) * LANES
        return jnp.concatenate([q[:, p * LANES:(p + 1) * LANES], q[:, rope0:rope0 + LANES]], axis=-1)

    def kcat(krows, p):
        return jnp.concatenate([k_ref[0, krows, p * LANES:(p + 1) * LANES],
                                k_ref[0, krows, nope_w:nope_w + LANES]], axis=-1)

    def iota2(rows, cols):
        return (lax.broadcasted_iota(jnp.int32, (rows, cols), 0),
                lax.broadcasted_iota(jnp.int32, (rows, cols), 1))

    def tile_all(q, krows, mask, first):
        scores = []
        for p in range(npair):
            kc = kcat(krows, p)
            qc = qcat(q, p)
            for hd in range(HEAD_PAIR):
                qh = jnp.where(head_lanes(p * HEAD_PAIR + hd), qc, jnp.zeros((1, 1), BF16))
                scores.append(lax.dot_general(kc, qh, (((1,), (1,)), ((), ())),
                                              preferred_element_type=F32))
        pexps, alphas = [], []
        for h in range(MLA_HEADS):
            s = scores[h] if mask is None else jnp.where(mask, scores[h], NEG_BIG)
            mx = jnp.max(s, axis=0, keepdims=True)
            if first:
                m_new = mx
                alphas.append(None)
            else:
                m_old = m_ref[h]
                m_new = jnp.maximum(m_old, mx)
                alphas.append(jnp.exp2(m_old - m_new))
            m_ref[h] = m_new
            pexps.append(jnp.exp2(s - m_new).astype(BF16))
        for h in range(MLA_HEADS):
            p, hd = divmod(h, HEAD_PAIR)
            vt = vt_ref[0, p * LANES:(p + 1) * LANES, krows]
            vte = jnp.where(top_rows, vt, one) if hd == 0 else jnp.where(top_rows, one, vt)
            pv = jnp.dot(vte, pexps[h], preferred_element_type=F32)
            acc_ref[h] = pv if first else acc_ref[h] * alphas[h] + pv

    @pl.when(i < nq)
    def _():
        q = q_ref[0]
        r, c = iota2(tq, tq)
        tile_all(q, pl.ds(pl.multiple_of(i * tq, tq), tq), r <= c, True)

        def body(t, carry):
            tile_all(q, pl.ds(pl.multiple_of(t * tq, tq), tq), None, False)
            return carry

        lax.fori_loop(0, i, body, 0)
        tile_all(q, meta, None, False)
        for p in range(npair):
            a = acc_ref[p * HEAD_PAIR]
            b = acc_ref[p * HEAD_PAIR + 1]
            out_t = jnp.where(top_rows, a * (1.0 / a[MLA_V_DIM:MLA_V_DIM + 1, :]), b * (1.0 / b[0:1, :]))
            o_ref[0, :, p * LANES:(p + 1) * LANES] = out_t.T.astype(o_ref.dtype)

    @pl.when(i == nq)
    def _():
        q = q_ref[0, 0:N_META, :]
        r, c = iota2(N_META, N_META)
        for p in range(npair):
            kc = kcat(meta, p)
            qc = qcat(q, p)
            vm = vmeta_ref[0, :, p * LANES:(p + 1) * LANES]
            res = []
            for hd in range(HEAD_PAIR):
                qh = jnp.where(head_lanes(p * HEAD_PAIR + hd), qc, jnp.zeros((1, 1), BF16))
                s = lax.dot_general(qh, kc, (((1,), (1,)), ((), ())), preferred_element_type=F32)
                s = jnp.where(c <= r, s, NEG_BIG)
                pexp = jnp.exp2(s - jnp.max(s, axis=1, keepdims=True))
                pv = jnp.dot(pexp.astype(BF16), vm, preferred_element_type=F32)
                res.append(pv * (1.0 / jnp.sum(pexp, axis=1, keepdims=True)))
            o_ref[0, 0:N_META, p * LANES:(p + 1) * LANES] = jnp.where(
                lane < MLA_V_DIM, res[0], res[1]).astype(o_ref.dtype)


def _mla_call(mq, mk, mv_t, mv_meta, tq):
    b, lp, _ = mq.shape
    seq = lp - N_META
    nq = seq // tq
    return pl.pallas_call(
        functools.partial(_mla_kernel, tq=tq, seq=seq),
        out_shape=jax.ShapeDtypeStruct((b, lp, MLA_WIDTH), BF16),
        grid=(b, nq + 1),
        in_specs=[pl.BlockSpec((1, tq, MQ_W), lambda bi, i: (bi, i, 0)),
                  pl.BlockSpec((1, lp, MK_W), lambda bi, i: (bi, 0, 0)),
                  pl.BlockSpec((1, MLA_WIDTH, lp), lambda bi, i: (bi, 0, 0)),
                  pl.BlockSpec((1, N_META, MLA_WIDTH), lambda bi, i: (bi, 0, 0))],
        out_specs=pl.BlockSpec((1, tq, MLA_WIDTH), lambda bi, i: (bi, i, 0)),
        scratch_shapes=[pltpu.VMEM((MLA_HEADS, LANES, tq), F32),
                        pltpu.VMEM((MLA_HEADS, 1, tq), F32)],
        compiler_params=pltpu.CompilerParams(
            dimension_semantics=("parallel", "arbitrary"),
            vmem_limit_bytes=VMEM_LIMIT),
        name="mla_attn",
    )(mq, mk, mv_t, mv_meta)


CONV_PAD = 32
CONV_ROWS = 128
SUBLANES = 8
CONV_WIN_EXTRA = CONV_PAD


def _conv_kernel(u_ref, w_ref, b_ref, g_ref, beta_ref, o_ref, buf_ref, *, seq):
    buf_ref[0:CONV_PAD, :] = jnp.zeros((CONV_PAD, CONV_CH), F32)
    buf_ref[CONV_PAD:CONV_PAD + N_META, :] = u_ref[0, seq:seq + N_META, :]
    buf_ref[CONV_PAD + N_META:CONV_PAD + N_META + seq, :] = u_ref[0, 0:seq, :]
    w = w_ref[...]
    lead = CONV_PAD - (CONV_K - 1)

    def finish(acc):
        y = _layer_norm(acc + b_ref[...], g_ref[...], beta_ref[...])
        return (y * _sigmoid(y)).astype(o_ref.dtype)

    def conv_rows(first_pos, rows):
        win = buf_ref[pl.ds(first_pos, rows + CONV_WIN_EXTRA), :]
        acc = jnp.zeros((rows, CONV_CH), F32)
        for sh in range(SUBLANES):
            offs = [o for o in range(lead, lead + CONV_K) if o % SUBLANES == sh]
            shifted = win[sh:sh + rows + offs[-1] - sh, :]
            for o in offs:
                k = o - lead
                acc = acc + shifted[o - sh:o - sh + rows, :] * w[k:k + 1, :]
        return finish(acc)

    def body(c, _):
        r0 = pl.multiple_of(c * CONV_ROWS, CONV_ROWS)
        o_ref[0, pl.ds(r0, CONV_ROWS), :] = conv_rows(pl.multiple_of(r0 + N_META, SUBLANES), CONV_ROWS)
        return 0

    lax.fori_loop(0, seq // CONV_ROWS, body, 0)
    o_ref[0, seq:seq + N_META, :] = conv_rows(0, N_META)


def _conv_call(cu, conv_w, conv_b, ln_g, ln_b):
    b, lp, _ = cu.shape
    seq = lp - N_META
    const2 = lambda bi: (0, 0)
    return pl.pallas_call(
        functools.partial(_conv_kernel, seq=seq),
        out_shape=jax.ShapeDtypeStruct((b, lp, CONV_CH), BF16),
        grid=(b,),
        in_specs=[pl.BlockSpec((1, lp, CONV_CH), lambda bi: (bi, 0, 0)),
                  pl.BlockSpec((CONV_K, CONV_CH), const2),
                  pl.BlockSpec((1, CONV_CH), const2),
                  pl.BlockSpec((1, CONV_CH), const2),
                  pl.BlockSpec((1, CONV_CH), const2)],
        out_specs=pl.BlockSpec((1, lp, CONV_CH), lambda bi: (bi, 0, 0)),
        scratch_shapes=[pltpu.VMEM((CONV_PAD + lp, CONV_CH), F32)],
        compiler_params=pltpu.CompilerParams(dimension_semantics=("parallel",),
                                             vmem_limit_bytes=VMEM_LIMIT),
        name="conv",
    )(cu, conv_w, conv_b.reshape(1, -1), ln_g.reshape(1, -1), ln_b.reshape(1, -1))


def _mix_kernel(sb_ref, mla_ref, cv_ref, h_ref, gg_ref, wo_ref, lg_ref, lb_ref, rw_ref, rb_ref,
                h1_ref, h1t_ref, idx_ref, gate_ref, rank_ref, cnt_ref, carry_ref, *, alpha, tm):
    step = pl.program_id(0)

    @pl.when(step == 0)
    def _():
        carry_ref[...] = jnp.zeros_like(carry_ref)

    gg = gg_ref[...]
    y = jnp.concatenate(
        [_rms_norm(sb_ref[...].astype(F32), gg[:, :SB_WIDTH]),
         _rms_norm(mla_ref[...].astype(F32), gg[:, SB_WIDTH:SB_WIDTH + MLA_WIDTH]),
         _rms_norm(cv_ref[...].astype(F32), gg[:, SB_WIDTH + MLA_WIDTH:])], axis=-1)
    mix = jnp.dot(y.astype(BF16), wo_ref[...], preferred_element_type=F32)
    h1 = _layer_norm(alpha * h_ref[...] + mix, lg_ref[...], lb_ref[...])
    h1_ref[...] = h1
    _store_token_tiles(h1t_ref, h1)

    logits = jnp.dot(h1.astype(BF16), rw_ref[...], preferred_element_type=F32) + rb_ref[...]
    eiota = lax.broadcasted_iota(jnp.int32, (tm, N_EXPERTS), 1)
    lane = lax.broadcasted_iota(jnp.int32, (tm, LANES), 1)
    vals = logits
    sels, tops, idxs = [], [], []
    for _ in range(TOP_K):
        m = jnp.max(vals, axis=1, keepdims=True)
        idx = jnp.min(jnp.where(vals == m, eiota, N_EXPERTS), axis=1, keepdims=True)
        sel = eiota == idx
        vals = jnp.where(sel, -jnp.inf, vals)
        sels.append(sel)
        tops.append(m)
        idxs.append(idx)
    exps = [jnp.exp(t - tops[0]) for t in tops]
    denom = exps[0] + exps[1] + exps[2] + exps[3]

    chosen = jnp.zeros((tm, N_EXPERTS), F32)
    for sel in sels:
        chosen = chosen + jnp.where(sel, 1.0, 0.0)
    r = lax.broadcasted_iota(jnp.int32, (tm, tm), 0)
    c = lax.broadcasted_iota(jnp.int32, (tm, tm), 1)
    below = jnp.where(c < r, 1.0, 0.0).astype(BF16)
    earlier = jnp.dot(below, chosen.astype(BF16), preferred_element_type=F32) + carry_ref[...]

    idx_out = jnp.zeros((tm, LANES), jnp.int32)
    gate_out = jnp.zeros((tm, LANES), F32)
    rank_out = jnp.zeros((tm, LANES), jnp.int32)
    for k in range(TOP_K):
        rank_k = jnp.sum(jnp.where(sels[k], earlier, 0.0), axis=1, keepdims=True)
        idx_out = jnp.where(lane == k, idxs[k], idx_out)
        gate_out = jnp.where(lane == k, exps[k] / denom, gate_out)
        rank_out = jnp.where(lane == k, rank_k.astype(jnp.int32), rank_out)
    idx_ref[...] = idx_out
    gate_ref[...] = gate_out
    rank_ref[...] = rank_out

    carry_ref[...] = carry_ref[...] + jnp.sum(chosen, axis=0, keepdims=True)
    cnt_ref[...] = carry_ref[...]


def _mix_call(sb_out, mla_out, conv_out, h2d, grp_g, w_out_b, ln_g, ln_b, router_w_b, router_b,
              alpha, tm):
    t, d = h2d.shape
    row = lambda i: (i, 0)
    const2 = lambda i: (0, 0)
    chunks = d // LANES
    outs = (jax.ShapeDtypeStruct((t, d), F32),
            jax.ShapeDtypeStruct((t * chunks, LANES), F32),
            jax.ShapeDtypeStruct((t, LANES), jnp.int32),
            jax.ShapeDtypeStruct((t, LANES), F32),
            jax.ShapeDtypeStruct((t, LANES), jnp.int32),
            jax.ShapeDtypeStruct((1, N_EXPERTS), F32))
    return pl.pallas_call(
        functools.partial(_mix_kernel, alpha=alpha, tm=tm),
        out_shape=outs,
        grid=(t // tm,),
        in_specs=[pl.BlockSpec((tm, SB_WIDTH), row),
                  pl.BlockSpec((tm, MLA_WIDTH), row),
                  pl.BlockSpec((tm, CONV_CH), row),
                  pl.BlockSpec((tm, d), row),
                  pl.BlockSpec((1, d), const2),
                  pl.BlockSpec((d, d), const2),
                  pl.BlockSpec((1, d), const2),
                  pl.BlockSpec((1, d), const2),
                  pl.BlockSpec((d, N_EXPERTS), const2),
                  pl.BlockSpec((1, N_EXPERTS), const2)],
        out_specs=(pl.BlockSpec((tm, d), row),
                   pl.BlockSpec((tm * chunks, LANES), row),
                   pl.BlockSpec((tm, LANES), row),
                   pl.BlockSpec((tm, LANES), row),
                   pl.BlockSpec((tm, LANES), row),
                   pl.BlockSpec((1, N_EXPERTS), const2)),
        scratch_shapes=[pltpu.VMEM((1, N_EXPERTS), F32)],
        compiler_params=pltpu.CompilerParams(dimension_semantics=("arbitrary",),
                                             vmem_limit_bytes=VMEM_LIMIT),
        name="mix_router",
    )(sb_out, mla_out, conv_out, h2d, grp_g.reshape(1, d), w_out_b, ln_g.reshape(1, d),
      ln_b.reshape(1, d), router_w_b, router_b.reshape(1, N_EXPERTS))


def _store_token_tiles(ref, val):
    tm, d = val.shape
    chunks = d // LANES
    for c in range(chunks):
        ref[pl.ds(c, tm, stride=chunks), :] = val[:, c * LANES:(c + 1) * LANES]


def _load_token_tiles(ref, tm, chunks):
    return jnp.concatenate([ref[pl.ds(c, tm, stride=chunks), :] for c in range(chunks)], axis=-1)


INV_CHUNK = 1024


def _invert_kernel(cnt_ref, start_ref, pos_hbm, tok_ref, chunk_ref, sem, *, n_chunks, n_out, tm):
    def fill(lo, hi):
        def body(r, _):
            tok_ref[r] = 0
            return 0
        lax.fori_loop(lo, hi, body, 0)

    def pad_rows(e, used_end):
        tiles_e = (cnt_ref[e] + (tm - 1)) // tm
        fill(start_ref[e] * tm + cnt_ref[e], (start_ref[e] + tiles_e) * tm)
        return jnp.maximum(used_end, (start_ref[e] + tiles_e) * tm)

    used_end = lax.fori_loop(0, N_EXPERTS, pad_rows, 0)
    fill(used_end, n_out)

    def chunk_copy(ch, slot):
        return pltpu.make_async_copy(pos_hbm.at[ch], chunk_ref.at[pl.ds(slot * INV_CHUNK, INV_CHUNK)],
                                     sem.at[slot])

    chunk_copy(0, 0).start()
    toks = INV_CHUNK // TOP_K

    def chunk(ch, _):
        slot = ch % 2
        chunk_copy(ch, slot).wait()

        @pl.when(ch + 1 < n_chunks)
        def _():
            chunk_copy(ch + 1, 1 - slot).start()

        first = slot * INV_CHUNK

        def body(tl, _):
            for k in range(TOP_K):
                tok_ref[chunk_ref[first + tl * TOP_K + k]] = ch * toks + tl
            return 0

        lax.fori_loop(0, toks, body, 0, unroll=4)
        return 0

    lax.fori_loop(0, n_chunks, chunk, 0)


def _invert_call(cnt, tile_start, pos_chunks, n_out, tm):
    n_chunks = pos_chunks.shape[0]
    return pl.pallas_call(
        functools.partial(_invert_kernel, n_chunks=n_chunks, n_out=n_out, tm=tm),
        out_shape=jax.ShapeDtypeStruct((n_out,), jnp.int32),
        grid_spec=pltpu.PrefetchScalarGridSpec(
            num_scalar_prefetch=2,
            grid=(1,),
            in_specs=[pl.BlockSpec(memory_space=pl.ANY)],
            out_specs=pl.BlockSpec(memory_space=pltpu.SMEM),
            scratch_shapes=[pltpu.SMEM((2 * INV_CHUNK,), jnp.int32),
                            pltpu.SemaphoreType.DMA((2,))]),
        compiler_params=pltpu.CompilerParams(dimension_semantics=("arbitrary",)),
        name="moe_row_table",
    )(cnt, tile_start, pos_chunks)


def _expert_kernel(te_ref, nv_ref, tok_ref, h_hbm, wgu_ref, bgu_ref, wd_ref, bd_ref, y_ref,
                   xbuf, wgu_bf, wd_bf, sem, *, tm, d, d_ff):
    i = pl.program_id(0)
    nv = nv_ref[0]
    chunks = d // LANES

    def row_copy(tile, r, slot):
        tok = tok_ref[tile * tm + r]
        return pltpu.make_async_copy(
            h_hbm.at[pl.ds(pl.multiple_of(tok * chunks, chunks), chunks), :],
            xbuf.at[slot, pl.ds(r * chunks, chunks), :], sem.at[slot])

    def wait_gather(slot):
        pltpu.make_async_copy(h_hbm.at[pl.ds(0, tm * chunks), :], xbuf.at[slot], sem.at[slot]).wait()

    @pl.when(i == 0)
    def _():
        def body(r, _):
            row_copy(0, r, 0).start()
            return 0
        lax.fori_loop(0, tm, body, 0, unroll=8)

    @pl.when((i < nv) & ((i == 0) | (te_ref[i] != te_ref[jnp.maximum(i - 1, 0)])))
    def _():
        wgu_bf[...] = wgu_ref[0].astype(BF16)
        wd_bf[...] = wd_ref[0].astype(BF16)

    for slot in range(2):
        @pl.when((i < nv) & (i % 2 == slot))
        def _():
            wait_gather(slot)
            x = _load_token_tiles(xbuf.at[slot], tm, chunks).astype(BF16)
            gu = jnp.dot(x, wgu_bf[...], preferred_element_type=F32) + bgu_ref[0]
            g = jnp.minimum(gu[:, :d_ff], SWIGLU_LIMIT)
            up = jnp.clip(gu[:, d_ff:], -SWIGLU_LIMIT, SWIGLU_LIMIT)
            act = (up + 1.0) * (g * _sigmoid(SWIGLU_ALPHA * g))
            y = jnp.dot(act.astype(BF16), wd_bf[...], preferred_element_type=F32) + bd_ref[0]
            _store_token_tiles(y_ref, y)
            for r in range(tm):
                row_copy(i + 1, r, 1 - slot).start()

        @pl.when((i == nv) & (i % 2 == slot))
        def _():
            wait_gather(slot)

    @pl.when(i >= nv)
    def _():
        y_ref[...] = jnp.zeros_like(y_ref)


def _expert_call(tile_expert, n_valid, row_tok, h_tiles, wgu, bgu, wd, bd, n_tiles, tm):
    d_ff, d = wd.shape[1:]
    chunks = d // LANES
    wsel = lambda i, te, nv, tok: (te[i], 0, 0)
    return pl.pallas_call(
        functools.partial(_expert_kernel, tm=tm, d=d, d_ff=d_ff),
        out_shape=jax.ShapeDtypeStruct((n_tiles * tm * chunks, LANES), F32),
        grid_spec=pltpu.PrefetchScalarGridSpec(
            num_scalar_prefetch=3,
            grid=(n_tiles,),
            in_specs=[pl.BlockSpec(memory_space=pl.ANY),
                      pl.BlockSpec((1, d, 2 * d_ff), wsel),
                      pl.BlockSpec((1, 1, 2 * d_ff), wsel),
                      pl.BlockSpec((1, d_ff, d), wsel),
                      pl.BlockSpec((1, 1, d), wsel)],
            out_specs=pl.BlockSpec((tm * chunks, LANES), lambda i, te, nv, tok: (i, 0)),
            scratch_shapes=[pltpu.VMEM((2, tm * chunks, LANES), F32),
                            pltpu.VMEM((d, 2 * d_ff), BF16),
                            pltpu.VMEM((d_ff, d), BF16),
                            pltpu.SemaphoreType.DMA((2,))]),
        compiler_params=pltpu.CompilerParams(dimension_semantics=("arbitrary",),
                                             vmem_limit_bytes=EXPERT_VMEM_LIMIT),
        name="moe_experts",
    )(tile_expert, n_valid, row_tok, h_tiles, wgu, bgu.reshape(N_EXPERTS, 1, -1), wd,
      bd.reshape(N_EXPERTS, 1, -1))


def _combine_kernel(pos_ref, ys_hbm, gate_ref, h_ref, lg_ref, lb_ref, o_ref, buf, sem,
                    *, alpha, tm, d):
    i = pl.program_id(0)
    chunks = d // LANES

    n_steps = pl.num_programs(0)

    def row_copy(step, r, k, slot):
        src = pos_ref[step * (tm * TOP_K) + r * TOP_K + k]
        return pltpu.make_async_copy(
            ys_hbm.at[pl.ds(pl.multiple_of(src * chunks, chunks), chunks), :],
            buf.at[slot, k, pl.ds(r * chunks, chunks), :], sem.at[slot])

    def wait_rows(slot):
        for k in range(TOP_K):
            pltpu.make_async_copy(ys_hbm.at[pl.ds(0, tm * chunks), :], buf.at[slot, k],
                                  sem.at[slot]).wait()

    @pl.when(i == 0)
    def _():
        def body(r, _):
            for k in range(TOP_K):
                row_copy(0, r, k, 0).start()
            return 0
        lax.fori_loop(0, tm, body, 0, unroll=2)

    for slot in range(2):
        @pl.when(i % 2 == slot)
        def _():
            wait_rows(slot)
            gate = gate_ref[...]
            ffn = _load_token_tiles(buf.at[slot, 0], tm, chunks) * gate[:, 0:1]
            for k in range(1, TOP_K):
                ffn = ffn + _load_token_tiles(buf.at[slot, k], tm, chunks) * gate[:, k:k + 1]
            o_ref[...] = _layer_norm(alpha * h_ref[...] + ffn, lg_ref[...], lb_ref[...])
            nxt = jnp.minimum(i + 1, n_steps - 1)
            for r in range(tm):
                for k in range(TOP_K):
                    row_copy(nxt, r, k, 1 - slot).start()

        @pl.when((i % 2 == slot) & (i + 1 == n_steps))
        def _():
            wait_rows(1 - slot)


def _combine_call(pos_flat, ys, gates, h2d, ln_g, ln_b, alpha, tm):
    t, d = h2d.shape
    chunks = d // LANES
    row = lambda i, pos: (i, 0)
    const2 = lambda i, pos: (0, 0)
    return pl.pallas_call(
        functools.partial(_combine_kernel, alpha=alpha, tm=tm, d=d),
        out_shape=jax.ShapeDtypeStruct((t, d), F32),
        grid_spec=pltpu.PrefetchScalarGridSpec(
            num_scalar_prefetch=1,
            grid=(t // tm,),
            in_specs=[pl.BlockSpec(memory_space=pl.ANY),
                      pl.BlockSpec((tm, LANES), row),
                      pl.BlockSpec((tm, d), row),
                      pl.BlockSpec((1, d), const2),
                      pl.BlockSpec((1, d), const2)],
            out_specs=pl.BlockSpec((tm, d), row),
            scratch_shapes=[pltpu.VMEM((2, TOP_K, tm * chunks, LANES), F32),
                            pltpu.SemaphoreType.DMA((2,))]),
        compiler_params=pltpu.CompilerParams(dimension_semantics=("arbitrary",),
                                             vmem_limit_bytes=VMEM_LIMIT),
        name="moe_combine",
    )(pos_flat, ys, gates, h2d, ln_g.reshape(1, d), ln_b.reshape(1, d))


def _in_proj_columns():
    o_cq = 3 * SB_WIDTH
    o_ckv = o_cq + MLA_Q_RANK
    o_kpe = o_ckv + MLA_KV_RANK
    o_ca = o_kpe + MLA_ROPE_DIM
    o_cg = o_ca + CONV_CH
    reps = LANES // 2 // ROPE_HALF
    kpe = np.concatenate([np.tile(o_kpe + np.arange(ROPE_HALF), reps),
                          np.tile(o_kpe + ROPE_HALF + np.arange(ROPE_HALF), reps)])
    return np.concatenate([np.arange(o_cq), o_cq + np.arange(MLA_Q_RANK),
                           o_ckv + np.arange(MLA_KV_RANK), kpe,
                           o_ca + np.arange(CONV_CH), o_cg + np.arange(CONV_CH)])


def _uq_columns():
    per = MLA_NOPE_DIM + MLA_ROPE_DIM
    nope = np.concatenate([h * per + np.arange(MLA_NOPE_DIM) for h in range(MLA_HEADS)])
    rope = []
    grp = LANES // 2 // ROPE_HALF
    for g0 in range(0, MLA_HEADS, grp):
        for half in range(2):
            for h in range(g0, g0 + grp):
                rope.append(h * per + MLA_NOPE_DIM + half * ROPE_HALF + np.arange(ROPE_HALF))
    return np.concatenate([nope] + rope)


def _ukv_columns():
    per = MLA_NOPE_DIM + MLA_V_DIM
    kn = np.concatenate([h * per + np.arange(MLA_NOPE_DIM) for h in range(MLA_HEADS)])
    vv = np.concatenate([h * per + MLA_NOPE_DIM + np.arange(MLA_V_DIM) for h in range(MLA_HEADS)])
    return np.concatenate([kn, vv])


def _rope_tables(seq):
    lp = seq + N_META
    inv = 1.0 / (ROPE_THETA ** (jnp.arange(0, MLA_ROPE_DIM, 2, dtype=F32) / MLA_ROPE_DIM))
    pos = jnp.concatenate([N_META + jnp.arange(seq, dtype=F32), jnp.arange(N_META, dtype=F32)])
    ang = pos[:, None] * inv[None, :]
    reps = LANES // 2 // ROPE_HALF
    cos = jnp.tile(jnp.cos(ang), (1, 2 * reps))
    sin = jnp.tile(jnp.sin(ang), (1, reps))
    assert cos.shape == (lp, LANES)
    return cos, jnp.concatenate([-sin, sin], axis=1)


ATTN_TQ = 256
EXPERT_TM = 256
ROUTE_TM = 128


def kernel(x, meta_tokens, ln_in_g, ln_in_b, w_in, q_norm_g, w_uq, kv_norm_g, w_ukv, conv_w, conv_b,
           conv_ln_g, conv_ln_b, grp_norm_g, w_out, ln_mix_g, ln_mix_b, router_w, router_b,
           w_gate_up, b_gate_up, w_down, b_down, ln_ffn_g, ln_ffn_b):
    b, seq, d = x.shape
    depth = w_in.shape[0]
    lp = seq + N_META
    t = b * lp
    alpha = float((2 * depth) ** 0.25)
    tq = min(ATTN_TQ, seq)
    assert seq % tq == 0 and tq % LANES == 0 and seq % CONV_ROWS == 0
    tm_tok = _row_tile(t, 512, LANES)
    tm_seq = _row_tile(lp, 1024)
    tm_route = _row_tile(t, ROUTE_TM, 8)
    n_assign = t * TOP_K
    n_tiles = -(-n_assign // EXPERT_TM) + N_EXPERTS + 1
    n_rows = n_tiles * EXPERT_TM
    n_chunks = -(-n_assign // INV_CHUNK)

    meta = jnp.broadcast_to(meta_tokens[None].astype(x.dtype), (b, N_META, d))
    xin = jnp.concatenate([x, meta], axis=1).reshape(t, d)
    h = _ln_call(xin, ln_in_g, ln_in_b, tm_tok)
    cos_t, sin_t = _rope_tables(seq)

    in_cols = _in_proj_columns()
    uq_cols, ukv_cols = _uq_columns(), _ukv_columns()

    for l in range(depth):
        w_in_p = w_in[l][:, in_cols].astype(BF16)
        sbqk, sbv, mq, mk, mv, cu = _inproj_call(
            h.reshape(b, lp, d), w_in_p, q_norm_g[l], w_uq[l][:, uq_cols].astype(BF16),
            kv_norm_g[l], w_ukv[l][:, ukv_cols].astype(BF16), cos_t, sin_t, tm_seq)
        sb_out = _sb_call(sbqk, sbv.transpose(0, 2, 1), sbv[:, seq:, :], tq)
        mla_out = _mla_call(mq, mk, mv.transpose(0, 2, 1), mv[:, seq:, :], tq)
        conv_out = _conv_call(cu, conv_w[l], conv_b[l], conv_ln_g[l], conv_ln_b[l])
        h1, h1_tiles, idx, gates, rank, counts = _mix_call(
            sb_out.reshape(t, -1), mla_out.reshape(t, -1), conv_out.reshape(t, -1), h,
            grp_norm_g[l], w_out[l].astype(BF16), ln_mix_g[l], ln_mix_b[l],
            router_w[l].astype(BF16), router_b[l], alpha, tm_tok)

        cnt = counts[0].astype(jnp.int32)
        tiles_e = (cnt + EXPERT_TM - 1) // EXPERT_TM
        tile_end = jnp.cumsum(tiles_e)
        tile_start = tile_end - tiles_e
        pos = tile_start[idx[:, :TOP_K]] * EXPERT_TM + rank[:, :TOP_K]
        pos_flat = pos.reshape(-1).astype(jnp.int32)
        n_valid = tile_end[-1:].astype(jnp.int32)
        tile_ids = jnp.minimum(jnp.arange(n_tiles, dtype=jnp.int32), n_valid[0] - 1)
        tile_expert = jnp.minimum(jnp.sum(tile_end[None, :] <= tile_ids[:, None], axis=1),
                                  N_EXPERTS - 1).astype(jnp.int32)

        pos_chunks = jnp.concatenate(
            [pos_flat, jnp.full((n_chunks * INV_CHUNK - n_assign,), n_rows, jnp.int32)]
        ).reshape(n_chunks, INV_CHUNK)
        row_tok = _invert_call(cnt, tile_start.astype(jnp.int32), pos_chunks, n_rows + INV_CHUNK,
                               EXPERT_TM)
        ys = _expert_call(tile_expert, n_valid, row_tok, h1_tiles, w_gate_up[l], b_gate_up[l],
                          w_down[l], b_down[l], n_tiles, EXPERT_TM)
        h = _combine_call(pos_flat, ys, gates, h1, ln_ffn_g[l], ln_ffn_b[l], alpha, tm_route)

    return h.reshape(b, lp, d)[:, :seq, :]
```

```python
import functools

import jax
import jax.numpy as jnp
import numpy as np
from jax import lax
from jax.experimental import pallas as pl
from jax.experimental.pallas import tpu as pltpu

F32 = jnp.float32
BF16 = jnp.bfloat16

N_META = 16
SB_HEADS = 4
SB_HEAD_DIM = 64
SB_WIDTH = SB_HEADS * SB_HEAD_DIM
MLA_HEADS = 8
MLA_NOPE_DIM = 64
MLA_ROPE_DIM = 32
MLA_V_DIM = 64
MLA_Q_RANK = 256
MLA_KV_RANK = 128
MLA_WIDTH = MLA_HEADS * MLA_V_DIM
ROPE_THETA = 10000.0
CONV_CH = 256
CONV_K = 31
N_EXPERTS = 32
TOP_K = 4
SWIGLU_LIMIT = 7.0
SWIGLU_ALPHA = 1.702
LN_EPS = 1e-5
RMS_EPS = 1e-6

LANES = 128
HEAD_PAIR = LANES // SB_HEAD_DIM
ROPE_HALF = MLA_ROPE_DIM // 2
VMEM_LIMIT = 48 * 1024 * 1024
EXPERT_VMEM_LIMIT = 56 * 1024 * 1024
NEG_BIG = -1e30
LOG2E = 1.4426950408889634
ATTN_TK = 256

IN_SB = 3 * SB_WIDTH
IN_CQ = IN_SB
IN_CKV = IN_CQ + MLA_Q_RANK
IN_KPE = IN_CKV + MLA_KV_RANK
IN_CA = IN_KPE + LANES
IN_CG = IN_CA + CONV_CH
IN_TOTAL = IN_CG + CONV_CH
MQ_W = MLA_HEADS * MLA_NOPE_DIM + 2 * LANES
MK_W = MLA_HEADS * MLA_NOPE_DIM + LANES


def _row_tile(n, cap, mult=16):
    best = None
    for t in range(mult, min(n, cap) + 1, mult):
        if n % t == 0:
            best = t
    assert best is not None, (n, cap, mult)
    return best


def _layer_norm(x, g, b):
    mu = jnp.mean(x, axis=-1, keepdims=True)
    xc = x - mu
    var = jnp.mean(xc * xc, axis=-1, keepdims=True)
    return xc * lax.rsqrt(var + LN_EPS) * g + b


def _rms_norm(x, g):
    return x * lax.rsqrt(jnp.mean(x * x, axis=-1, keepdims=True) + RMS_EPS) * g


def _sigmoid(x):
    return 1.0 / (1.0 + jnp.exp(-x))


def _ln_kernel(x_ref, g_ref, b_ref, o_ref):
    o_ref[...] = _layer_norm(x_ref[...], g_ref[...], b_ref[...])


def _ln_call(x2d, g, b, tm):
    t, d = x2d.shape
    return pl.pallas_call(
        _ln_kernel,
        out_shape=jax.ShapeDtypeStruct((t, d), F32),
        grid=(t // tm,),
        in_specs=[pl.BlockSpec((tm, d), lambda i: (i, 0)),
                  pl.BlockSpec((1, d), lambda i: (0, 0)),
                  pl.BlockSpec((1, d), lambda i: (0, 0))],
        out_specs=pl.BlockSpec((tm, d), lambda i: (i, 0)),
        compiler_params=pltpu.CompilerParams(dimension_semantics=("parallel",),
                                             vmem_limit_bytes=VMEM_LIMIT),
        name="ln_in",
    )(x2d, g.reshape(1, d), b.reshape(1, d))


def _inproj_kernel(h_ref, w_ref, qg_ref, wuq_ref, kvg_ref, wukv_ref, cos_ref, sin_ref,
                   sbqk_ref, sbv_ref, mq_ref, mk_ref, mv_ref, cu_ref):
    h = h_ref[0].astype(BF16)
    proj = jnp.dot(h, w_ref[...], preferred_element_type=F32)
    sbqk_ref[0, :, :SB_WIDTH] = (proj[:, :SB_WIDTH] * (SB_HEAD_DIM ** -0.5 * LOG2E)).astype(BF16)
    sbqk_ref[0, :, SB_WIDTH:] = proj[:, SB_WIDTH:2 * SB_WIDTH].astype(BF16)
    sbv_ref[0] = proj[:, 2 * SB_WIDTH:IN_SB].astype(BF16)
    cos = cos_ref[...]
    sin = sin_ref[...]

    def rot(x):
        return x * cos + pltpu.roll(x, LANES // 2, 1) * sin

    cq = _rms_norm(proj[:, IN_CQ:IN_CKV], qg_ref[...])
    qm = jnp.dot(cq.astype(BF16), wuq_ref[...], preferred_element_type=F32)
    qm = qm * ((MLA_NOPE_DIM + MLA_ROPE_DIM) ** -0.5 * LOG2E)
    nope_w = MLA_HEADS * MLA_NOPE_DIM
    mq_ref[0, :, :nope_w] = qm[:, :nope_w].astype(BF16)
    mq_ref[0, :, nope_w:nope_w + LANES] = rot(qm[:, nope_w:nope_w + LANES]).astype(BF16)
    mq_ref[0, :, nope_w + LANES:] = rot(qm[:, nope_w + LANES:]).astype(BF16)

    ckv = _rms_norm(proj[:, IN_CKV:IN_KPE], kvg_ref[...])
    kv = jnp.dot(ckv.astype(BF16), wukv_ref[...], preferred_element_type=F32)
    mk_ref[0, :, :nope_w] = kv[:, :nope_w].astype(BF16)
    mk_ref[0, :, nope_w:] = rot(proj[:, IN_KPE:IN_CA]).astype(BF16)
    mv_ref[0] = kv[:, nope_w:].astype(BF16)

    cu_ref[0] = proj[:, IN_CA:IN_CG] * _sigmoid(proj[:, IN_CG:IN_TOTAL])


def _inproj_call(h3, w_in_p, qg, wuq_p, kvg, wukv_p, cos_t, sin_t, tm):
    b, lp, d = h3.shape
    nt = lp // tm
    const2 = lambda bi, ti: (0, 0)
    tok3 = lambda bi, ti: (bi, ti, 0)
    outs = (jax.ShapeDtypeStruct((b, lp, 2 * SB_WIDTH), BF16),
            jax.ShapeDtypeStruct((b, lp, SB_WIDTH), BF16),
            jax.ShapeDtypeStruct((b, lp, MQ_W), BF16),
            jax.ShapeDtypeStruct((b, lp, MK_W), BF16),
            jax.ShapeDtypeStruct((b, lp, MLA_WIDTH), BF16),
            jax.ShapeDtypeStruct((b, lp, CONV_CH), F32))
    return pl.pallas_call(
        _inproj_kernel,
        out_shape=outs,
        grid=(b, nt),
        in_specs=[pl.BlockSpec((1, tm, d), tok3),
                  pl.BlockSpec(w_in_p.shape, const2),
                  pl.BlockSpec((1, MLA_Q_RANK), const2),
                  pl.BlockSpec(wuq_p.shape, const2),
                  pl.BlockSpec((1, MLA_KV_RANK), const2),
                  pl.BlockSpec(wukv_p.shape, const2),
                  pl.BlockSpec((tm, LANES), lambda bi, ti: (ti, 0)),
                  pl.BlockSpec((tm, LANES), lambda bi, ti: (ti, 0))],
        out_specs=tuple(pl.BlockSpec((1, tm, s.shape[2]), tok3) for s in outs),
        compiler_params=pltpu.CompilerParams(dimension_semantics=("parallel", "parallel"),
                                             vmem_limit_bytes=VMEM_LIMIT),
        name="in_proj",
    )(h3, w_in_p, qg.reshape(1, -1), wuq_p, kvg.reshape(1, -1), wukv_p, cos_t, sin_t)


def _log_stay2(z):
    nz = -z
    return jnp.minimum(nz, 0.0) - jnp.log2(1.0 + jnp.exp2(jnp.minimum(z, nz)))


def _split_bf16(x):
    hi = x.astype(BF16)
    return hi, (x - hi.astype(F32)).astype(BF16)


def _sb_kernel(q_ref, k_ref, vt_ref, vmeta_ref, o_ref, acc_ref, c_ref, *, tq, seq):
    i = pl.program_id(1)
    nq = seq // tq
    npair = SB_HEADS // HEAD_PAIR
    lane = lax.broadcasted_iota(jnp.int32, (1, LANES), 1)
    head_sel = (lane < SB_HEAD_DIM, lane >= SB_HEAD_DIM)
    top_rows = lax.broadcasted_iota(jnp.int32, (LANES, 1), 0) < SB_HEAD_DIM
    meta = pl.ds(seq, N_META)

    def iota2(rows, cols):
        return (lax.broadcasted_iota(jnp.int32, (rows, cols), 0),
                lax.broadcasted_iota(jnp.int32, (rows, cols), 1))

    def tri_down(w):
        r, c = iota2(w, w)
        return jnp.where(c >= r, 1.0, 0.0).astype(BF16)

    def tile_all(q, krows, mask, ut, first):
        zs = []
        for p in range(npair):
            cols = slice(p * LANES, (p + 1) * LANES)
            kt = k_ref[0, krows, cols]
            for hd in range(HEAD_PAIR):
                qh = jnp.where(head_sel[hd], q[:, cols], jnp.zeros((1, 1), BF16))
                zs.append(lax.dot_general(kt, qh, (((1,), (1,)), ((), ())),
                                          preferred_element_type=F32))
        splits = []
        for h in range(SB_HEADS):
            ls = _log_stay2(zs[h])
            if mask is not None:
                ls = jnp.where(mask, ls, 0.0)
            splits.append(_split_bf16(ls))
        cums = [jnp.dot(ut, hi, preferred_element_type=F32) + jnp.dot(ut, lo, preferred_element_type=F32)
                for hi, lo in splits]
        ws = []
        for h in range(SB_HEADS):
            lw = zs[h] + cums[h] if first else zs[h] + cums[h] + c_ref[h]
            w = jnp.exp2(lw)
            if mask is not None:
                w = jnp.where(mask, w, 0.0)
            ws.append(w.astype(BF16))
            c_ref[h] = cums[h][0:1, :] if first else c_ref[h] + cums[h][0:1, :]
        for p in range(npair):
            vt = vt_ref[0, p * LANES:(p + 1) * LANES, krows]
            res = [jnp.dot(vt, ws[p * HEAD_PAIR + hd], preferred_element_type=F32)
                   for hd in range(HEAD_PAIR)]
            contrib = jnp.where(top_rows, res[0], res[1])
            acc_ref[p] = contrib if first else acc_ref[p] + contrib

    @pl.when(i < nq)
    def _():
        q = q_ref[0]
        ut = tri_down(ATTN_TK)
        r, c = iota2(ATTN_TK, tq)
        per_q = tq // ATTN_TK
        for d in range(per_q - 1, -1, -1):
            tile_all(q, pl.ds(pl.multiple_of(i * tq + d * ATTN_TK, ATTN_TK), ATTN_TK),
                     r + d * ATTN_TK < c, ut, d == per_q - 1)

        def body(t, carry):
            start = pl.multiple_of((i * per_q - 1 - t) * ATTN_TK, ATTN_TK)
            tile_all(q, pl.ds(start, ATTN_TK), None, ut, False)
            return carry

        lax.fori_loop(0, i * per_q, body, 0)
        tile_all(q, meta, None, tri_down(N_META), False)
        for p in range(npair):
            o_ref[0, :, p * LANES:(p + 1) * LANES] = acc_ref[p].T.astype(o_ref.dtype)

    @pl.when(i == nq)
    def _():
        r, c = iota2(N_META, N_META)
        mask = c < r
        ut = jnp.where(r >= c, 1.0, 0.0).astype(BF16)
        for p in range(npair):
            cols = slice(p * LANES, (p + 1) * LANES)
            q = q_ref[0, 0:N_META, cols]
            kt = k_ref[0, meta, cols]
            vm = vmeta_ref[0, :, cols]
            res = []
            for hd in range(HEAD_PAIR):
                qh = jnp.where(head_sel[hd], q, jnp.zeros((1, 1), BF16))
                z = lax.dot_general(qh, kt, (((1,), (1,)), ((), ())), preferred_element_type=F32)
                hi, lo = _split_bf16(jnp.where(mask, _log_stay2(z), 0.0))
                cum = (jnp.dot(hi, ut, preferred_element_type=F32)
                       + jnp.dot(lo, ut, preferred_element_type=F32))
                w = jnp.where(mask, jnp.exp2(z + cum), 0.0)
                res.append(jnp.dot(w.astype(BF16), vm, preferred_element_type=F32))
            o_ref[0, 0:N_META, cols] = jnp.where(head_sel[0], res[0], res[1]).astype(o_ref.dtype)


def _sb_call(sbqk, sbv_t, sbv_meta, tq):
    b, lp, _ = sbqk.shape
    seq = lp - N_META
    nq = seq // tq
    return pl.pallas_call(
        functools.partial(_sb_kernel, tq=tq, seq=seq),
        out_shape=jax.ShapeDtypeStruct((b, lp, SB_WIDTH), BF16),
        grid=(b, nq + 1),
        in_specs=[pl.BlockSpec((1, tq, SB_WIDTH), lambda bi, i: (bi, i, 0)),
                  pl.BlockSpec((1, lp, SB_WIDTH), lambda bi, i: (bi, 0, 1)),
                  pl.BlockSpec((1, SB_WIDTH, lp), lambda bi, i: (bi, 0, 0)),
                  pl.BlockSpec((1, N_META, SB_WIDTH), lambda bi, i: (bi, 0, 0))],
        out_specs=pl.BlockSpec((1, tq, SB_WIDTH), lambda bi, i: (bi, i, 0)),
        scratch_shapes=[pltpu.VMEM((SB_HEADS // HEAD_PAIR, LANES, tq), F32),
                        pltpu.VMEM((SB_HEADS, 1, tq), F32)],
        compiler_params=pltpu.CompilerParams(
            dimension_semantics=("parallel", "arbitrary"),
            vmem_limit_bytes=VMEM_LIMIT),
        name="sb_attn",
    )(sbqk, sbqk, sbv_t, sbv_meta)


def _mla_kernel(q_ref, k_ref, vt_ref, vmeta_ref, o_ref, acc_ref, m_ref, *, tq, seq):
    i = pl.program_id(1)
    nq = seq // tq
    npair = MLA_HEADS // HEAD_PAIR
    nope_w = MLA_HEADS * MLA_NOPE_DIM
    grp = LANES // 2 // ROPE_HALF
    lane = lax.broadcasted_iota(jnp.int32, (1, LANES), 1)
    lane2 = lax.broadcasted_iota(jnp.int32, (1, 2 * LANES), 1)
    top_rows = lax.broadcasted_iota(jnp.int32, (LANES, 1), 0) < MLA_V_DIM
    meta = pl.ds(seq, N_META)
    one = jnp.ones((1, 1), BF16)

    def head_lanes(h):
        n0 = (h % HEAD_PAIR) * MLA_NOPE_DIM
        r0 = LANES + (h % grp) * ROPE_HALF
        r1 = r0 + LANES // 2
        return (((lane2 >= n0) & (lane2 < n0 + MLA_NOPE_DIM))
                | ((lane2 >= r0) & (lane2 < r0 + ROPE_HALF))
                | ((lane2 >= r1) & (lane2 < r1 + ROPE_HALF)))

    def qcat(q, p):
        rope0 = nope_w + (p * HEAD_PAIR // grp) * LANES
        return jnp.concatenate([q[:, p * LANES:(p + 1) * LANES], q[:, rope0:rope0 + LANES]], axis=-1)

    def kcat(krows, p):
        return jnp.concatenate([k_ref[0, krows, p * LANES:(p + 1) * LANES],
                                k_ref[0, krows, nope_w:nope_w + LANES]], axis=-1)

    def iota2(rows, cols):
        return (lax.broadcasted_iota(jnp.int32, (rows, cols), 0),
                lax.broadcasted_iota(jnp.int32, (rows, cols), 1))

    def tile_all(q, krows, mask, first):
        scores = []
        for p in range(npair):
            kc = kcat(krows, p)
            qc = qcat(q, p)
            for hd in range(HEAD_PAIR):
                qh = jnp.where(head_lanes(p * HEAD_PAIR + hd), qc, jnp.zeros((1, 1), BF16))
                scores.append(lax.dot_general(kc, qh, (((1,), (1,)), ((), ())),
                                              preferred_element_type=F32))
        pexps, alphas = [], []
        for h in range(MLA_HEADS):
            s = scores[h] if mask is None else jnp.where(mask, scores[h], NEG_BIG)
            mx = jnp.max(s, axis=0, keepdims=True)
            if first:
                m_new = mx
                alphas.append(None)
            else:
                m_old = m_ref[h]
                m_new = jnp.maximum(m_old, mx)
                alphas.append(jnp.exp2(m_old - m_new))
            m_ref[h] = m_new
            pexps.append(jnp.exp2(s - m_new).astype(BF16))
        for h in range(MLA_HEADS):
            p, hd = divmod(h, HEAD_PAIR)
            vt = vt_ref[0, p * LANES:(p + 1) * LANES, krows]
            vte = jnp.where(top_rows, vt, one) if hd == 0 else jnp.where(top_rows, one, vt)
            pv = jnp.dot(vte, pexps[h], preferred_element_type=F32)
            acc_ref[h] = pv if first else acc_ref[h] * alphas[h] + pv

    @pl.when(i < nq)
    def _():
        q = q_ref[0]
        r, c = iota2(ATTN_TK, tq)
        per_q = tq // ATTN_TK
        for d in range(per_q):
            tile_all(q, pl.ds(pl.multiple_of(i * tq + d * ATTN_TK, ATTN_TK), ATTN_TK),
                     r + d * ATTN_TK <= c, d == 0)

        def body(t, carry):
            tile_all(q, pl.ds(pl.multiple_of(t * ATTN_TK, ATTN_TK), ATTN_TK), None, False)
            return carry

        lax.fori_loop(0, i * per_q, body, 0)
        tile_all(q, meta, None, False)
        for p in range(npair):
            a = acc_ref[p * HEAD_PAIR]
            b = acc_ref[p * HEAD_PAIR + 1]
            out_t = jnp.where(top_rows, a * (1.0 / a[MLA_V_DIM:MLA_V_DIM + 1, :]), b * (1.0 / b[0:1, :]))
            o_ref[0, :, p * LANES:(p + 1) * LANES] = out_t.T.astype(o_ref.dtype)

    @pl.when(i == nq)
    def _():
        q = q_ref[0, 0:N_META, :]
        r, c = iota2(N_META, N_META)
        for p in range(npair):
            kc = kcat(meta, p)
            qc = qcat(q, p)
            vm = vmeta_ref[0, :, p * LANES:(p + 1) * LANES]
            res = []
            for hd in range(HEAD_PAIR):
                qh = jnp.where(head_lanes(p * HEAD_PAIR + hd), qc, jnp.zeros((1, 1), BF16))
                s = lax.dot_general(qh, kc, (((1,), (1,)), ((), ())), preferred_element_type=F32)
                s = jnp.where(c <= r, s, NEG_BIG)
                pexp = jnp.exp2(s - jnp.max(s, axis=1, keepdims=True))
                pv = jnp.dot(pexp.astype(BF16), vm, preferred_element_type=F32)
                res.append(pv * (1.0 / jnp.sum(pexp, axis=1, keepdims=True)))
            o_ref[0, 0:N_META, p * LANES:(p + 1) * LANES] = jnp.where(
                lane < MLA_V_DIM, res[0], res[1]).astype(o_ref.dtype)


def _mla_call(mq, mk, mv_t, mv_meta, tq):
    b, lp, _ = mq.shape
    seq = lp - N_META
    nq = seq // tq
    return pl.pallas_call(
        functools.partial(_mla_kernel, tq=tq, seq=seq),
        out_shape=jax.ShapeDtypeStruct((b, lp, MLA_WIDTH), BF16),
        grid=(b, nq + 1),
        in_specs=[pl.BlockSpec((1, tq, MQ_W), lambda bi, i: (bi, i, 0)),
                  pl.BlockSpec((1, lp, MK_W), lambda bi, i: (bi, 0, 0)),
                  pl.BlockSpec((1, MLA_WIDTH, lp), lambda bi, i: (bi, 0, 0)),
                  pl.BlockSpec((1, N_META, MLA_WIDTH), lambda bi, i: (bi, 0, 0))],
        out_specs=pl.BlockSpec((1, tq, MLA_WIDTH), lambda bi, i: (bi, i, 0)),
        scratch_shapes=[pltpu.VMEM((MLA_HEADS, LANES, tq), F32),
                        pltpu.VMEM((MLA_HEADS, 1, tq), F32)],
        compiler_params=pltpu.CompilerParams(
            dimension_semantics=("parallel", "arbitrary"),
            vmem_limit_bytes=VMEM_LIMIT),
        name="mla_attn",
    )(mq, mk, mv_t, mv_meta)


CONV_PAD = 32
CONV_ROWS = 128
SUBLANES = 8
CONV_WIN_EXTRA = CONV_PAD


def _conv_kernel(u_ref, w_ref, b_ref, g_ref, beta_ref, o_ref, buf_ref, *, seq):
    buf_ref[0:CONV_PAD, :] = jnp.zeros((CONV_PAD, CONV_CH), F32)
    buf_ref[CONV_PAD:CONV_PAD + N_META, :] = u_ref[0, seq:seq + N_META, :]
    buf_ref[CONV_PAD + N_META:CONV_PAD + N_META + seq, :] = u_ref[0, 0:seq, :]
    w = w_ref[...]
    lead = CONV_PAD - (CONV_K - 1)

    def finish(acc):
        y = _layer_norm(acc + b_ref[...], g_ref[...], beta_ref[...])
        return (y * _sigmoid(y)).astype(o_ref.dtype)

    def conv_rows(first_pos, rows):
        win = buf_ref[pl.ds(first_pos, rows + CONV_WIN_EXTRA), :]
        acc = jnp.zeros((rows, CONV_CH), F32)
        for sh in range(SUBLANES):
            offs = [o for o in range(lead, lead + CONV_K) if o % SUBLANES == sh]
            shifted = win[sh:sh + rows + offs[-1] - sh, :]
            for o in offs:
                k = o - lead
                acc = acc + shifted[o - sh:o - sh + rows, :] * w[k:k + 1, :]
        return finish(acc)

    def body(c, _):
        r0 = pl.multiple_of(c * CONV_ROWS, CONV_ROWS)
        o_ref[0, pl.ds(r0, CONV_ROWS), :] = conv_rows(pl.multiple_of(r0 + N_META, SUBLANES), CONV_ROWS)
        return 0

    lax.fori_loop(0, seq // CONV_ROWS, body, 0)
    o_ref[0, seq:seq + N_META, :] = conv_rows(0, N_META)


def _conv_call(cu, conv_w, conv_b, ln_g, ln_b):
    b, lp, _ = cu.shape
    seq = lp - N_META
    const2 = lambda bi: (0, 0)
    return pl.pallas_call(
        functools.partial(_conv_kernel, seq=seq),
        out_shape=jax.ShapeDtypeStruct((b, lp, CONV_CH), BF16),
        grid=(b,),
        in_specs=[pl.BlockSpec((1, lp, CONV_CH), lambda bi: (bi, 0, 0)),
                  pl.BlockSpec((CONV_K, CONV_CH), const2),
                  pl.BlockSpec((1, CONV_CH), const2),
                  pl.BlockSpec((1, CONV_CH), const2),
                  pl.BlockSpec((1, CONV_CH), const2)],
        out_specs=pl.BlockSpec((1, lp, CONV_CH), lambda bi: (bi, 0, 0)),
        scratch_shapes=[pltpu.VMEM((CONV_PAD + lp, CONV_CH), F32)],
        compiler_params=pltpu.CompilerParams(dimension_semantics=("parallel",),
                                             vmem_limit_bytes=VMEM_LIMIT),
        name="conv",
    )(cu, conv_w, conv_b.reshape(1, -1), ln_g.reshape(1, -1), ln_b.reshape(1, -1))


def _mix_kernel(sb_ref, mla_ref, cv_ref, h_ref, gg_ref, wo_ref, lg_ref, lb_ref, rw_ref, rb_ref,
                h1_ref, h1t_ref, idx_ref, gate_ref, rank_ref, cnt_ref, carry_ref, *, alpha, tm):
    step = pl.program_id(0)

    @pl.when(step == 0)
    def _():
        carry_ref[...] = jnp.zeros_like(carry_ref)

    gg = gg_ref[...]
    y = jnp.concatenate(
        [_rms_norm(sb_ref[...].astype(F32), gg[:, :SB_WIDTH]),
         _rms_norm(mla_ref[...].astype(F32), gg[:, SB_WIDTH:SB_WIDTH + MLA_WIDTH]),
         _rms_norm(cv_ref[...].astype(F32), gg[:, SB_WIDTH + MLA_WIDTH:])], axis=-1)
    mix = jnp.dot(y.astype(BF16), wo_ref[...], preferred_element_type=F32)
    h1 = _layer_norm(alpha * h_ref[...] + mix, lg_ref[...], lb_ref[...])
    h1_ref[...] = h1
    _store_token_tiles(h1t_ref, h1)

    logits = jnp.dot(h1.astype(BF16), rw_ref[...], preferred_element_type=F32) + rb_ref[...]
    eiota = lax.broadcasted_iota(jnp.int32, (tm, N_EXPERTS), 1)
    lane = lax.broadcasted_iota(jnp.int32, (tm, LANES), 1)
    vals = logits
    sels, tops, idxs = [], [], []
    for _ in range(TOP_K):
        m = jnp.max(vals, axis=1, keepdims=True)
        idx = jnp.min(jnp.where(vals == m, eiota, N_EXPERTS), axis=1, keepdims=True)
        sel = eiota == idx
        vals = jnp.where(sel, -jnp.inf, vals)
        sels.append(sel)
        tops.append(m)
        idxs.append(idx)
    exps = [jnp.exp(t - tops[0]) for t in tops]
    denom = exps[0] + exps[1] + exps[2] + exps[3]

    chosen = jnp.zeros((tm, N_EXPERTS), F32)
    for sel in sels:
        chosen = chosen + jnp.where(sel, 1.0, 0.0)
    r = lax.broadcasted_iota(jnp.int32, (tm, tm), 0)
    c = lax.broadcasted_iota(jnp.int32, (tm, tm), 1)
    below = jnp.where(c < r, 1.0, 0.0).astype(BF16)
    earlier = jnp.dot(below, chosen.astype(BF16), preferred_element_type=F32) + carry_ref[...]

    idx_out = jnp.zeros((tm, LANES), jnp.int32)
    gate_out = jnp.zeros((tm, LANES), F32)
    rank_out = jnp.zeros((tm, LANES), jnp.int32)
    for k in range(TOP_K):
        rank_k = jnp.sum(jnp.where(sels[k], earlier, 0.0), axis=1, keepdims=True)
        idx_out = jnp.where(lane == k, idxs[k], idx_out)
        gate_out = jnp.where(lane == k, exps[k] / denom, gate_out)
        rank_out = jnp.where(lane == k, rank_k.astype(jnp.int32), rank_out)
    idx_ref[...] = idx_out
    gate_ref[...] = gate_out
    rank_ref[...] = rank_out

    carry_ref[...] = carry_ref[...] + jnp.sum(chosen, axis=0, keepdims=True)
    cnt_ref[...] = carry_ref[...]


def _mix_call(sb_out, mla_out, conv_out, h2d, grp_g, w_out_b, ln_g, ln_b, router_w_b, router_b,
              alpha, tm):
    t, d = h2d.shape
    row = lambda i: (i, 0)
    const2 = lambda i: (0, 0)
    chunks = d // LANES
    outs = (jax.ShapeDtypeStruct((t, d), F32),
            jax.ShapeDtypeStruct((t * chunks, LANES), F32),
            jax.ShapeDtypeStruct((t, LANES), jnp.int32),
            jax.ShapeDtypeStruct((t, LANES), F32),
            jax.ShapeDtypeStruct((t, LANES), jnp.int32),
            jax.ShapeDtypeStruct((1, N_EXPERTS), F32))
    return pl.pallas_call(
        functools.partial(_mix_kernel, alpha=alpha, tm=tm),
        out_shape=outs,
        grid=(t // tm,),
        in_specs=[pl.BlockSpec((tm, SB_WIDTH), row),
                  pl.BlockSpec((tm, MLA_WIDTH), row),
                  pl.BlockSpec((tm, CONV_CH), row),
                  pl.BlockSpec((tm, d), row),
                  pl.BlockSpec((1, d), const2),
                  pl.BlockSpec((d, d), const2),
                  pl.BlockSpec((1, d), const2),
                  pl.BlockSpec((1, d), const2),
                  pl.BlockSpec((d, N_EXPERTS), const2),
                  pl.BlockSpec((1, N_EXPERTS), const2)],
        out_specs=(pl.BlockSpec((tm, d), row),
                   pl.BlockSpec((tm * chunks, LANES), row),
                   pl.BlockSpec((tm, LANES), row),
                   pl.BlockSpec((tm, LANES), row),
                   pl.BlockSpec((tm, LANES), row),
                   pl.BlockSpec((1, N_EXPERTS), const2)),
        scratch_shapes=[pltpu.VMEM((1, N_EXPERTS), F32)],
        compiler_params=pltpu.CompilerParams(dimension_semantics=("arbitrary",),
                                             vmem_limit_bytes=VMEM_LIMIT),
        name="mix_router",
    )(sb_out, mla_out, conv_out, h2d, grp_g.reshape(1, d), w_out_b, ln_g.reshape(1, d),
      ln_b.reshape(1, d), router_w_b, router_b.reshape(1, N_EXPERTS))


GATHER_AHEAD = 2
GATHER_BUFS = GATHER_AHEAD + 1


def _store_token_tiles(ref, val):
    tm, d = val.shape
    chunks = d // LANES
    for c in range(chunks):
        ref[pl.ds(c, tm, stride=chunks), :] = val[:, c * LANES:(c + 1) * LANES]


def _load_token_tiles(ref, tm, chunks):
    return jnp.concatenate([ref[pl.ds(c, tm, stride=chunks), :] for c in range(chunks)], axis=-1)


INV_CHUNK = 1024


def _invert_kernel(cnt_ref, start_ref, pos_hbm, tok_ref, chunk_ref, sem, *, n_chunks, n_out, tm):
    def fill(lo, hi):
        def body(r, _):
            tok_ref[r] = 0
            return 0
        lax.fori_loop(lo, hi, body, 0)

    def pad_rows(e, used_end):
        tiles_e = (cnt_ref[e] + (tm - 1)) // tm
        fill(start_ref[e] * tm + cnt_ref[e], (start_ref[e] + tiles_e) * tm)
        return jnp.maximum(used_end, (start_ref[e] + tiles_e) * tm)

    used_end = lax.fori_loop(0, N_EXPERTS, pad_rows, 0)
    fill(used_end, n_out)

    def chunk_copy(ch, slot):
        return pltpu.make_async_copy(pos_hbm.at[ch], chunk_ref.at[pl.ds(slot * INV_CHUNK, INV_CHUNK)],
                                     sem.at[slot])

    chunk_copy(0, 0).start()
    toks = INV_CHUNK // TOP_K

    def chunk(ch, _):
        slot = ch % 2
        chunk_copy(ch, slot).wait()

        @pl.when(ch + 1 < n_chunks)
        def _():
            chunk_copy(ch + 1, 1 - slot).start()

        first = slot * INV_CHUNK

        def body(tl, _):
            for k in range(TOP_K):
                tok_ref[chunk_ref[first + tl * TOP_K + k]] = ch * toks + tl
            return 0

        lax.fori_loop(0, toks, body, 0, unroll=4)
        return 0

    lax.fori_loop(0, n_chunks, chunk, 0)


def _invert_call(cnt, tile_start, pos_chunks, n_out, tm):
    n_chunks = pos_chunks.shape[0]
    return pl.pallas_call(
        functools.partial(_invert_kernel, n_chunks=n_chunks, n_out=n_out, tm=tm),
        out_shape=jax.ShapeDtypeStruct((n_out,), jnp.int32),
        grid_spec=pltpu.PrefetchScalarGridSpec(
            num_scalar_prefetch=2,
            grid=(1,),
            in_specs=[pl.BlockSpec(memory_space=pl.ANY)],
            out_specs=pl.BlockSpec(memory_space=pltpu.SMEM),
            scratch_shapes=[pltpu.SMEM((2 * INV_CHUNK,), jnp.int32),
                            pltpu.SemaphoreType.DMA((2,))]),
        compiler_params=pltpu.CompilerParams(dimension_semantics=("arbitrary",)),
        name="moe_row_table",
    )(cnt, tile_start, pos_chunks)


def _expert_kernel(te_ref, nv_ref, tok_ref, h_hbm, wgu_ref, bgu_ref, wd_ref, bd_ref, y_ref,
                   xbuf, wgu_bf, wd_bf, sem, *, tm, d, d_ff):
    i = pl.program_id(0)
    nv = nv_ref[0]
    chunks = d // LANES

    def row_copy(tile, r, slot):
        tok = tok_ref[tile * tm + r]
        return pltpu.make_async_copy(
            h_hbm.at[pl.ds(pl.multiple_of(tok * chunks, chunks), chunks), :],
            xbuf.at[slot, pl.ds(r * chunks, chunks), :], sem.at[slot])

    def wait_gather(slot):
        pltpu.make_async_copy(h_hbm.at[pl.ds(0, tm * chunks), :], xbuf.at[slot], sem.at[slot]).wait()

    @pl.when(i == 0)
    def _():
        for ahead in range(GATHER_AHEAD):
            def body(r, _):
                row_copy(ahead, r, ahead).start()
                return 0
            lax.fori_loop(0, tm, body, 0, unroll=8)

    @pl.when((i < nv) & ((i == 0) | (te_ref[i] != te_ref[jnp.maximum(i - 1, 0)])))
    def _():
        wgu_bf[...] = wgu_ref[0, 0].astype(BF16)
        wd_bf[...] = wd_ref[0, 0].astype(BF16)

    for slot in range(GATHER_BUFS):
        @pl.when((i < nv) & (i % GATHER_BUFS == slot))
        def _():
            wait_gather(slot)
            x = _load_token_tiles(xbuf.at[slot], tm, chunks).astype(BF16)
            gu = jnp.dot(x, wgu_bf[...], preferred_element_type=F32) + bgu_ref[0]
            g = jnp.minimum(gu[:, :d_ff], SWIGLU_LIMIT)
            up = jnp.clip(gu[:, d_ff:], -SWIGLU_LIMIT, SWIGLU_LIMIT)
            act = (up + 1.0) * (g * _sigmoid(SWIGLU_ALPHA * g))
            y = jnp.dot(act.astype(BF16), wd_bf[...], preferred_element_type=F32) + bd_ref[0]
            _store_token_tiles(y_ref, y)
            for r in range(tm):
                row_copy(i + GATHER_AHEAD, r, (slot + GATHER_AHEAD) % GATHER_BUFS).start()

        @pl.when((i >= nv) & (i < nv + GATHER_AHEAD) & (i % GATHER_BUFS == slot))
        def _():
            wait_gather(slot)

    @pl.when(i >= nv)
    def _():
        y_ref[...] = jnp.zeros_like(y_ref)


def _expert_call(tile_expert, n_valid, row_tok, h_tiles, wgu_all, bgu, wd_all, bd, layer, n_tiles, tm):
    d_ff, d = wd_all.shape[2:]
    chunks = d // LANES
    wsel = lambda i, te, nv, tok: (layer, te[i], 0, 0)
    bsel = lambda i, te, nv, tok: (te[i], 0, 0)
    return pl.pallas_call(
        functools.partial(_expert_kernel, tm=tm, d=d, d_ff=d_ff),
        out_shape=jax.ShapeDtypeStruct((n_tiles * tm * chunks, LANES), F32),
        grid_spec=pltpu.PrefetchScalarGridSpec(
            num_scalar_prefetch=3,
            grid=(n_tiles,),
            in_specs=[pl.BlockSpec(memory_space=pl.ANY),
                      pl.BlockSpec((1, 1, d, 2 * d_ff), wsel),
                      pl.BlockSpec((1, 1, 2 * d_ff), bsel),
                      pl.BlockSpec((1, 1, d_ff, d), wsel),
                      pl.BlockSpec((1, 1, d), bsel)],
            out_specs=pl.BlockSpec((tm * chunks, LANES), lambda i, te, nv, tok: (i, 0)),
            scratch_shapes=[pltpu.VMEM((GATHER_BUFS, tm * chunks, LANES), F32),
                            pltpu.VMEM((d, 2 * d_ff), BF16),
                            pltpu.VMEM((d_ff, d), BF16),
                            pltpu.SemaphoreType.DMA((GATHER_BUFS,))]),
        compiler_params=pltpu.CompilerParams(dimension_semantics=("arbitrary",),
                                             vmem_limit_bytes=EXPERT_VMEM_LIMIT),
        name="moe_experts",
    )(tile_expert, n_valid, row_tok, h_tiles, wgu_all, bgu.reshape(N_EXPERTS, 1, -1), wd_all,
      bd.reshape(N_EXPERTS, 1, -1))


def _combine_kernel(pos_ref, ys_hbm, gate_ref, h_ref, lg_ref, lb_ref, o_ref, buf, sem,
                    *, alpha, tm, d):
    i = pl.program_id(0)
    chunks = d // LANES

    n_steps = pl.num_programs(0)

    def row_copy(step, r, k, slot):
        src = pos_ref[step * (tm * TOP_K) + r * TOP_K + k]
        return pltpu.make_async_copy(
            ys_hbm.at[pl.ds(pl.multiple_of(src * chunks, chunks), chunks), :],
            buf.at[slot, k, pl.ds(r * chunks, chunks), :], sem.at[slot])

    def wait_rows(slot):
        for k in range(TOP_K):
            pltpu.make_async_copy(ys_hbm.at[pl.ds(0, tm * chunks), :], buf.at[slot, k],
                                  sem.at[slot]).wait()

    @pl.when(i == 0)
    def _():
        for ahead in range(GATHER_AHEAD):
            def body(r, _):
                for k in range(TOP_K):
                    row_copy(jnp.minimum(ahead, n_steps - 1), r, k, ahead).start()
                return 0
            lax.fori_loop(0, tm, body, 0, unroll=2)

    for slot in range(GATHER_BUFS):
        @pl.when(i % GATHER_BUFS == slot)
        def _():
            wait_rows(slot)
            gate = gate_ref[...]
            ffn = _load_token_tiles(buf.at[slot, 0], tm, chunks) * gate[:, 0:1]
            for k in range(1, TOP_K):
                ffn = ffn + _load_token_tiles(buf.at[slot, k], tm, chunks) * gate[:, k:k + 1]
            o_ref[...] = _layer_norm(alpha * h_ref[...] + ffn, lg_ref[...], lb_ref[...])
            nxt = jnp.minimum(i + GATHER_AHEAD, n_steps - 1)
            for r in range(tm):
                for k in range(TOP_K):
                    row_copy(nxt, r, k, (slot + GATHER_AHEAD) % GATHER_BUFS).start()

        @pl.when((i % GATHER_BUFS == slot) & (i + 1 == n_steps))
        def _():
            for ahead in range(1, GATHER_BUFS):
                wait_rows((slot + ahead) % GATHER_BUFS)


def _combine_call(pos_flat, ys, gates, h2d, ln_g, ln_b, alpha, tm):
    t, d = h2d.shape
    chunks = d // LANES
    row = lambda i, pos: (i, 0)
    const2 = lambda i, pos: (0, 0)
    return pl.pallas_call(
        functools.partial(_combine_kernel, alpha=alpha, tm=tm, d=d),
        out_shape=jax.ShapeDtypeStruct((t, d), F32),
        grid_spec=pltpu.PrefetchScalarGridSpec(
            num_scalar_prefetch=1,
            grid=(t // tm,),
            in_specs=[pl.BlockSpec(memory_space=pl.ANY),
                      pl.BlockSpec((tm, LANES), row),
                      pl.BlockSpec((tm, d), row),
                      pl.BlockSpec((1, d), const2),
                      pl.BlockSpec((1, d), const2)],
            out_specs=pl.BlockSpec((tm, d), row),
            scratch_shapes=[pltpu.VMEM((GATHER_BUFS, TOP_K, tm * chunks, LANES), F32),
                            pltpu.SemaphoreType.DMA((GATHER_BUFS,))]),
        compiler_params=pltpu.CompilerParams(dimension_semantics=("arbitrary",),
                                             vmem_limit_bytes=VMEM_LIMIT),
        name="moe_combine",
    )(pos_flat, ys, gates, h2d, ln_g.reshape(1, d), ln_b.reshape(1, d))


def _in_proj_columns():
    o_cq = 3 * SB_WIDTH
    o_ckv = o_cq + MLA_Q_RANK
    o_kpe = o_ckv + MLA_KV_RANK
    o_ca = o_kpe + MLA_ROPE_DIM
    o_cg = o_ca + CONV_CH
    reps = LANES // 2 // ROPE_HALF
    kpe = np.concatenate([np.tile(o_kpe + np.arange(ROPE_HALF), reps),
                          np.tile(o_kpe + ROPE_HALF + np.arange(ROPE_HALF), reps)])
    return np.concatenate([np.arange(o_cq), o_cq + np.arange(MLA_Q_RANK),
                           o_ckv + np.arange(MLA_KV_RANK), kpe,
                           o_ca + np.arange(CONV_CH), o_cg + np.arange(CONV_CH)])


def _uq_columns():
    per = MLA_NOPE_DIM + MLA_ROPE_DIM
    nope = np.concatenate([h * per + np.arange(MLA_NOPE_DIM) for h in range(MLA_HEADS)])
    rope = []
    grp = LANES // 2 // ROPE_HALF
    for g0 in range(0, MLA_HEADS, grp):
        for half in range(2):
            for h in range(g0, g0 + grp):
                rope.append(h * per + MLA_NOPE_DIM + half * ROPE_HALF + np.arange(ROPE_HALF))
    return np.concatenate([nope] + rope)


def _ukv_columns():
    per = MLA_NOPE_DIM + MLA_V_DIM
    kn = np.concatenate([h * per + np.arange(MLA_NOPE_DIM) for h in range(MLA_HEADS)])
    vv = np.concatenate([h * per + MLA_NOPE_DIM + np.arange(MLA_V_DIM) for h in range(MLA_HEADS)])
    return np.concatenate([kn, vv])


def _rope_tables(seq):
    lp = seq + N_META
    inv = 1.0 / (ROPE_THETA ** (jnp.arange(0, MLA_ROPE_DIM, 2, dtype=F32) / MLA_ROPE_DIM))
    pos = jnp.concatenate([N_META + jnp.arange(seq, dtype=F32), jnp.arange(N_META, dtype=F32)])
    ang = pos[:, None] * inv[None, :]
    reps = LANES // 2 // ROPE_HALF
    cos = jnp.tile(jnp.cos(ang), (1, 2 * reps))
    sin = jnp.tile(jnp.sin(ang), (1, reps))
    assert cos.shape == (lp, LANES)
    return cos, jnp.concatenate([-sin, sin], axis=1)


ATTN_TQ = 512
EXPERT_TM = 256
ROUTE_TM = 128


def kernel(x, meta_tokens, ln_in_g, ln_in_b, w_in, q_norm_g, w_uq, kv_norm_g, w_ukv, conv_w, conv_b,
           conv_ln_g, conv_ln_b, grp_norm_g, w_out, ln_mix_g, ln_mix_b, router_w, router_b,
           w_gate_up, b_gate_up, w_down, b_down, ln_ffn_g, ln_ffn_b):
    b, seq, d = x.shape
    depth = w_in.shape[0]
    lp = seq + N_META
    t = b * lp
    alpha = float((2 * depth) ** 0.25)
    tq = min(ATTN_TQ, seq)
    assert seq % tq == 0 and tq % ATTN_TK == 0 and seq % CONV_ROWS == 0
    tm_tok = _row_tile(t, 512, LANES)
    tm_seq = _row_tile(lp, 1024)
    tm_route = _row_tile(t, ROUTE_TM, 8)
    n_assign = t * TOP_K
    n_tiles = -(-n_assign // EXPERT_TM) + N_EXPERTS + GATHER_AHEAD
    n_rows = n_tiles * EXPERT_TM
    n_chunks = -(-n_assign // INV_CHUNK)

    meta = jnp.broadcast_to(meta_tokens[None].astype(x.dtype), (b, N_META, d))
    xin = jnp.concatenate([x, meta], axis=1).reshape(t, d)
    h = _ln_call(xin, ln_in_g, ln_in_b, tm_tok)
    cos_t, sin_t = _rope_tables(seq)

    in_cols = _in_proj_columns()
    uq_cols, ukv_cols = _uq_columns(), _ukv_columns()

    for l in range(depth):
        w_in_p = w_in[l][:, in_cols].astype(BF16)
        sbqk, sbv, mq, mk, mv, cu = _inproj_call(
            h.reshape(b, lp, d), w_in_p, q_norm_g[l], w_uq[l][:, uq_cols].astype(BF16),
            kv_norm_g[l], w_ukv[l][:, ukv_cols].astype(BF16), cos_t, sin_t, tm_seq)
        sb_out = _sb_call(sbqk, sbv.transpose(0, 2, 1), sbv[:, seq:, :], tq)
        mla_out = _mla_call(mq, mk, mv.transpose(0, 2, 1), mv[:, seq:, :], tq)
        conv_out = _conv_call(cu, conv_w[l], conv_b[l], conv_ln_g[l], conv_ln_b[l])
        h1, h1_tiles, idx, gates, rank, counts = _mix_call(
            sb_out.reshape(t, -1), mla_out.reshape(t, -1), conv_out.reshape(t, -1), h,
            grp_norm_g[l], w_out[l].astype(BF16), ln_mix_g[l], ln_mix_b[l],
            router_w[l].astype(BF16), router_b[l], alpha, tm_tok)

        cnt = counts[0].astype(jnp.int32)
        tiles_e = (cnt + EXPERT_TM - 1) // EXPERT_TM
        tile_end = jnp.cumsum(tiles_e)
        tile_start = tile_end - tiles_e
        pos = tile_start[idx[:, :TOP_K]] * EXPERT_TM + rank[:, :TOP_K]
        pos_flat = pos.reshape(-1).astype(jnp.int32)
        n_valid = tile_end[-1:].astype(jnp.int32)
        tile_ids = jnp.minimum(jnp.arange(n_tiles, dtype=jnp.int32), n_valid[0] - 1)
        tile_expert = jnp.minimum(jnp.sum(tile_end[None, :] <= tile_ids[:, None], axis=1),
                                  N_EXPERTS - 1).astype(jnp.int32)

        pos_chunks = jnp.concatenate(
            [pos_flat, jnp.full((n_chunks * INV_CHUNK - n_assign,), n_rows, jnp.int32)]
        ).reshape(n_chunks, INV_CHUNK)
        row_tok = _invert_call(cnt, tile_start.astype(jnp.int32), pos_chunks, n_rows + INV_CHUNK,
                               EXPERT_TM)
        ys = _expert_call(tile_expert, n_valid, row_tok, h1_tiles, w_gate_up, b_gate_up[l],
                          w_down, b_down[l], l, n_tiles, EXPERT_TM)
        h = _combine_call(pos_flat, ys, gates, h1, ln_ffn_g[l], ln_ffn_b[l], alpha, tm_route)

    return h.reshape(b, lp, d)[:, :seq, :]
```

```python
import functools

import jax
import jax.numpy as jnp
import numpy as np
from jax import lax
from jax.experimental import pallas as pl
from jax.experimental.pallas import tpu as pltpu

F32 = jnp.float32
BF16 = jnp.bfloat16

N_META = 16
SB_HEADS = 4
SB_HEAD_DIM = 64
SB_WIDTH = SB_HEADS * SB_HEAD_DIM
MLA_HEADS = 8
MLA_NOPE_DIM = 64
MLA_ROPE_DIM = 32
MLA_V_DIM = 64
MLA_Q_RANK = 256
MLA_KV_RANK = 128
MLA_WIDTH = MLA_HEADS * MLA_V_DIM
ROPE_THETA = 10000.0
CONV_CH = 256
CONV_K = 31
N_EXPERTS = 32
TOP_K = 4
SWIGLU_LIMIT = 7.0
SWIGLU_ALPHA = 1.702
LN_EPS = 1e-5
RMS_EPS = 1e-6

LANES = 128
HEAD_PAIR = LANES // SB_HEAD_DIM
ROPE_HALF = MLA_ROPE_DIM // 2
VMEM_LIMIT = 48 * 1024 * 1024
EXPERT_VMEM_LIMIT = 56 * 1024 * 1024
NEG_BIG = -1e30
LOG2E = 1.4426950408889634
ATTN_TK = 256

IN_SB = 3 * SB_WIDTH
IN_CQ = IN_SB
IN_CKV = IN_CQ + MLA_Q_RANK
IN_KPE = IN_CKV + MLA_KV_RANK
IN_CA = IN_KPE + LANES
IN_CG = IN_CA + CONV_CH
IN_TOTAL = IN_CG + CONV_CH
MQ_W = MLA_HEADS * MLA_NOPE_DIM + 2 * LANES
MK_W = MLA_HEADS * MLA_NOPE_DIM + LANES


def _row_tile(n, cap, mult=16):
    best = None
    for t in range(mult, min(n, cap) + 1, mult):
        if n % t == 0:
            best = t
    assert best is not None, (n, cap, mult)
    return best


def _layer_norm(x, g, b):
    mu = jnp.mean(x, axis=-1, keepdims=True)
    xc = x - mu
    var = jnp.mean(xc * xc, axis=-1, keepdims=True)
    return xc * lax.rsqrt(var + LN_EPS) * g + b


def _rms_norm(x, g):
    return x * lax.rsqrt(jnp.mean(x * x, axis=-1, keepdims=True) + RMS_EPS) * g


def _sigmoid(x):
    return 1.0 / (1.0 + jnp.exp(-x))


def _ln_kernel(x_ref, g_ref, b_ref, o_ref):
    o_ref[...] = _layer_norm(x_ref[...], g_ref[...], b_ref[...])


def _ln_call(x2d, g, b, tm):
    t, d = x2d.shape
    return pl.pallas_call(
        _ln_kernel,
        out_shape=jax.ShapeDtypeStruct((t, d), F32),
        grid=(t // tm,),
        in_specs=[pl.BlockSpec((tm, d), lambda i: (i, 0)),
                  pl.BlockSpec((1, d), lambda i: (0, 0)),
                  pl.BlockSpec((1, d), lambda i: (0, 0))],
        out_specs=pl.BlockSpec((tm, d), lambda i: (i, 0)),
        compiler_params=pltpu.CompilerParams(dimension_semantics=("parallel",),
                                             vmem_limit_bytes=VMEM_LIMIT),
        name="ln_in",
    )(x2d, g.reshape(1, d), b.reshape(1, d))


def _inproj_kernel(h_ref, w_ref, qg_ref, wuq_ref, kvg_ref, wukv_ref, cos_ref, sin_ref,
                   sbqk_ref, sbv_ref, mq_ref, mk_ref, mv_ref, cu_ref):
    h = h_ref[0].astype(BF16)
    proj = jnp.dot(h, w_ref[...], preferred_element_type=F32)
    sbqk_ref[0, :, :SB_WIDTH] = (proj[:, :SB_WIDTH] * (SB_HEAD_DIM ** -0.5 * LOG2E)).astype(BF16)
    sbqk_ref[0, :, SB_WIDTH:] = proj[:, SB_WIDTH:2 * SB_WIDTH].astype(BF16)
    sbv_ref[0] = proj[:, 2 * SB_WIDTH:IN_SB].astype(BF16)
    cos = cos_ref[...]
    sin = sin_ref[...]

    def rot(x):
        return x * cos + pltpu.roll(x, LANES // 2, 1) * sin

    cq = _rms_norm(proj[:, IN_CQ:IN_CKV], qg_ref[...])
    qm = jnp.dot(cq.astype(BF16), wuq_ref[...], preferred_element_type=F32)
    qm = qm * ((MLA_NOPE_DIM + MLA_ROPE_DIM) ** -0.5 * LOG2E)
    nope_w = MLA_HEADS * MLA_NOPE_DIM
    mq_ref[0, :, :nope_w] = qm[:, :nope_w].astype(BF16)
    mq_ref[0, :, nope_w:nope_w + LANES] = rot(qm[:, nope_w:nope_w + LANES]).astype(BF16)
    mq_ref[0, :, nope_w + LANES:] = rot(qm[:, nope_w + LANES:]).astype(BF16)

    ckv = _rms_norm(proj[:, IN_CKV:IN_KPE], kvg_ref[...])
    kv = jnp.dot(ckv.astype(BF16), wukv_ref[...], preferred_element_type=F32)
    mk_ref[0, :, :nope_w] = kv[:, :nope_w].astype(BF16)
    mk_ref[0, :, nope_w:] = rot(proj[:, IN_KPE:IN_CA]).astype(BF16)
    mv_ref[0] = kv[:, nope_w:].astype(BF16)

    cu_ref[0] = proj[:, IN_CA:IN_CG] * _sigmoid(proj[:, IN_CG:IN_TOTAL])


def _inproj_call(h3, w_in_p, qg, wuq_p, kvg, wukv_p, cos_t, sin_t, tm):
    b, lp, d = h3.shape
    nt = lp // tm
    const2 = lambda bi, ti: (0, 0)
    tok3 = lambda bi, ti: (bi, ti, 0)
    outs = (jax.ShapeDtypeStruct((b, lp, 2 * SB_WIDTH), BF16),
            jax.ShapeDtypeStruct((b, lp, SB_WIDTH), BF16),
            jax.ShapeDtypeStruct((b, lp, MQ_W), BF16),
            jax.ShapeDtypeStruct((b, lp, MK_W), BF16),
            jax.ShapeDtypeStruct((b, lp, MLA_WIDTH), BF16),
            jax.ShapeDtypeStruct((b, lp, CONV_CH), F32))
    return pl.pallas_call(
        _inproj_kernel,
        out_shape=outs,
        grid=(b, nt),
        in_specs=[pl.BlockSpec((1, tm, d), tok3),
                  pl.BlockSpec(w_in_p.shape, const2),
                  pl.BlockSpec((1, MLA_Q_RANK), const2),
                  pl.BlockSpec(wuq_p.shape, const2),
                  pl.BlockSpec((1, MLA_KV_RANK), const2),
                  pl.BlockSpec(wukv_p.shape, const2),
                  pl.BlockSpec((tm, LANES), lambda bi, ti: (ti, 0)),
                  pl.BlockSpec((tm, LANES), lambda bi, ti: (ti, 0))],
        out_specs=tuple(pl.BlockSpec((1, tm, s.shape[2]), tok3) for s in outs),
        compiler_params=pltpu.CompilerParams(dimension_semantics=("parallel", "parallel"),
                                             vmem_limit_bytes=VMEM_LIMIT),
        name="in_proj",
    )(h3, w_in_p, qg.reshape(1, -1), wuq_p, kvg.reshape(1, -1), wukv_p, cos_t, sin_t)


def _log_stay2(z):
    nz = -z
    return jnp.minimum(nz, 0.0) - jnp.log2(1.0 + jnp.exp2(jnp.minimum(z, nz)))


def _split_bf16(x):
    hi = x.astype(BF16)
    return hi, (x - hi.astype(F32)).astype(BF16)


def _sb_kernel(q_ref, k_ref, vt_ref, vmeta_ref, o_ref, acc_ref, c_ref, *, tq, seq):
    i = pl.program_id(1)
    nq = seq // tq
    npair = SB_HEADS // HEAD_PAIR
    lane = lax.broadcasted_iota(jnp.int32, (1, LANES), 1)
    head_sel = (lane < SB_HEAD_DIM, lane >= SB_HEAD_DIM)
    top_rows = lax.broadcasted_iota(jnp.int32, (LANES, 1), 0) < SB_HEAD_DIM
    meta = pl.ds(seq, N_META)

    def iota2(rows, cols):
        return (lax.broadcasted_iota(jnp.int32, (rows, cols), 0),
                lax.broadcasted_iota(jnp.int32, (rows, cols), 1))

    def tri_down(w):
        r, c = iota2(w, w)
        return jnp.where(c >= r, 1.0, 0.0).astype(BF16)

    def tile_all(q, krows, mask, ut, first):
        zs = []
        for p in range(npair):
            cols = slice(p * LANES, (p + 1) * LANES)
            kt = k_ref[0, krows, cols]
            for hd in range(HEAD_PAIR):
                qh = jnp.where(head_sel[hd], q[:, cols], jnp.zeros((1, 1), BF16))
                zs.append(lax.dot_general(kt, qh, (((1,), (1,)), ((), ())),
                                          preferred_element_type=F32))
        splits = []
        for h in range(SB_HEADS):
            ls = _log_stay2(zs[h])
            if mask is not None:
                ls = jnp.where(mask, ls, 0.0)
            splits.append(_split_bf16(ls))
        cums = [jnp.dot(ut, hi, preferred_element_type=F32) + jnp.dot(ut, lo, preferred_element_type=F32)
                for hi, lo in splits]
        ws = []
        for h in range(SB_HEADS):
            lw = zs[h] + cums[h] if first else zs[h] + cums[h] + c_ref[h]
            w = jnp.exp2(lw)
            if mask is not None:
                w = jnp.where(mask, w, 0.0)
            ws.append(w.astype(BF16))
            c_ref[h] = cums[h][0:1, :] if first else c_ref[h] + cums[h][0:1, :]
        for p in range(npair):
            vt = vt_ref[0, p * LANES:(p + 1) * LANES, krows]
            res = [jnp.dot(vt, ws[p * HEAD_PAIR + hd], preferred_element_type=F32)
                   for hd in range(HEAD_PAIR)]
            contrib = jnp.where(top_rows, res[0], res[1])
            acc_ref[p] = contrib if first else acc_ref[p] + contrib

    @pl.when(i < nq)
    def _():
        q = q_ref[0]
        ut = tri_down(ATTN_TK)
        r, c = iota2(ATTN_TK, tq)
        per_q = tq // ATTN_TK
        for d in range(per_q - 1, -1, -1):
            tile_all(q, pl.ds(pl.multiple_of(i * tq + d * ATTN_TK, ATTN_TK), ATTN_TK),
                     r + d * ATTN_TK < c, ut, d == per_q - 1)

        def body(t, carry):
            start = pl.multiple_of((i * per_q - 1 - t) * ATTN_TK, ATTN_TK)
            tile_all(q, pl.ds(start, ATTN_TK), None, ut, False)
            return carry

        lax.fori_loop(0, i * per_q, body, 0)
        tile_all(q, meta, None, tri_down(N_META), False)
        for p in range(npair):
            o_ref[0, :, p * LANES:(p + 1) * LANES] = acc_ref[p].T.astype(o_ref.dtype)

    @pl.when(i == nq)
    def _():
        r, c = iota2(N_META, N_META)
        mask = c < r
        ut = jnp.where(r >= c, 1.0, 0.0).astype(BF16)
        for p in range(npair):
            cols = slice(p * LANES, (p + 1) * LANES)
            q = q_ref[0, 0:N_META, cols]
            kt = k_ref[0, meta, cols]
            vm = vmeta_ref[0, :, cols]
            res = []
            for hd in range(HEAD_PAIR):
                qh = jnp.where(head_sel[hd], q, jnp.zeros((1, 1), BF16))
                z = lax.dot_general(qh, kt, (((1,), (1,)), ((), ())), preferred_element_type=F32)
                hi, lo = _split_bf16(jnp.where(mask, _log_stay2(z), 0.0))
                cum = (jnp.dot(hi, ut, preferred_element_type=F32)
                       + jnp.dot(lo, ut, preferred_element_type=F32))
                w = jnp.where(mask, jnp.exp2(z + cum), 0.0)
                res.append(jnp.dot(w.astype(BF16), vm, preferred_element_type=F32))
            o_ref[0, 0:N_META, cols] = jnp.where(head_sel[0], res[0], res[1]).astype(o_ref.dtype)


def _sb_call(sbqk, sbv_t, sbv_meta, tq):
    b, lp, _ = sbqk.shape
    seq = lp - N_META
    nq = seq // tq
    return pl.pallas_call(
        functools.partial(_sb_kernel, tq=tq, seq=seq),
        out_shape=jax.ShapeDtypeStruct((b, lp, SB_WIDTH), BF16),
        grid=(b, nq + 1),
        in_specs=[pl.BlockSpec((1, tq, SB_WIDTH), lambda bi, i: (bi, i, 0)),
                  pl.BlockSpec((1, lp, SB_WIDTH), lambda bi, i: (bi, 0, 1)),
                  pl.BlockSpec((1, SB_WIDTH, lp), lambda bi, i: (bi, 0, 0)),
                  pl.BlockSpec((1, N_META, SB_WIDTH), lambda bi, i: (bi, 0, 0))],
        out_specs=pl.BlockSpec((1, tq, SB_WIDTH), lambda bi, i: (bi, i, 0)),
        scratch_shapes=[pltpu.VMEM((SB_HEADS // HEAD_PAIR, LANES, tq), F32),
                        pltpu.VMEM((SB_HEADS, 1, tq), F32)],
        compiler_params=pltpu.CompilerParams(
            dimension_semantics=("parallel", "arbitrary"),
            vmem_limit_bytes=VMEM_LIMIT),
        name="sb_attn",
    )(sbqk, sbqk, sbv_t, sbv_meta)


def _mla_kernel(q_ref, k_ref, vt_ref, vmeta_ref, o_ref, acc_ref, m_ref, *, tq, seq):
    i = pl.program_id(1)
    nq = seq // tq
    npair = MLA_HEADS // HEAD_PAIR
    nope_w = MLA_HEADS * MLA_NOPE_DIM
    grp = LANES // 2 // ROPE_HALF
    lane = lax.broadcasted_iota(jnp.int32, (1, LANES), 1)
    lane2 = lax.broadcasted_iota(jnp.int32, (1, 2 * LANES), 1)
    top_rows = lax.broadcasted_iota(jnp.int32, (LANES, 1), 0) < MLA_V_DIM
    meta = pl.ds(seq, N_META)
    one = jnp.ones((1, 1), BF16)

    def head_lanes(h):
        n0 = (h % HEAD_PAIR) * MLA_NOPE_DIM
        r0 = LANES + (h % grp) * ROPE_HALF
        r1 = r0 + LANES // 2
        return (((lane2 >= n0) & (lane2 < n0 + MLA_NOPE_DIM))
                | ((lane2 >= r0) & (lane2 < r0 + ROPE_HALF))
                | ((lane2 >= r1) & (lane2 < r1 + ROPE_HALF)))

    def qcat(q, p):
        rope0 = nope_w + (p * HEAD_PAIR // grp) * LANES
        return jnp.concatenate([q[:, p * LANES:(p + 1) * LANES], q[:, rope0:rope0 + LANES]], axis=-1)

    def kcat(krows, p):
        return jnp.concatenate([k_ref[0, krows, p * LANES:(p + 1) * LANES],
                                k_ref[0, krows, nope_w:nope_w + LANES]], axis=-1)

    def iota2(rows, cols):
        return (lax.broadcasted_iota(jnp.int32, (rows, cols), 0),
                lax.broadcasted_iota(jnp.int32, (rows, cols), 1))

    def tile_all(q, krows, mask, first):
        scores = []
        for p in range(npair):
            kc = kcat(krows, p)
            qc = qcat(q, p)
            for hd in range(HEAD_PAIR):
                qh = jnp.where(head_lanes(p * HEAD_PAIR + hd), qc, jnp.zeros((1, 1), BF16))
                scores.append(lax.dot_general(kc, qh, (((1,), (1,)), ((), ())),
                                              preferred_element_type=F32))
        pexps, alphas = [], []
        for h in range(MLA_HEADS):
            s = scores[h] if mask is None else jnp.where(mask, scores[h], NEG_BIG)
            mx = jnp.max(s, axis=0, keepdims=True)
            if first:
                m_new = mx
                alphas.append(None)
            else:
                m_old = m_ref[h]
                m_new = jnp.maximum(m_old, mx)
                alphas.append(jnp.exp2(m_old - m_new))
            m_ref[h] = m_new
            pexps.append(jnp.exp2(s - m_new).astype(BF16))
        for h in range(MLA_HEADS):
            p, hd = divmod(h, HEAD_PAIR)
            vt = vt_ref[0, p * LANES:(p + 1) * LANES, krows]
            vte = jnp.where(top_rows, vt, one) if hd == 0 else jnp.where(top_rows, one, vt)
            pv = jnp.dot(vte, pexps[h], preferred_element_type=F32)
            acc_ref[h] = pv if first else acc_ref[h] * alphas[h] + pv

    @pl.when(i < nq)
    def _():
        q = q_ref[0]
        r, c = iota2(ATTN_TK, tq)
        per_q = tq // ATTN_TK
        for d in range(per_q):
            tile_all(q, pl.ds(pl.multiple_of(i * tq + d * ATTN_TK, ATTN_TK), ATTN_TK),
                     r + d * ATTN_TK <= c, d == 0)

        def body(t, carry):
            tile_all(q, pl.ds(pl.multiple_of(t * ATTN_TK, ATTN_TK), ATTN_TK), None, False)
            return carry

        lax.fori_loop(0, i * per_q, body, 0)
        tile_all(q, meta, None, False)
        for p in range(npair):
            a = acc_ref[p * HEAD_PAIR]
            b = acc_ref[p * HEAD_PAIR + 1]
            out_t = jnp.where(top_rows, a * (1.0 / a[MLA_V_DIM:MLA_V_DIM + 1, :]), b * (1.0 / b[0:1, :]))
            o_ref[0, :, p * LANES:(p + 1) * LANES] = out_t.T.astype(o_ref.dtype)

    @pl.when(i == nq)
    def _():
        q = q_ref[0, 0:N_META, :]
        r, c = iota2(N_META, N_META)
        for p in range(npair):
            kc = kcat(meta, p)
            qc = qcat(q, p)
            vm = vmeta_ref[0, :, p * LANES:(p + 1) * LANES]
            res = []
            for hd in range(HEAD_PAIR):
                qh = jnp.where(head_lanes(p * HEAD_PAIR + hd), qc, jnp.zeros((1, 1), BF16))
                s = lax.dot_general(qh, kc, (((1,), (1,)), ((), ())), preferred_element_type=F32)
                s = jnp.where(c <= r, s, NEG_BIG)
                pexp = jnp.exp2(s - jnp.max(s, axis=1, keepdims=True))
                pv = jnp.dot(pexp.astype(BF16), vm, preferred_element_type=F32)
                res.append(pv * (1.0 / jnp.sum(pexp, axis=1, keepdims=True)))
            o_ref[0, 0:N_META, p * LANES:(p + 1) * LANES] = jnp.where(
                lane < MLA_V_DIM, res[0], res[1]).astype(o_ref.dtype)


def _mla_call(mq, mk, mv_t, mv_meta, tq):
    b, lp, _ = mq.shape
    seq = lp - N_META
    nq = seq // tq
    return pl.pallas_call(
        functools.partial(_mla_kernel, tq=tq, seq=seq),
        out_shape=jax.ShapeDtypeStruct((b, lp, MLA_WIDTH), BF16),
        grid=(b, nq + 1),
        in_specs=[pl.BlockSpec((1, tq, MQ_W), lambda bi, i: (bi, i, 0)),
                  pl.BlockSpec((1, lp, MK_W), lambda bi, i: (bi, 0, 0)),
                  pl.BlockSpec((1, MLA_WIDTH, lp), lambda bi, i: (bi, 0, 0)),
                  pl.BlockSpec((1, N_META, MLA_WIDTH), lambda bi, i: (bi, 0, 0))],
        out_specs=pl.BlockSpec((1, tq, MLA_WIDTH), lambda bi, i: (bi, i, 0)),
        scratch_shapes=[pltpu.VMEM((MLA_HEADS, LANES, tq), F32),
                        pltpu.VMEM((MLA_HEADS, 1, tq), F32)],
        compiler_params=pltpu.CompilerParams(
            dimension_semantics=("parallel", "arbitrary"),
            vmem_limit_bytes=VMEM_LIMIT),
        name="mla_attn",
    )(mq, mk, mv_t, mv_meta)


CONV_PAD = 32
CONV_ROWS = 128
SUBLANES = 8
CONV_WIN_EXTRA = CONV_PAD


def _conv_kernel(u_ref, w_ref, b_ref, g_ref, beta_ref, o_ref, buf_ref, *, seq):
    buf_ref[0:CONV_PAD, :] = jnp.zeros((CONV_PAD, CONV_CH), F32)
    buf_ref[CONV_PAD:CONV_PAD + N_META, :] = u_ref[0, seq:seq + N_META, :]
    buf_ref[CONV_PAD + N_META:CONV_PAD + N_META + seq, :] = u_ref[0, 0:seq, :]
    w = w_ref[...]
    lead = CONV_PAD - (CONV_K - 1)

    def finish(acc):
        y = _layer_norm(acc + b_ref[...], g_ref[...], beta_ref[...])
        return (y * _sigmoid(y)).astype(o_ref.dtype)

    def conv_rows(first_pos, rows):
        win = buf_ref[pl.ds(first_pos, rows + CONV_WIN_EXTRA), :]
        acc = jnp.zeros((rows, CONV_CH), F32)
        for sh in range(SUBLANES):
            offs = [o for o in range(lead, lead + CONV_K) if o % SUBLANES == sh]
            shifted = win[sh:sh + rows + offs[-1] - sh, :]
            for o in offs:
                k = o - lead
                acc = acc + shifted[o - sh:o - sh + rows, :] * w[k:k + 1, :]
        return finish(acc)

    def body(c, _):
        r0 = pl.multiple_of(c * CONV_ROWS, CONV_ROWS)
        o_ref[0, pl.ds(r0, CONV_ROWS), :] = conv_rows(pl.multiple_of(r0 + N_META, SUBLANES), CONV_ROWS)
        return 0

    lax.fori_loop(0, seq // CONV_ROWS, body, 0)
    o_ref[0, seq:seq + N_META, :] = conv_rows(0, N_META)


def _conv_call(cu, conv_w, conv_b, ln_g, ln_b):
    b, lp, _ = cu.shape
    seq = lp - N_META
    const2 = lambda bi: (0, 0)
    return pl.pallas_call(
        functools.partial(_conv_kernel, seq=seq),
        out_shape=jax.ShapeDtypeStruct((b, lp, CONV_CH), BF16),
        grid=(b,),
        in_specs=[pl.BlockSpec((1, lp, CONV_CH), lambda bi: (bi, 0, 0)),
                  pl.BlockSpec((CONV_K, CONV_CH), const2),
                  pl.BlockSpec((1, CONV_CH), const2),
                  pl.BlockSpec((1, CONV_CH), const2),
                  pl.BlockSpec((1, CONV_CH), const2)],
        out_specs=pl.BlockSpec((1, lp, CONV_CH), lambda bi: (bi, 0, 0)),
        scratch_shapes=[pltpu.VMEM((CONV_PAD + lp, CONV_CH), F32)],
        compiler_params=pltpu.CompilerParams(dimension_semantics=("parallel",),
                                             vmem_limit_bytes=VMEM_LIMIT),
        name="conv",
    )(cu, conv_w, conv_b.reshape(1, -1), ln_g.reshape(1, -1), ln_b.reshape(1, -1))


def _mix_kernel(sb_ref, mla_ref, cv_ref, h_ref, gg_ref, wo_ref, lg_ref, lb_ref, rw_ref, rb_ref,
                h1_ref, h1t_ref, idx_ref, gate_ref, rank_ref, cnt_ref, carry_ref, *, alpha, tm):
    step = pl.program_id(0)

    @pl.when(step == 0)
    def _():
        carry_ref[...] = jnp.zeros_like(carry_ref)

    gg = gg_ref[...]
    y = jnp.concatenate(
        [_rms_norm(sb_ref[...].astype(F32), gg[:, :SB_WIDTH]),
         _rms_norm(mla_ref[...].astype(F32), gg[:, SB_WIDTH:SB_WIDTH + MLA_WIDTH]),
         _rms_norm(cv_ref[...].astype(F32), gg[:, SB_WIDTH + MLA_WIDTH:])], axis=-1)
    mix = jnp.dot(y.astype(BF16), wo_ref[...], preferred_element_type=F32)
    h1 = _layer_norm(alpha * h_ref[...] + mix, lg_ref[...], lb_ref[...])
    h1_ref[...] = h1
    _store_token_tiles(h1t_ref, h1)

    logits = jnp.dot(h1.astype(BF16), rw_ref[...], preferred_element_type=F32) + rb_ref[...]
    eiota = lax.broadcasted_iota(jnp.int32, (tm, N_EXPERTS), 1)
    lane = lax.broadcasted_iota(jnp.int32, (tm, LANES), 1)
    vals = logits
    sels, tops, idxs = [], [], []
    for _ in range(TOP_K):
        m = jnp.max(vals, axis=1, keepdims=True)
        idx = jnp.min(jnp.where(vals == m, eiota, N_EXPERTS), axis=1, keepdims=True)
        sel = eiota == idx
        vals = jnp.where(sel, -jnp.inf, vals)
        sels.append(sel)
        tops.append(m)
        idxs.append(idx)
    exps = [jnp.exp(t - tops[0]) for t in tops]
    denom = exps[0] + exps[1] + exps[2] + exps[3]

    chosen = jnp.zeros((tm, N_EXPERTS), F32)
    for sel in sels:
        chosen = chosen + jnp.where(sel, 1.0, 0.0)
    r = lax.broadcasted_iota(jnp.int32, (tm, tm), 0)
    c = lax.broadcasted_iota(jnp.int32, (tm, tm), 1)
    below = jnp.where(c < r, 1.0, 0.0).astype(BF16)
    earlier = jnp.dot(below, chosen.astype(BF16), preferred_element_type=F32) + carry_ref[...]

    idx_out = jnp.zeros((tm, LANES), jnp.int32)
    gate_out = jnp.zeros((tm, LANES), F32)
    rank_out = jnp.zeros((tm, LANES), jnp.int32)
    for k in range(TOP_K):
        rank_k = jnp.sum(jnp.where(sels[k], earlier, 0.0), axis=1, keepdims=True)
        idx_out = jnp.where(lane == k, idxs[k], idx_out)
        gate_out = jnp.where(lane == k, exps[k] / denom, gate_out)
        rank_out = jnp.where(lane == k, rank_k.astype(jnp.int32), rank_out)
    idx_ref[...] = idx_out
    gate_ref[...] = gate_out
    rank_ref[...] = rank_out

    carry_ref[...] = carry_ref[...] + jnp.sum(chosen, axis=0, keepdims=True)
    cnt_ref[...] = carry_ref[...]


def _mix_call(sb_out, mla_out, conv_out, h2d, grp_g, w_out_b, ln_g, ln_b, router_w_b, router_b,
              alpha, tm):
    t, d = h2d.shape
    row = lambda i: (i, 0)
    const2 = lambda i: (0, 0)
    chunks = d // LANES
    outs = (jax.ShapeDtypeStruct((t, d), F32),
            jax.ShapeDtypeStruct((t * chunks, LANES), F32),
            jax.ShapeDtypeStruct((t, LANES), jnp.int32),
            jax.ShapeDtypeStruct((t, LANES), F32),
            jax.ShapeDtypeStruct((t, LANES), jnp.int32),
            jax.ShapeDtypeStruct((1, N_EXPERTS), F32))
    return pl.pallas_call(
        functools.partial(_mix_kernel, alpha=alpha, tm=tm),
        out_shape=outs,
        grid=(t // tm,),
        in_specs=[pl.BlockSpec((tm, SB_WIDTH), row),
                  pl.BlockSpec((tm, MLA_WIDTH), row),
                  pl.BlockSpec((tm, CONV_CH), row),
                  pl.BlockSpec((tm, d), row),
                  pl.BlockSpec((1, d), const2),
                  pl.BlockSpec((d, d), const2),
                  pl.BlockSpec((1, d), const2),
                  pl.BlockSpec((1, d), const2),
                  pl.BlockSpec((d, N_EXPERTS), const2),
                  pl.BlockSpec((1, N_EXPERTS), const2)],
        out_specs=(pl.BlockSpec((tm, d), row),
                   pl.BlockSpec((tm * chunks, LANES), row),
                   pl.BlockSpec((tm, LANES), row),
                   pl.BlockSpec((tm, LANES), row),
                   pl.BlockSpec((tm, LANES), row),
                   pl.BlockSpec((1, N_EXPERTS), const2)),
        scratch_shapes=[pltpu.VMEM((1, N_EXPERTS), F32)],
        compiler_params=pltpu.CompilerParams(dimension_semantics=("arbitrary",),
                                             vmem_limit_bytes=VMEM_LIMIT),
        name="mix_router",
    )(sb_out, mla_out, conv_out, h2d, grp_g.reshape(1, d), w_out_b, ln_g.reshape(1, d),
      ln_b.reshape(1, d), router_w_b, router_b.reshape(1, N_EXPERTS))


GATHER_AHEAD = 2
GATHER_BUFS = GATHER_AHEAD + 1


def _store_token_tiles(ref, val):
    tm, d = val.shape
    chunks = d // LANES
    for c in range(chunks):
        ref[pl.ds(c, tm, stride=chunks), :] = val[:, c * LANES:(c + 1) * LANES]


def _load_token_tiles(ref, tm, chunks):
    return jnp.concatenate([ref[pl.ds(c, tm, stride=chunks), :] for c in range(chunks)], axis=-1)


INV_CHUNK = 1024


def _invert_kernel(cnt_ref, start_ref, pos_hbm, tok_ref, chunk_ref, sem, *, n_chunks, n_out, tm):
    def fill(lo, hi):
        def body(r, _):
            tok_ref[r] = 0
            return 0
        lax.fori_loop(lo, hi, body, 0)

    def pad_rows(e, used_end):
        tiles_e = (cnt_ref[e] + (tm - 1)) // tm
        fill(start_ref[e] * tm + cnt_ref[e], (start_ref[e] + tiles_e) * tm)
        return jnp.maximum(used_end, (start_ref[e] + tiles_e) * tm)

    used_end = lax.fori_loop(0, N_EXPERTS, pad_rows, 0)
    fill(used_end, n_out)

    def chunk_copy(ch, slot):
        return pltpu.make_async_copy(pos_hbm.at[ch], chunk_ref.at[pl.ds(slot * INV_CHUNK, INV_CHUNK)],
                                     sem.at[slot])

    chunk_copy(0, 0).start()
    toks = INV_CHUNK // TOP_K

    def chunk(ch, _):
        slot = ch % 2
        chunk_copy(ch, slot).wait()

        @pl.when(ch + 1 < n_chunks)
        def _():
            chunk_copy(ch + 1, 1 - slot).start()

        first = slot * INV_CHUNK

        def body(tl, _):
            for k in range(TOP_K):
                tok_ref[chunk_ref[first + tl * TOP_K + k]] = ch * toks + tl
            return 0

        lax.fori_loop(0, toks, body, 0, unroll=16)
        return 0

    lax.fori_loop(0, n_chunks, chunk, 0)


def _invert_call(cnt, tile_start, pos_chunks, n_out, tm):
    n_chunks = pos_chunks.shape[0]
    return pl.pallas_call(
        functools.partial(_invert_kernel, n_chunks=n_chunks, n_out=n_out, tm=tm),
        out_shape=jax.ShapeDtypeStruct((n_out,), jnp.int32),
        grid_spec=pltpu.PrefetchScalarGridSpec(
            num_scalar_prefetch=2,
            grid=(1,),
            in_specs=[pl.BlockSpec(memory_space=pl.ANY)],
            out_specs=pl.BlockSpec(memory_space=pltpu.SMEM),
            scratch_shapes=[pltpu.SMEM((2 * INV_CHUNK,), jnp.int32),
                            pltpu.SemaphoreType.DMA((2,))]),
        compiler_params=pltpu.CompilerParams(dimension_semantics=("arbitrary",)),
        name="moe_row_table",
    )(cnt, tile_start, pos_chunks)


def _expert_kernel(te_ref, nxt_ref, par_ref, nv_ref, tok_ref, h_hbm, wgu_hbm, bgu_ref, wd_hbm,
                   bd_ref, y_ref, xbuf, wgu_f32, wd_f32, wgu_bf, wd_bf, sem, wsem,
                   *, layer, tm, d, d_ff):
    i = pl.program_id(0)
    nv = nv_ref[0]
    chunks = d // LANES

    def weight_copies(e, wslot):
        return (pltpu.make_async_copy(wgu_hbm.at[layer, e], wgu_f32.at[wslot], wsem.at[0, wslot]),
                pltpu.make_async_copy(wd_hbm.at[layer, e], wd_f32.at[wslot], wsem.at[1, wslot]))

    def row_copy(tile, r, slot):
        tok = tok_ref[tile * tm + r]
        return pltpu.make_async_copy(
            h_hbm.at[pl.ds(pl.multiple_of(tok * chunks, chunks), chunks), :],
            xbuf.at[slot, pl.ds(r * chunks, chunks), :], sem.at[slot])

    def wait_gather(slot):
        pltpu.make_async_copy(h_hbm.at[pl.ds(0, tm * chunks), :], xbuf.at[slot], sem.at[slot]).wait()

    @pl.when(i == 0)
    def _():
        for ahead in range(GATHER_AHEAD):
            def body(r, _):
                row_copy(ahead, r, ahead).start()
                return 0
            lax.fori_loop(0, tm, body, 0, unroll=8)
        for cp in weight_copies(te_ref[0], par_ref[0]):
            cp.start()

    @pl.when((i < nv) & ((i == 0) | (te_ref[i] != te_ref[jnp.maximum(i - 1, 0)])))
    def _():
        wslot = par_ref[i]
        for cp in weight_copies(te_ref[i], wslot):
            cp.wait()
        wgu_bf[...] = wgu_f32[wslot].astype(BF16)
        wd_bf[...] = wd_f32[wslot].astype(BF16)

        @pl.when(nxt_ref[i] >= 0)
        def _():
            for cp in weight_copies(nxt_ref[i], 1 - wslot):
                cp.start()

    for slot in range(GATHER_BUFS):
        @pl.when((i < nv) & (i % GATHER_BUFS == slot))
        def _():
            wait_gather(slot)
            x = _load_token_tiles(xbuf.at[slot], tm, chunks).astype(BF16)
            gu = jnp.dot(x, wgu_bf[...], preferred_element_type=F32) + bgu_ref[0]
            g = jnp.minimum(gu[:, :d_ff], SWIGLU_LIMIT)
            up = jnp.clip(gu[:, d_ff:], -SWIGLU_LIMIT, SWIGLU_LIMIT)
            act = (up + 1.0) * (g * _sigmoid(SWIGLU_ALPHA * g))
            y = jnp.dot(act.astype(BF16), wd_bf[...], preferred_element_type=F32) + bd_ref[0]
            _store_token_tiles(y_ref, y)
            for r in range(tm):
                row_copy(i + GATHER_AHEAD, r, (slot + GATHER_AHEAD) % GATHER_BUFS).start()

        @pl.when((i >= nv) & (i < nv + GATHER_AHEAD) & (i % GATHER_BUFS == slot))
        def _():
            wait_gather(slot)

    @pl.when(i >= nv)
    def _():
        y_ref[...] = jnp.zeros_like(y_ref)


def _expert_call(tile_expert, tile_next, tile_parity, n_valid, row_tok, h_tiles, wgu_all, bgu,
                 wd_all, bd, layer, n_tiles, tm):
    d_ff, d = wd_all.shape[2:]
    chunks = d // LANES
    bsel = lambda i, te, nxt, par, nv, tok: (te[i], 0, 0)
    return pl.pallas_call(
        functools.partial(_expert_kernel, layer=layer, tm=tm, d=d, d_ff=d_ff),
        out_shape=jax.ShapeDtypeStruct((n_tiles * tm * chunks, LANES), F32),
        grid_spec=pltpu.PrefetchScalarGridSpec(
            num_scalar_prefetch=5,
            grid=(n_tiles,),
            in_specs=[pl.BlockSpec(memory_space=pl.ANY),
                      pl.BlockSpec(memory_space=pl.ANY),
                      pl.BlockSpec((1, 1, 2 * d_ff), bsel),
                      pl.BlockSpec(memory_space=pl.ANY),
                      pl.BlockSpec((1, 1, d), bsel)],
            out_specs=pl.BlockSpec((tm * chunks, LANES), lambda i, te, nxt, par, nv, tok: (i, 0)),
            scratch_shapes=[pltpu.VMEM((GATHER_BUFS, tm * chunks, LANES), F32),
                            pltpu.VMEM((2, d, 2 * d_ff), F32),
                            pltpu.VMEM((2, d_ff, d), F32),
                            pltpu.VMEM((d, 2 * d_ff), BF16),
                            pltpu.VMEM((d_ff, d), BF16),
                            pltpu.SemaphoreType.DMA((GATHER_BUFS,)),
                            pltpu.SemaphoreType.DMA((2, 2))]),
        compiler_params=pltpu.CompilerParams(dimension_semantics=("arbitrary",),
                                             vmem_limit_bytes=EXPERT_VMEM_LIMIT),
        name="moe_experts",
    )(tile_expert, tile_next, tile_parity, n_valid, row_tok, h_tiles, wgu_all,
      bgu.reshape(N_EXPERTS, 1, -1), wd_all, bd.reshape(N_EXPERTS, 1, -1))


def _combine_kernel(pos_ref, ys_hbm, gate_ref, h_ref, lg_ref, lb_ref, o_ref, buf, sem,
                    *, alpha, tm, d):
    i = pl.program_id(0)
    chunks = d // LANES

    n_steps = pl.num_programs(0)

    def row_copy(step, r, k, slot):
        src = pos_ref[step * (tm * TOP_K) + r * TOP_K + k]
        return pltpu.make_async_copy(
            ys_hbm.at[pl.ds(pl.multiple_of(src * chunks, chunks), chunks), :],
            buf.at[slot, k, pl.ds(r * chunks, chunks), :], sem.at[slot])

    def wait_rows(slot):
        for k in range(TOP_K):
            pltpu.make_async_copy(ys_hbm.at[pl.ds(0, tm * chunks), :], buf.at[slot, k],
                                  sem.at[slot]).wait()

    @pl.when(i == 0)
    def _():
        for ahead in range(GATHER_AHEAD):
            def body(r, _):
                for k in range(TOP_K):
                    row_copy(jnp.minimum(ahead, n_steps - 1), r, k, ahead).start()
                return 0
            lax.fori_loop(0, tm, body, 0, unroll=2)

    for slot in range(GATHER_BUFS):
        @pl.when(i % GATHER_BUFS == slot)
        def _():
            wait_rows(slot)
            gate = gate_ref[...]
            ffn = _load_token_tiles(buf.at[slot, 0], tm, chunks) * gate[:, 0:1]
            for k in range(1, TOP_K):
                ffn = ffn + _load_token_tiles(buf.at[slot, k], tm, chunks) * gate[:, k:k + 1]
            o_ref[...] = _layer_norm(alpha * h_ref[...] + ffn, lg_ref[...], lb_ref[...])
            nxt = jnp.minimum(i + GATHER_AHEAD, n_steps - 1)
            for r in range(tm):
                for k in range(TOP_K):
                    row_copy(nxt, r, k, (slot + GATHER_AHEAD) % GATHER_BUFS).start()

        @pl.when((i % GATHER_BUFS == slot) & (i + 1 == n_steps))
        def _():
            for ahead in range(1, GATHER_BUFS):
                wait_rows((slot + ahead) % GATHER_BUFS)


def _combine_call(pos_flat, ys, gates, h2d, ln_g, ln_b, alpha, tm):
    t, d = h2d.shape
    chunks = d // LANES
    row = lambda i, pos: (i, 0)
    const2 = lambda i, pos: (0, 0)
    return pl.pallas_call(
        functools.partial(_combine_kernel, alpha=alpha, tm=tm, d=d),
        out_shape=jax.ShapeDtypeStruct((t, d), F32),
        grid_spec=pltpu.PrefetchScalarGridSpec(
            num_scalar_prefetch=1,
            grid=(t // tm,),
            in_specs=[pl.BlockSpec(memory_space=pl.ANY),
                      pl.BlockSpec((tm, LANES), row),
                      pl.BlockSpec((tm, d), row),
                      pl.BlockSpec((1, d), const2),
                      pl.BlockSpec((1, d), const2)],
            out_specs=pl.BlockSpec((tm, d), row),
            scratch_shapes=[pltpu.VMEM((GATHER_BUFS, TOP_K, tm * chunks, LANES), F32),
                            pltpu.SemaphoreType.DMA((GATHER_BUFS,))]),
        compiler_params=pltpu.CompilerParams(dimension_semantics=("arbitrary",),
                                             vmem_limit_bytes=VMEM_LIMIT),
        name="moe_combine",
    )(pos_flat, ys, gates, h2d, ln_g.reshape(1, d), ln_b.reshape(1, d))


def _in_proj_columns():
    o_cq = 3 * SB_WIDTH
    o_ckv = o_cq + MLA_Q_RANK
    o_kpe = o_ckv + MLA_KV_RANK
    o_ca = o_kpe + MLA_ROPE_DIM
    o_cg = o_ca + CONV_CH
    reps = LANES // 2 // ROPE_HALF
    kpe = np.concatenate([np.tile(o_kpe + np.arange(ROPE_HALF), reps),
                          np.tile(o_kpe + ROPE_HALF + np.arange(ROPE_HALF), reps)])
    return np.concatenate([np.arange(o_cq), o_cq + np.arange(MLA_Q_RANK),
                           o_ckv + np.arange(MLA_KV_RANK), kpe,
                           o_ca + np.arange(CONV_CH), o_cg + np.arange(CONV_CH)])


def _uq_columns():
    per = MLA_NOPE_DIM + MLA_ROPE_DIM
    nope = np.concatenate([h * per + np.arange(MLA_NOPE_DIM) for h in range(MLA_HEADS)])
    rope = []
    grp = LANES // 2 // ROPE_HALF
    for g0 in range(0, MLA_HEADS, grp):
        for half in range(2):
            for h in range(g0, g0 + grp):
                rope.append(h * per + MLA_NOPE_DIM + half * ROPE_HALF + np.arange(ROPE_HALF))
    return np.concatenate([nope] + rope)


def _ukv_columns():
    per = MLA_NOPE_DIM + MLA_V_DIM
    kn = np.concatenate([h * per + np.arange(MLA_NOPE_DIM) for h in range(MLA_HEADS)])
    vv = np.concatenate([h * per + MLA_NOPE_DIM + np.arange(MLA_V_DIM) for h in range(MLA_HEADS)])
    return np.concatenate([kn, vv])


def _rope_tables(seq):
    lp = seq + N_META
    inv = 1.0 / (ROPE_THETA ** (jnp.arange(0, MLA_ROPE_DIM, 2, dtype=F32) / MLA_ROPE_DIM))
    pos = jnp.concatenate([N_META + jnp.arange(seq, dtype=F32), jnp.arange(N_META, dtype=F32)])
    ang = pos[:, None] * inv[None, :]
    reps = LANES // 2 // ROPE_HALF
    cos = jnp.tile(jnp.cos(ang), (1, 2 * reps))
    sin = jnp.tile(jnp.sin(ang), (1, reps))
    assert cos.shape == (lp, LANES)
    return cos, jnp.concatenate([-sin, sin], axis=1)


ATTN_TQ = 512
EXPERT_TM = 256
ROUTE_TM = 128


def kernel(x, meta_tokens, ln_in_g, ln_in_b, w_in, q_norm_g, w_uq, kv_norm_g, w_ukv, conv_w, conv_b,
           conv_ln_g, conv_ln_b, grp_norm_g, w_out, ln_mix_g, ln_mix_b, router_w, router_b,
           w_gate_up, b_gate_up, w_down, b_down, ln_ffn_g, ln_ffn_b):
    b, seq, d = x.shape
    depth = w_in.shape[0]
    lp = seq + N_META
    t = b * lp
    alpha = float((2 * depth) ** 0.25)
    tq = min(ATTN_TQ, seq)
    assert seq % tq == 0 and tq % ATTN_TK == 0 and seq % CONV_ROWS == 0
    tm_tok = _row_tile(t, 512, LANES)
    tm_seq = _row_tile(lp, 1024)
    tm_route = _row_tile(t, ROUTE_TM, 8)
    n_assign = t * TOP_K
    n_tiles = -(-n_assign // EXPERT_TM) + N_EXPERTS + GATHER_AHEAD
    n_rows = n_tiles * EXPERT_TM
    n_chunks = -(-n_assign // INV_CHUNK)

    meta = jnp.broadcast_to(meta_tokens[None].astype(x.dtype), (b, N_META, d))
    xin = jnp.concatenate([x, meta], axis=1).reshape(t, d)
    h = _ln_call(xin, ln_in_g, ln_in_b, tm_tok)
    cos_t, sin_t = _rope_tables(seq)

    in_cols = _in_proj_columns()
    uq_cols, ukv_cols = _uq_columns(), _ukv_columns()

    for l in range(depth):
        w_in_p = w_in[l][:, in_cols].astype(BF16)
        sbqk, sbv, mq, mk, mv, cu = _inproj_call(
            h.reshape(b, lp, d), w_in_p, q_norm_g[l], w_uq[l][:, uq_cols].astype(BF16),
            kv_norm_g[l], w_ukv[l][:, ukv_cols].astype(BF16), cos_t, sin_t, tm_seq)
        sb_out = _sb_call(sbqk, sbv.transpose(0, 2, 1), sbv[:, seq:, :], tq)
        mla_out = _mla_call(mq, mk, mv.transpose(0, 2, 1), mv[:, seq:, :], tq)
        conv_out = _conv_call(cu, conv_w[l], conv_b[l], conv_ln_g[l], conv_ln_b[l])
        h1, h1_tiles, idx, gates, rank, counts = _mix_call(
            sb_out.reshape(t, -1), mla_out.reshape(t, -1), conv_out.reshape(t, -1), h,
            grp_norm_g[l], w_out[l].astype(BF16), ln_mix_g[l], ln_mix_b[l],
            router_w[l].astype(BF16), router_b[l], alpha, tm_tok)

        cnt = counts[0].astype(jnp.int32)
        tiles_e = (cnt + EXPERT_TM - 1) // EXPERT_TM
        tile_end = jnp.cumsum(tiles_e)
        tile_start = tile_end - tiles_e
        pos = tile_start[idx[:, :TOP_K]] * EXPERT_TM + rank[:, :TOP_K]
        pos_flat = pos.reshape(-1).astype(jnp.int32)
        n_valid = tile_end[-1:].astype(jnp.int32)
        tile_ids = jnp.minimum(jnp.arange(n_tiles, dtype=jnp.int32), n_valid[0] - 1)
        tile_expert = jnp.minimum(jnp.sum(tile_end[None, :] <= tile_ids[:, None], axis=1),
                                  N_EXPERTS - 1).astype(jnp.int32)

        pos_chunks = jnp.concatenate(
            [pos_flat, jnp.full((n_chunks * INV_CHUNK - n_assign,), n_rows, jnp.int32)]
        ).reshape(n_chunks, INV_CHUNK)
        row_tok = _invert_call(cnt, tile_start.astype(jnp.int32), pos_chunks, n_rows + INV_CHUNK,
                               EXPERT_TM)
        experts = jnp.arange(N_EXPERTS, dtype=jnp.int32)
        used = tiles_e > 0
        later_used = used[None, :] & (experts[None, :] > experts[:, None])
        next_e = jnp.min(jnp.where(later_used, experts[None, :], N_EXPERTS), axis=1)
        next_e = jnp.where(next_e == N_EXPERTS, -1, next_e)
        run_e = jnp.cumsum(used) - used
        tile_next = next_e[tile_expert].astype(jnp.int32)
        tile_parity = (run_e[tile_expert] % 2).astype(jnp.int32)
        ys = _expert_call(tile_expert, tile_next, tile_parity, n_valid, row_tok, h1_tiles,
                          w_gate_up, b_gate_up[l], w_down, b_down[l], l, n_tiles, EXPERT_TM)
        h = _combine_call(pos_flat, ys, gates, h1, ln_ffn_g[l], ln_ffn_b[l], alpha, tm_route)

    return h.reshape(b, lp, d)[:, :seq, :]
```

```python
import functools

import jax
import jax.numpy as jnp
import numpy as np
from jax import lax
from jax.experimental import pallas as pl
from jax.experimental.pallas import tpu as pltpu

F32 = jnp.float32
BF16 = jnp.bfloat16

N_META = 16
SB_HEADS = 4
SB_HEAD_DIM = 64
SB_WIDTH = SB_HEADS * SB_HEAD_DIM
MLA_HEADS = 8
MLA_NOPE_DIM = 64
MLA_ROPE_DIM = 32
MLA_V_DIM = 64
MLA_Q_RANK = 256
MLA_KV_RANK = 128
MLA_WIDTH = MLA_HEADS * MLA_V_DIM
ROPE_THETA = 10000.0
CONV_CH = 256
CONV_K = 31
N_EXPERTS = 32
TOP_K = 4
SWIGLU_LIMIT = 7.0
SWIGLU_ALPHA = 1.702
LN_EPS = 1e-5
RMS_EPS = 1e-6

LANES = 128
HEAD_PAIR = LANES // SB_HEAD_DIM
ROPE_HALF = MLA_ROPE_DIM // 2
VMEM_LIMIT = 48 * 1024 * 1024
EXPERT_VMEM_LIMIT = 56 * 1024 * 1024
NEG_BIG = -1e30
LOG2E = 1.4426950408889634
ATTN_TK = 256

IN_SB = 3 * SB_WIDTH
IN_CQ = IN_SB
IN_CKV = IN_CQ + MLA_Q_RANK
IN_KPE = IN_CKV + MLA_KV_RANK
IN_CA = IN_KPE + LANES
IN_CG = IN_CA + CONV_CH
IN_TOTAL = IN_CG + CONV_CH
MQ_W = MLA_HEADS * MLA_NOPE_DIM + 2 * LANES
MK_W = MLA_HEADS * MLA_NOPE_DIM + LANES


def _row_tile(n, cap, mult=16):
    best = None
    for t in range(mult, min(n, cap) + 1, mult):
        if n % t == 0:
            best = t
    assert best is not None, (n, cap, mult)
    return best


def _layer_norm(x, g, b):
    mu = jnp.mean(x, axis=-1, keepdims=True)
    xc = x - mu
    var = jnp.mean(xc * xc, axis=-1, keepdims=True)
    return xc * lax.rsqrt(var + LN_EPS) * g + b


def _rms_norm(x, g):
    return x * lax.rsqrt(jnp.mean(x * x, axis=-1, keepdims=True) + RMS_EPS) * g


def _sigmoid(x):
    return 1.0 / (1.0 + jnp.exp(-x))


def _ln_kernel(x_ref, g_ref, b_ref, o_ref):
    o_ref[...] = _layer_norm(x_ref[...], g_ref[...], b_ref[...])


def _ln_call(x2d, g, b, tm):
    t, d = x2d.shape
    return pl.pallas_call(
        _ln_kernel,
        out_shape=jax.ShapeDtypeStruct((t, d), F32),
        grid=(t // tm,),
        in_specs=[pl.BlockSpec((tm, d), lambda i: (i, 0)),
                  pl.BlockSpec((1, d), lambda i: (0, 0)),
                  pl.BlockSpec((1, d), lambda i: (0, 0))],
        out_specs=pl.BlockSpec((tm, d), lambda i: (i, 0)),
        compiler_params=pltpu.CompilerParams(dimension_semantics=("parallel",),
                                             vmem_limit_bytes=VMEM_LIMIT),
        name="ln_in",
    )(x2d, g.reshape(1, d), b.reshape(1, d))


def _inproj_kernel(h_ref, w_ref, qg_ref, wuq_ref, kvg_ref, wukv_ref, cos_ref, sin_ref,
                   sbqk_ref, sbv_ref, mq_ref, mk_ref, mv_ref, cu_ref):
    h = h_ref[0].astype(BF16)
    proj = jnp.dot(h, w_ref[...], preferred_element_type=F32)
    sbqk_ref[0, :, :SB_WIDTH] = (proj[:, :SB_WIDTH] * (SB_HEAD_DIM ** -0.5 * LOG2E)).astype(BF16)
    sbqk_ref[0, :, SB_WIDTH:] = proj[:, SB_WIDTH:2 * SB_WIDTH].astype(BF16)
    sbv_ref[0] = proj[:, 2 * SB_WIDTH:IN_SB].astype(BF16)
    cos = cos_ref[...]
    sin = sin_ref[...]

    def rot(x):
        return x * cos + pltpu.roll(x, LANES // 2, 1) * sin

    cq = _rms_norm(proj[:, IN_CQ:IN_CKV], qg_ref[...])
    qm = jnp.dot(cq.astype(BF16), wuq_ref[...], preferred_element_type=F32)
    qm = qm * ((MLA_NOPE_DIM + MLA_ROPE_DIM) ** -0.5 * LOG2E)
    nope_w = MLA_HEADS * MLA_NOPE_DIM
    mq_ref[0, :, :nope_w] = qm[:, :nope_w].astype(BF16)
    mq_ref[0, :, nope_w:nope_w + LANES] = rot(qm[:, nope_w:nope_w + LANES]).astype(BF16)
    mq_ref[0, :, nope_w + LANES:] = rot(qm[:, nope_w + LANES:]).astype(BF16)

    ckv = _rms_norm(proj[:, IN_CKV:IN_KPE], kvg_ref[...])
    kv = jnp.dot(ckv.astype(BF16), wukv_ref[...], preferred_element_type=F32)
    mk_ref[0, :, :nope_w] = kv[:, :nope_w].astype(BF16)
    mk_ref[0, :, nope_w:] = rot(proj[:, IN_KPE:IN_CA]).astype(BF16)
    mv_ref[0] = kv[:, nope_w:].astype(BF16)

    cu_ref[0] = proj[:, IN_CA:IN_CG] * _sigmoid(proj[:, IN_CG:IN_TOTAL])


def _inproj_call(h3, w_in_p, qg, wuq_p, kvg, wukv_p, cos_t, sin_t, tm):
    b, lp, d = h3.shape
    nt = lp // tm
    const2 = lambda bi, ti: (0, 0)
    tok3 = lambda bi, ti: (bi, ti, 0)
    outs = (jax.ShapeDtypeStruct((b, lp, 2 * SB_WIDTH), BF16),
            jax.ShapeDtypeStruct((b, lp, SB_WIDTH), BF16),
            jax.ShapeDtypeStruct((b, lp, MQ_W), BF16),
            jax.ShapeDtypeStruct((b, lp, MK_W), BF16),
            jax.ShapeDtypeStruct((b, lp, MLA_WIDTH), BF16),
            jax.ShapeDtypeStruct((b, lp, CONV_CH), F32))
    return pl.pallas_call(
        _inproj_kernel,
        out_shape=outs,
        grid=(b, nt),
        in_specs=[pl.BlockSpec((1, tm, d), tok3),
                  pl.BlockSpec(w_in_p.shape, const2),
                  pl.BlockSpec((1, MLA_Q_RANK), const2),
                  pl.BlockSpec(wuq_p.shape, const2),
                  pl.BlockSpec((1, MLA_KV_RANK), const2),
                  pl.BlockSpec(wukv_p.shape, const2),
                  pl.BlockSpec((tm, LANES), lambda bi, ti: (ti, 0)),
                  pl.BlockSpec((tm, LANES), lambda bi, ti: (ti, 0))],
        out_specs=tuple(pl.BlockSpec((1, tm, s.shape[2]), tok3) for s in outs),
        compiler_params=pltpu.CompilerParams(dimension_semantics=("parallel", "parallel"),
                                             vmem_limit_bytes=VMEM_LIMIT),
        name="in_proj",
    )(h3, w_in_p, qg.reshape(1, -1), wuq_p, kvg.reshape(1, -1), wukv_p, cos_t, sin_t)


def _log_stay_take2(z):
    nz = -z
    soft = jnp.log2(1.0 + jnp.exp2(jnp.minimum(z, nz)))
    return jnp.minimum(nz, 0.0) - soft, jnp.minimum(z, 0.0) - soft


def _sb_kernel(q_ref, k_ref, vt_ref, vmeta_ref, o_ref, acc_ref, c_ref, *, tq, seq):
    i = pl.program_id(1)
    nq = seq // tq
    npair = SB_HEADS // HEAD_PAIR
    lane = lax.broadcasted_iota(jnp.int32, (1, LANES), 1)
    head_sel = (lane < SB_HEAD_DIM, lane >= SB_HEAD_DIM)
    top_rows = lax.broadcasted_iota(jnp.int32, (LANES, 1), 0) < SB_HEAD_DIM
    meta = pl.ds(seq, N_META)

    def iota2(rows, cols):
        return (lax.broadcasted_iota(jnp.int32, (rows, cols), 0),
                lax.broadcasted_iota(jnp.int32, (rows, cols), 1))

    def tri_down(w):
        r, c = iota2(w, w)
        return jnp.where(c > r, 1.0, 0.0).astype(BF16)

    def tile_all(q, krows, mask, ut, first):
        zs = []
        for p in range(npair):
            cols = slice(p * LANES, (p + 1) * LANES)
            kt = k_ref[0, krows, cols]
            for hd in range(HEAD_PAIR):
                qh = jnp.where(head_sel[hd], q[:, cols], jnp.zeros((1, 1), BF16))
                zs.append(lax.dot_general(kt, qh, (((1,), (1,)), ((), ())),
                                          preferred_element_type=F32))
        stays, takes = [], []
        for h in range(SB_HEADS):
            ls, lt = _log_stay_take2(zs[h])
            if mask is not None:
                ls = jnp.where(mask, ls, 0.0)
            stays.append(ls.astype(BF16))
            takes.append(lt)
        newer = [jnp.dot(ut, st, preferred_element_type=F32) for st in stays]
        ws = []
        for h in range(SB_HEADS):
            lw = takes[h] + newer[h] if first else takes[h] + newer[h] + c_ref[h]
            w = jnp.exp2(lw)
            if mask is not None:
                w = jnp.where(mask, w, 0.0)
            ws.append(w.astype(BF16))
            total = newer[h][0:1, :] + stays[h][0:1, :].astype(F32)
            c_ref[h] = total if first else c_ref[h] + total
        for p in range(npair):
            vt = vt_ref[0, p * LANES:(p + 1) * LANES, krows]
            res = [jnp.dot(vt, ws[p * HEAD_PAIR + hd], preferred_element_type=F32)
                   for hd in range(HEAD_PAIR)]
            contrib = jnp.where(top_rows, res[0], res[1])
            acc_ref[p] = contrib if first else acc_ref[p] + contrib

    @pl.when(i < nq)
    def _():
        q = q_ref[0]
        ut = tri_down(ATTN_TK)
        r, c = iota2(ATTN_TK, tq)
        per_q = tq // ATTN_TK
        for d in range(per_q - 1, -1, -1):
            tile_all(q, pl.ds(pl.multiple_of(i * tq + d * ATTN_TK, ATTN_TK), ATTN_TK),
                     r + d * ATTN_TK < c, ut, d == per_q - 1)

        def body(t, carry):
            start = pl.multiple_of((i * per_q - 1 - t) * ATTN_TK, ATTN_TK)
            tile_all(q, pl.ds(start, ATTN_TK), None, ut, False)
            return carry

        lax.fori_loop(0, i * per_q, body, 0)
        tile_all(q, meta, None, tri_down(N_META), False)
        for p in range(npair):
            o_ref[0, :, p * LANES:(p + 1) * LANES] = acc_ref[p].T.astype(o_ref.dtype)

    @pl.when(i == nq)
    def _():
        r, c = iota2(N_META, N_META)
        mask = c < r
        ut = jnp.where(r > c, 1.0, 0.0).astype(BF16)
        for p in range(npair):
            cols = slice(p * LANES, (p + 1) * LANES)
            q = q_ref[0, 0:N_META, cols]
            kt = k_ref[0, meta, cols]
            vm = vmeta_ref[0, :, cols]
            res = []
            for hd in range(HEAD_PAIR):
                qh = jnp.where(head_sel[hd], q, jnp.zeros((1, 1), BF16))
                z = lax.dot_general(qh, kt, (((1,), (1,)), ((), ())), preferred_element_type=F32)
                ls, lt = _log_stay_take2(z)
                newer = jnp.dot(jnp.where(mask, ls, 0.0).astype(BF16), ut, preferred_element_type=F32)
                w = jnp.where(mask, jnp.exp2(lt + newer), 0.0)
                res.append(jnp.dot(w.astype(BF16), vm, preferred_element_type=F32))
            o_ref[0, 0:N_META, cols] = jnp.where(head_sel[0], res[0], res[1]).astype(o_ref.dtype)


def _sb_call(sbqk, sbv_t, sbv_meta, tq):
    b, lp, _ = sbqk.shape
    seq = lp - N_META
    nq = seq // tq
    return pl.pallas_call(
        functools.partial(_sb_kernel, tq=tq, seq=seq),
        out_shape=jax.ShapeDtypeStruct((b, lp, SB_WIDTH), BF16),
        grid=(b, nq + 1),
        in_specs=[pl.BlockSpec((1, tq, SB_WIDTH), lambda bi, i: (bi, i, 0)),
                  pl.BlockSpec((1, lp, SB_WIDTH), lambda bi, i: (bi, 0, 1)),
                  pl.BlockSpec((1, SB_WIDTH, lp), lambda bi, i: (bi, 0, 0)),
                  pl.BlockSpec((1, N_META, SB_WIDTH), lambda bi, i: (bi, 0, 0))],
        out_specs=pl.BlockSpec((1, tq, SB_WIDTH), lambda bi, i: (bi, i, 0)),
        scratch_shapes=[pltpu.VMEM((SB_HEADS // HEAD_PAIR, LANES, tq), F32),
                        pltpu.VMEM((SB_HEADS, 1, tq), F32)],
        compiler_params=pltpu.CompilerParams(
            dimension_semantics=("parallel", "arbitrary"),
            vmem_limit_bytes=VMEM_LIMIT),
        name="sb_attn",
    )(sbqk, sbqk, sbv_t, sbv_meta)


def _mla_kernel(q_ref, k_ref, vt_ref, vmeta_ref, o_ref, acc_ref, m_ref, *, tq, seq):
    i = pl.program_id(1)
    nq = seq // tq
    npair = MLA_HEADS // HEAD_PAIR
    nope_w = MLA_HEADS * MLA_NOPE_DIM
    grp = LANES // 2 // ROPE_HALF
    lane = lax.broadcasted_iota(jnp.int32, (1, LANES), 1)
    lane2 = lax.broadcasted_iota(jnp.int32, (1, 2 * LANES), 1)
    top_rows = lax.broadcasted_iota(jnp.int32, (LANES, 1), 0) < MLA_V_DIM
    meta = pl.ds(seq, N_META)
    one = jnp.ones((1, 1), BF16)

    def head_lanes(h):
        n0 = (h % HEAD_PAIR) * MLA_NOPE_DIM
        r0 = LANES + (h % grp) * ROPE_HALF
        r1 = r0 + LANES // 2
        return (((lane2 >= n0) & (lane2 < n0 + MLA_NOPE_DIM))
                | ((lane2 >= r0) & (lane2 < r0 + ROPE_HALF))
                | ((lane2 >= r1) & (lane2 < r1 + ROPE_HALF)))

    def qcat(q, p):
        rope0 = nope_w + (p * HEAD_PAIR // grp) * LANES
        return jnp.concatenate([q[:, p * LANES:(p + 1) * LANES], q[:, rope0:rope0 + LANES]], axis=-1)

    def kcat(krows, p):
        return jnp.concatenate([k_ref[0, krows, p * LANES:(p + 1) * LANES],
                                k_ref[0, krows, nope_w:nope_w + LANES]], axis=-1)

    def iota2(rows, cols):
        return (lax.broadcasted_iota(jnp.int32, (rows, cols), 0),
                lax.broadcasted_iota(jnp.int32, (rows, cols), 1))

    def tile_all(q, krows, mask, first):
        scores = []
        for p in range(npair):
            kc = kcat(krows, p)
            qc = qcat(q, p)
            for hd in range(HEAD_PAIR):
                qh = jnp.where(head_lanes(p * HEAD_PAIR + hd), qc, jnp.zeros((1, 1), BF16))
                scores.append(lax.dot_general(kc, qh, (((1,), (1,)), ((), ())),
                                              preferred_element_type=F32))
        pexps, alphas = [], []
        for h in range(MLA_HEADS):
            s = scores[h] if mask is None else jnp.where(mask, scores[h], NEG_BIG)
            mx = jnp.max(s, axis=0, keepdims=True)
            if first:
                m_new = mx
                alphas.append(None)
            else:
                m_old = m_ref[h]
                m_new = jnp.maximum(m_old, mx)
                alphas.append(jnp.exp2(m_old - m_new))
            m_ref[h] = m_new
            pexps.append(jnp.exp2(s - m_new).astype(BF16))
        for h in range(MLA_HEADS):
            p, hd = divmod(h, HEAD_PAIR)
            vt = vt_ref[0, p * LANES:(p + 1) * LANES, krows]
            vte = jnp.where(top_rows, vt, one) if hd == 0 else jnp.where(top_rows, one, vt)
            pv = jnp.dot(vte, pexps[h], preferred_element_type=F32)
            acc_ref[h] = pv if first else acc_ref[h] * alphas[h] + pv

    @pl.when(i < nq)
    def _():
        q = q_ref[0]
        r, c = iota2(ATTN_TK, tq)
        per_q = tq // ATTN_TK
        for d in range(per_q):
            tile_all(q, pl.ds(pl.multiple_of(i * tq + d * ATTN_TK, ATTN_TK), ATTN_TK),
                     r + d * ATTN_TK <= c, d == 0)

        def body(t, carry):
            tile_all(q, pl.ds(pl.multiple_of(t * ATTN_TK, ATTN_TK), ATTN_TK), None, False)
            return carry

        lax.fori_loop(0, i * per_q, body, 0)
        tile_all(q, meta, None, False)
        for p in range(npair):
            a = acc_ref[p * HEAD_PAIR]
            b = acc_ref[p * HEAD_PAIR + 1]
            out_t = jnp.where(top_rows, a * (1.0 / a[MLA_V_DIM:MLA_V_DIM + 1, :]), b * (1.0 / b[0:1, :]))
            o_ref[0, :, p * LANES:(p + 1) * LANES] = out_t.T.astype(o_ref.dtype)

    @pl.when(i == nq)
    def _():
        q = q_ref[0, 0:N_META, :]
        r, c = iota2(N_META, N_META)
        for p in range(npair):
            kc = kcat(meta, p)
            qc = qcat(q, p)
            vm = vmeta_ref[0, :, p * LANES:(p + 1) * LANES]
            res = []
            for hd in range(HEAD_PAIR):
                qh = jnp.where(head_lanes(p * HEAD_PAIR + hd), qc, jnp.zeros((1, 1), BF16))
                s = lax.dot_general(qh, kc, (((1,), (1,)), ((), ())), preferred_element_type=F32)
                s = jnp.where(c <= r, s, NEG_BIG)
                pexp = jnp.exp2(s - jnp.max(s, axis=1, keepdims=True))
                pv = jnp.dot(pexp.astype(BF16), vm, preferred_element_type=F32)
                res.append(pv * (1.0 / jnp.sum(pexp, axis=1, keepdims=True)))
            o_ref[0, 0:N_META, p * LANES:(p + 1) * LANES] = jnp.where(
                lane < MLA_V_DIM, res[0], res[1]).astype(o_ref.dtype)


def _mla_call(mq, mk, mv_t, mv_meta, tq):
    b, lp, _ = mq.shape
    seq = lp - N_META
    nq = seq // tq
    return pl.pallas_call(
        functools.partial(_mla_kernel, tq=tq, seq=seq),
        out_shape=jax.ShapeDtypeStruct((b, lp, MLA_WIDTH), BF16),
        grid=(b, nq + 1),
        in_specs=[pl.BlockSpec((1, tq, MQ_W), lambda bi, i: (bi, i, 0)),
                  pl.BlockSpec((1, lp, MK_W), lambda bi, i: (bi, 0, 0)),
                  pl.BlockSpec((1, MLA_WIDTH, lp), lambda bi, i: (bi, 0, 0)),
                  pl.BlockSpec((1, N_META, MLA_WIDTH), lambda bi, i: (bi, 0, 0))],
        out_specs=pl.BlockSpec((1, tq, MLA_WIDTH), lambda bi, i: (bi, i, 0)),
        scratch_shapes=[pltpu.VMEM((MLA_HEADS, LANES, tq), F32),
                        pltpu.VMEM((MLA_HEADS, 1, tq), F32)],
        compiler_params=pltpu.CompilerParams(
            dimension_semantics=("parallel", "arbitrary"),
            vmem_limit_bytes=VMEM_LIMIT),
        name="mla_attn",
    )(mq, mk, mv_t, mv_meta)


CONV_PAD = 32
CONV_ROWS = 128
SUBLANES = 8
CONV_WIN_EXTRA = CONV_PAD


def _conv_kernel(u_ref, w_ref, b_ref, g_ref, beta_ref, o_ref, buf_ref, *, seq):
    buf_ref[0:CONV_PAD, :] = jnp.zeros((CONV_PAD, CONV_CH), F32)
    buf_ref[CONV_PAD:CONV_PAD + N_META, :] = u_ref[0, seq:seq + N_META, :]
    buf_ref[CONV_PAD + N_META:CONV_PAD + N_META + seq, :] = u_ref[0, 0:seq, :]
    w = w_ref[...]
    lead = CONV_PAD - (CONV_K - 1)

    def finish(acc):
        y = _layer_norm(acc + b_ref[...], g_ref[...], beta_ref[...])
        return (y * _sigmoid(y)).astype(o_ref.dtype)

    def conv_rows(first_pos, rows):
        win = buf_ref[pl.ds(first_pos, rows + CONV_WIN_EXTRA), :]
        acc = jnp.zeros((rows, CONV_CH), F32)
        for sh in range(SUBLANES):
            offs = [o for o in range(lead, lead + CONV_K) if o % SUBLANES == sh]
            shifted = win[sh:sh + rows + offs[-1] - sh, :]
            for o in offs:
                k = o - lead
                acc = acc + shifted[o - sh:o - sh + rows, :] * w[k:k + 1, :]
        return finish(acc)

    def body(c, _):
        r0 = pl.multiple_of(c * CONV_ROWS, CONV_ROWS)
        o_ref[0, pl.ds(r0, CONV_ROWS), :] = conv_rows(pl.multiple_of(r0 + N_META, SUBLANES), CONV_ROWS)
        return 0

    lax.fori_loop(0, seq // CONV_ROWS, body, 0)
    o_ref[0, seq:seq + N_META, :] = conv_rows(0, N_META)


def _conv_call(cu, conv_w, conv_b, ln_g, ln_b):
    b, lp, _ = cu.shape
    seq = lp - N_META
    const2 = lambda bi: (0, 0)
    return pl.pallas_call(
        functools.partial(_conv_kernel, seq=seq),
        out_shape=jax.ShapeDtypeStruct((b, lp, CONV_CH), BF16),
        grid=(b,),
        in_specs=[pl.BlockSpec((1, lp, CONV_CH), lambda bi: (bi, 0, 0)),
                  pl.BlockSpec((CONV_K, CONV_CH), const2),
                  pl.BlockSpec((1, CONV_CH), const2),
                  pl.BlockSpec((1, CONV_CH), const2),
                  pl.BlockSpec((1, CONV_CH), const2)],
        out_specs=pl.BlockSpec((1, lp, CONV_CH), lambda bi: (bi, 0, 0)),
        scratch_shapes=[pltpu.VMEM((CONV_PAD + lp, CONV_CH), F32)],
        compiler_params=pltpu.CompilerParams(dimension_semantics=("parallel",),
                                             vmem_limit_bytes=VMEM_LIMIT),
        name="conv",
    )(cu, conv_w, conv_b.reshape(1, -1), ln_g.reshape(1, -1), ln_b.reshape(1, -1))


ROUTE_ROWS = 8


def _mix_kernel(sb_ref, mla_ref, cv_ref, h_ref, gg_ref, wo_ref, lg_ref, lb_ref, rw_ref, rb_ref,
                h1_ref, h1t_ref, idx_ref, gate_ref, rank_ref, cnt_ref, carry_ref, *, alpha, tm):
    step = pl.program_id(0)

    @pl.when(step == 0)
    def _():
        carry_ref[...] = jnp.zeros_like(carry_ref)

    gg = gg_ref[...]
    y = jnp.concatenate(
        [_rms_norm(sb_ref[...].astype(F32), gg[:, :SB_WIDTH]),
         _rms_norm(mla_ref[...].astype(F32), gg[:, SB_WIDTH:SB_WIDTH + MLA_WIDTH]),
         _rms_norm(cv_ref[...].astype(F32), gg[:, SB_WIDTH + MLA_WIDTH:])], axis=-1)
    mix = jnp.dot(y.astype(BF16), wo_ref[...], preferred_element_type=F32)
    h1 = _layer_norm(alpha * h_ref[...] + mix, lg_ref[...], lb_ref[...])
    h1_ref[...] = h1
    _store_token_tiles(h1t_ref, h1)

    logits = jnp.dot(h1.astype(BF16), rw_ref[...], preferred_element_type=F32) + rb_ref[...]
    vals = logits.T[:N_EXPERTS, :]
    eiota = lax.broadcasted_iota(jnp.int32, (N_EXPERTS, 1), 0).astype(F32)
    sels, tops, idxs = [], [], []
    for _ in range(TOP_K):
        m = jnp.max(vals, axis=0, keepdims=True)
        idx = jnp.min(jnp.where(vals == m, eiota, float(N_EXPERTS)), axis=0, keepdims=True)
        sel = eiota == idx
        vals = jnp.where(sel, -jnp.inf, vals)
        sels.append(sel)
        tops.append(m)
        idxs.append(idx)
    exps = [jnp.exp(t - tops[0]) for t in tops]
    denom = exps[0] + exps[1] + exps[2] + exps[3]

    chosen = jnp.zeros((N_EXPERTS, tm), F32)
    for sel in sels:
        chosen = chosen + jnp.where(sel, 1.0, 0.0)
    r = lax.broadcasted_iota(jnp.int32, (tm, tm), 0)
    c = lax.broadcasted_iota(jnp.int32, (tm, tm), 1)
    before = jnp.where(r < c, 1.0, 0.0).astype(BF16)
    earlier = jnp.dot(chosen.astype(BF16), before, preferred_element_type=F32) + carry_ref[...]

    row = lax.broadcasted_iota(jnp.int32, (ROUTE_ROWS, 1), 0)
    idx_out = jnp.zeros((ROUTE_ROWS, tm), jnp.int32)
    gate_out = jnp.zeros((ROUTE_ROWS, tm), F32)
    rank_out = jnp.zeros((ROUTE_ROWS, tm), jnp.int32)
    for k in range(TOP_K):
        rank_k = jnp.sum(jnp.where(sels[k], earlier, 0.0), axis=0, keepdims=True)
        idx_out = jnp.where(row == k, idxs[k].astype(jnp.int32), idx_out)
        gate_out = jnp.where(row == k, exps[k] / denom, gate_out)
        rank_out = jnp.where(row == k, rank_k.astype(jnp.int32), rank_out)
    idx_ref[0] = idx_out
    gate_ref[0] = gate_out
    rank_ref[0] = rank_out

    carry_ref[...] = carry_ref[...] + jnp.sum(chosen, axis=1, keepdims=True)
    cnt_ref[...] = carry_ref[...]


def _mix_call(sb_out, mla_out, conv_out, h2d, grp_g, w_out_b, ln_g, ln_b, router_w_b, router_b,
              alpha, tm):
    t, d = h2d.shape
    row = lambda i: (i, 0)
    const2 = lambda i: (0, 0)
    chunks = d // LANES
    steps = t // tm
    route = lambda i: (i, 0, 0)
    outs = (jax.ShapeDtypeStruct((t, d), F32),
            jax.ShapeDtypeStruct((t * chunks, LANES), F32),
            jax.ShapeDtypeStruct((steps, ROUTE_ROWS, tm), jnp.int32),
            jax.ShapeDtypeStruct((steps, ROUTE_ROWS, tm), F32),
            jax.ShapeDtypeStruct((steps, ROUTE_ROWS, tm), jnp.int32),
            jax.ShapeDtypeStruct((N_EXPERTS, 1), F32))
    rw_pad = jnp.pad(router_w_b, ((0, 0), (0, LANES - N_EXPERTS)))
    rb_pad = jnp.pad(router_b.reshape(1, N_EXPERTS), ((0, 0), (0, LANES - N_EXPERTS)))
    return pl.pallas_call(
        functools.partial(_mix_kernel, alpha=alpha, tm=tm),
        out_shape=outs,
        grid=(t // tm,),
        in_specs=[pl.BlockSpec((tm, SB_WIDTH), row),
                  pl.BlockSpec((tm, MLA_WIDTH), row),
                  pl.BlockSpec((tm, CONV_CH), row),
                  pl.BlockSpec((tm, d), row),
                  pl.BlockSpec((1, d), const2),
                  pl.BlockSpec((d, d), const2),
                  pl.BlockSpec((1, d), const2),
                  pl.BlockSpec((1, d), const2),
                  pl.BlockSpec((d, LANES), const2),
                  pl.BlockSpec((1, LANES), const2)],
        out_specs=(pl.BlockSpec((tm, d), row),
                   pl.BlockSpec((tm * chunks, LANES), row),
                   pl.BlockSpec((1, ROUTE_ROWS, tm), route),
                   pl.BlockSpec((1, ROUTE_ROWS, tm), route),
                   pl.BlockSpec((1, ROUTE_ROWS, tm), route),
                   pl.BlockSpec((N_EXPERTS, 1), const2)),
        scratch_shapes=[pltpu.VMEM((N_EXPERTS, 1), F32)],
        compiler_params=pltpu.CompilerParams(dimension_semantics=("arbitrary",),
                                             vmem_limit_bytes=VMEM_LIMIT),
        name="mix_router",
    )(sb_out, mla_out, conv_out, h2d, grp_g.reshape(1, d), w_out_b, ln_g.reshape(1, d),
      ln_b.reshape(1, d), rw_pad, rb_pad)


GATHER_AHEAD = 2
GATHER_BUFS = GATHER_AHEAD + 1


def _store_token_tiles(ref, val):
    tm, d = val.shape
    chunks = d // LANES
    for c in range(chunks):
        ref[pl.ds(c, tm, stride=chunks), :] = val[:, c * LANES:(c + 1) * LANES]


def _load_token_tiles(ref, tm, chunks):
    return jnp.concatenate([ref[pl.ds(c, tm, stride=chunks), :] for c in range(chunks)], axis=-1)


INV_CHUNK = 1024


def _invert_kernel(cnt_ref, start_ref, pos_hbm, tok_ref, chunk_ref, sem, *, n_chunks, n_out, tm):
    def fill(lo, hi):
        def body(r, _):
            tok_ref[r] = 0
            return 0
        lax.fori_loop(lo, hi, body, 0)

    def pad_rows(e, used_end):
        tiles_e = (cnt_ref[e] + (tm - 1)) // tm
        fill(start_ref[e] * tm + cnt_ref[e], (start_ref[e] + tiles_e) * tm)
        return jnp.maximum(used_end, (start_ref[e] + tiles_e) * tm)

    used_end = lax.fori_loop(0, N_EXPERTS, pad_rows, 0)
    fill(used_end, n_out)

    def chunk_copy(ch, slot):
        return pltpu.make_async_copy(pos_hbm.at[ch], chunk_ref.at[pl.ds(slot * INV_CHUNK, INV_CHUNK)],
                                     sem.at[slot])

    chunk_copy(0, 0).start()
    toks = INV_CHUNK // TOP_K

    def chunk(ch, _):
        slot = ch % 2
        chunk_copy(ch, slot).wait()

        @pl.when(ch + 1 < n_chunks)
        def _():
            chunk_copy(ch + 1, 1 - slot).start()

        first = slot * INV_CHUNK

        def body(tl, _):
            for k in range(TOP_K):
                tok_ref[chunk_ref[first + tl * TOP_K + k]] = ch * toks + tl
            return 0

        lax.fori_loop(0, toks, body, 0, unroll=16)
        return 0

    lax.fori_loop(0, n_chunks, chunk, 0)


def _invert_call(cnt, tile_start, pos_chunks, n_out, tm):
    n_chunks = pos_chunks.shape[0]
    return pl.pallas_call(
        functools.partial(_invert_kernel, n_chunks=n_chunks, n_out=n_out, tm=tm),
        out_shape=jax.ShapeDtypeStruct((n_out,), jnp.int32),
        grid_spec=pltpu.PrefetchScalarGridSpec(
            num_scalar_prefetch=2,
            grid=(1,),
            in_specs=[pl.BlockSpec(memory_space=pl.ANY)],
            out_specs=pl.BlockSpec(memory_space=pltpu.SMEM),
            scratch_shapes=[pltpu.SMEM((2 * INV_CHUNK,), jnp.int32),
                            pltpu.SemaphoreType.DMA((2,))]),
        compiler_params=pltpu.CompilerParams(dimension_semantics=("arbitrary",)),
        name="moe_row_table",
    )(cnt, tile_start, pos_chunks)


def _expert_kernel(te_ref, nxt_ref, par_ref, nv_ref, tok_ref, h_hbm, wgu_hbm, bgu_ref, wd_hbm,
                   bd_ref, y_ref, xbuf, wgu_f32, wd_f32, wgu_bf, wd_bf, sem, wsem,
                   *, layer, tm, d, d_ff):
    i = pl.program_id(0)
    nv = nv_ref[0]
    chunks = d // LANES

    def weight_copies(e, wslot):
        return (pltpu.make_async_copy(wgu_hbm.at[layer, e], wgu_f32.at[wslot], wsem.at[0, wslot]),
                pltpu.make_async_copy(wd_hbm.at[layer, e], wd_f32.at[wslot], wsem.at[1, wslot]))

    def row_copy(tile, r, slot):
        tok = tok_ref[tile * tm + r]
        return pltpu.make_async_copy(
            h_hbm.at[pl.ds(pl.multiple_of(tok * chunks, chunks), chunks), :],
            xbuf.at[slot, pl.ds(r * chunks, chunks), :], sem.at[slot])

    def wait_gather(slot):
        pltpu.make_async_copy(h_hbm.at[pl.ds(0, tm * chunks), :], xbuf.at[slot], sem.at[slot]).wait()

    @pl.when(i == 0)
    def _():
        for ahead in range(GATHER_AHEAD):
            def body(r, _):
                row_copy(ahead, r, ahead).start()
                return 0
            lax.fori_loop(0, tm, body, 0, unroll=8)
        for cp in weight_copies(te_ref[0], par_ref[0]):
            cp.start()

    @pl.when((i < nv) & ((i == 0) | (te_ref[i] != te_ref[jnp.maximum(i - 1, 0)])))
    def _():
        wslot = par_ref[i]
        for cp in weight_copies(te_ref[i], wslot):
            cp.wait()
        wgu_bf[...] = wgu_f32[wslot].astype(BF16)
        wd_bf[...] = wd_f32[wslot].astype(BF16)

        @pl.when(nxt_ref[i] >= 0)
        def _():
            for cp in weight_copies(nxt_ref[i], 1 - wslot):
                cp.start()

    for slot in range(GATHER_BUFS):
        @pl.when((i < nv) & (i % GATHER_BUFS == slot))
        def _():
            wait_gather(slot)
            x = _load_token_tiles(xbuf.at[slot], tm, chunks).astype(BF16)
            gu = jnp.dot(x, wgu_bf[...], preferred_element_type=F32) + bgu_ref[0]
            g = jnp.minimum(gu[:, :d_ff], SWIGLU_LIMIT)
            up = jnp.clip(gu[:, d_ff:], -SWIGLU_LIMIT, SWIGLU_LIMIT)
            act = (up + 1.0) * (g * _sigmoid(SWIGLU_ALPHA * g))
            y = jnp.dot(act.astype(BF16), wd_bf[...], preferred_element_type=F32) + bd_ref[0]
            _store_token_tiles(y_ref, y)
            for r in range(tm):
                row_copy(i + GATHER_AHEAD, r, (slot + GATHER_AHEAD) % GATHER_BUFS).start()

        @pl.when((i >= nv) & (i < nv + GATHER_AHEAD) & (i % GATHER_BUFS == slot))
        def _():
            wait_gather(slot)

    @pl.when(i >= nv)
    def _():
        y_ref[...] = jnp.zeros_like(y_ref)


def _expert_call(tile_expert, tile_next, tile_parity, n_valid, row_tok, h_tiles, wgu_all, bgu,
                 wd_all, bd, layer, n_tiles, tm):
    d_ff, d = wd_all.shape[2:]
    chunks = d // LANES
    bsel = lambda i, te, nxt, par, nv, tok: (te[i], 0, 0)
    return pl.pallas_call(
        functools.partial(_expert_kernel, layer=layer, tm=tm, d=d, d_ff=d_ff),
        out_shape=jax.ShapeDtypeStruct((n_tiles * tm * chunks, LANES), F32),
        grid_spec=pltpu.PrefetchScalarGridSpec(
            num_scalar_prefetch=5,
            grid=(n_tiles,),
            in_specs=[pl.BlockSpec(memory_space=pl.ANY),
                      pl.BlockSpec(memory_space=pl.ANY),
                      pl.BlockSpec((1, 1, 2 * d_ff), bsel),
                      pl.BlockSpec(memory_space=pl.ANY),
                      pl.BlockSpec((1, 1, d), bsel)],
            out_specs=pl.BlockSpec((tm * chunks, LANES), lambda i, te, nxt, par, nv, tok: (i, 0)),
            scratch_shapes=[pltpu.VMEM((GATHER_BUFS, tm * chunks, LANES), F32),
                            pltpu.VMEM((2, d, 2 * d_ff), F32),
                            pltpu.VMEM((2, d_ff, d), F32),
                            pltpu.VMEM((d, 2 * d_ff), BF16),
                            pltpu.VMEM((d_ff, d), BF16),
                            pltpu.SemaphoreType.DMA((GATHER_BUFS,)),
                            pltpu.SemaphoreType.DMA((2, 2))]),
        compiler_params=pltpu.CompilerParams(dimension_semantics=("arbitrary",),
                                             vmem_limit_bytes=EXPERT_VMEM_LIMIT),
        name="moe_experts",
    )(tile_expert, tile_next, tile_parity, n_valid, row_tok, h_tiles, wgu_all,
      bgu.reshape(N_EXPERTS, 1, -1), wd_all, bd.reshape(N_EXPERTS, 1, -1))


def _combine_kernel(pos_ref, ys_hbm, gate_ref, h_ref, lg_ref, lb_ref, o_ref, buf, sem,
                    *, alpha, tm, d):
    i = pl.program_id(0)
    chunks = d // LANES

    n_steps = pl.num_programs(0)

    def row_copy(step, r, k, slot):
        src = pos_ref[step * (tm * TOP_K) + r * TOP_K + k]
        return pltpu.make_async_copy(
            ys_hbm.at[pl.ds(pl.multiple_of(src * chunks, chunks), chunks), :],
            buf.at[slot, k, pl.ds(r * chunks, chunks), :], sem.at[slot])

    def wait_rows(slot):
        for k in range(TOP_K):
            pltpu.make_async_copy(ys_hbm.at[pl.ds(0, tm * chunks), :], buf.at[slot, k],
                                  sem.at[slot]).wait()

    @pl.when(i == 0)
    def _():
        for ahead in range(GATHER_AHEAD):
            def body(r, _):
                for k in range(TOP_K):
                    row_copy(jnp.minimum(ahead, n_steps - 1), r, k, ahead).start()
                return 0
            lax.fori_loop(0, tm, body, 0, unroll=2)

    for slot in range(GATHER_BUFS):
        @pl.when(i % GATHER_BUFS == slot)
        def _():
            wait_rows(slot)
            gate = gate_ref[...]
            ffn = _load_token_tiles(buf.at[slot, 0], tm, chunks) * gate[:, 0:1]
            for k in range(1, TOP_K):
                ffn = ffn + _load_token_tiles(buf.at[slot, k], tm, chunks) * gate[:, k:k + 1]
            o_ref[...] = _layer_norm(alpha * h_ref[...] + ffn, lg_ref[...], lb_ref[...])
            nxt = jnp.minimum(i + GATHER_AHEAD, n_steps - 1)
            for r in range(tm):
                for k in range(TOP_K):
                    row_copy(nxt, r, k, (slot + GATHER_AHEAD) % GATHER_BUFS).start()

        @pl.when((i % GATHER_BUFS == slot) & (i + 1 == n_steps))
        def _():
            for ahead in range(1, GATHER_BUFS):
                wait_rows((slot + ahead) % GATHER_BUFS)


def _combine_call(pos_flat, ys, gates, h2d, ln_g, ln_b, alpha, tm):
    t, d = h2d.shape
    chunks = d // LANES
    row = lambda i, pos: (i, 0)
    const2 = lambda i, pos: (0, 0)
    return pl.pallas_call(
        functools.partial(_combine_kernel, alpha=alpha, tm=tm, d=d),
        out_shape=jax.ShapeDtypeStruct((t, d), F32),
        grid_spec=pltpu.PrefetchScalarGridSpec(
            num_scalar_prefetch=1,
            grid=(t // tm,),
            in_specs=[pl.BlockSpec(memory_space=pl.ANY),
                      pl.BlockSpec((tm, LANES), row),
                      pl.BlockSpec((tm, d), row),
                      pl.BlockSpec((1, d), const2),
                      pl.BlockSpec((1, d), const2)],
            out_specs=pl.BlockSpec((tm, d), row),
            scratch_shapes=[pltpu.VMEM((GATHER_BUFS, TOP_K, tm * chunks, LANES), F32),
                            pltpu.SemaphoreType.DMA((GATHER_BUFS,))]),
        compiler_params=pltpu.CompilerParams(dimension_semantics=("arbitrary",),
                                             vmem_limit_bytes=VMEM_LIMIT),
        name="moe_combine",
    )(pos_flat, ys, gates, h2d, ln_g.reshape(1, d), ln_b.reshape(1, d))


def _in_proj_columns():
    o_cq = 3 * SB_WIDTH
    o_ckv = o_cq + MLA_Q_RANK
    o_kpe = o_ckv + MLA_KV_RANK
    o_ca = o_kpe + MLA_ROPE_DIM
    o_cg = o_ca + CONV_CH
    reps = LANES // 2 // ROPE_HALF
    kpe = np.concatenate([np.tile(o_kpe + np.arange(ROPE_HALF), reps),
                          np.tile(o_kpe + ROPE_HALF + np.arange(ROPE_HALF), reps)])
    return np.concatenate([np.arange(o_cq), o_cq + np.arange(MLA_Q_RANK),
                           o_ckv + np.arange(MLA_KV_RANK), kpe,
                           o_ca + np.arange(CONV_CH), o_cg + np.arange(CONV_CH)])


def _uq_columns():
    per = MLA_NOPE_DIM + MLA_ROPE_DIM
    nope = np.concatenate([h * per + np.arange(MLA_NOPE_DIM) for h in range(MLA_HEADS)])
    rope = []
    grp = LANES // 2 // ROPE_HALF
    for g0 in range(0, MLA_HEADS, grp):
        for half in range(2):
            for h in range(g0, g0 + grp):
                rope.append(h * per + MLA_NOPE_DIM + half * ROPE_HALF + np.arange(ROPE_HALF))
    return np.concatenate([nope] + rope)


def _ukv_columns():
    per = MLA_NOPE_DIM + MLA_V_DIM
    kn = np.concatenate([h * per + np.arange(MLA_NOPE_DIM) for h in range(MLA_HEADS)])
    vv = np.concatenate([h * per + MLA_NOPE_DIM + np.arange(MLA_V_DIM) for h in range(MLA_HEADS)])
    return np.concatenate([kn, vv])


def _rope_tables(seq):
    lp = seq + N_META
    inv = 1.0 / (ROPE_THETA ** (jnp.arange(0, MLA_ROPE_DIM, 2, dtype=F32) / MLA_ROPE_DIM))
    pos = jnp.concatenate([N_META + jnp.arange(seq, dtype=F32), jnp.arange(N_META, dtype=F32)])
    ang = pos[:, None] * inv[None, :]
    reps = LANES // 2 // ROPE_HALF
    cos = jnp.tile(jnp.cos(ang), (1, 2 * reps))
    sin = jnp.tile(jnp.sin(ang), (1, reps))
    assert cos.shape == (lp, LANES)
    return cos, jnp.concatenate([-sin, sin], axis=1)


ATTN_TQ = 512
EXPERT_TM = 256
ROUTE_TM = 128


def kernel(x, meta_tokens, ln_in_g, ln_in_b, w_in, q_norm_g, w_uq, kv_norm_g, w_ukv, conv_w, conv_b,
           conv_ln_g, conv_ln_b, grp_norm_g, w_out, ln_mix_g, ln_mix_b, router_w, router_b,
           w_gate_up, b_gate_up, w_down, b_down, ln_ffn_g, ln_ffn_b):
    b, seq, d = x.shape
    depth = w_in.shape[0]
    lp = seq + N_META
    t = b * lp
    alpha = float((2 * depth) ** 0.25)
    tq = min(ATTN_TQ, seq)
    assert seq % tq == 0 and tq % ATTN_TK == 0 and seq % CONV_ROWS == 0
    tm_tok = _row_tile(t, 512, LANES)
    tm_seq = _row_tile(lp, 1024)
    tm_route = _row_tile(t, ROUTE_TM, 8)
    n_assign = t * TOP_K
    n_tiles = -(-n_assign // EXPERT_TM) + N_EXPERTS + GATHER_AHEAD
    n_rows = n_tiles * EXPERT_TM
    n_chunks = -(-n_assign // INV_CHUNK)

    meta = jnp.broadcast_to(meta_tokens[None].astype(x.dtype), (b, N_META, d))
    xin = jnp.concatenate([x, meta], axis=1).reshape(t, d)
    h = _ln_call(xin, ln_in_g, ln_in_b, tm_tok)
    cos_t, sin_t = _rope_tables(seq)

    in_cols = _in_proj_columns()
    uq_cols, ukv_cols = _uq_columns(), _ukv_columns()

    for l in range(depth):
        w_in_p = w_in[l][:, in_cols].astype(BF16)
        sbqk, sbv, mq, mk, mv, cu = _inproj_call(
            h.reshape(b, lp, d), w_in_p, q_norm_g[l], w_uq[l][:, uq_cols].astype(BF16),
            kv_norm_g[l], w_ukv[l][:, ukv_cols].astype(BF16), cos_t, sin_t, tm_seq)
        sb_out = _sb_call(sbqk, sbv.transpose(0, 2, 1), sbv[:, seq:, :], tq)
        mla_out = _mla_call(mq, mk, mv.transpose(0, 2, 1), mv[:, seq:, :], tq)
        conv_out = _conv_call(cu, conv_w[l], conv_b[l], conv_ln_g[l], conv_ln_b[l])
        h1, h1_tiles, idx, gates, rank, counts = _mix_call(
            sb_out.reshape(t, -1), mla_out.reshape(t, -1), conv_out.reshape(t, -1), h,
            grp_norm_g[l], w_out[l].astype(BF16), ln_mix_g[l], ln_mix_b[l],
            router_w[l].astype(BF16), router_b[l], alpha, tm_tok)

        def per_token(a):
            return a[:, :TOP_K, :].transpose(0, 2, 1).reshape(t, TOP_K)

        idx, rank = per_token(idx), per_token(rank)
        gates = jnp.pad(per_token(gates), ((0, 0), (0, LANES - TOP_K)))
        cnt = counts[:, 0].astype(jnp.int32)
        tiles_e = (cnt + EXPERT_TM - 1) // EXPERT_TM
        tile_end = jnp.cumsum(tiles_e)
        tile_start = tile_end - tiles_e
        pos = tile_start[idx] * EXPERT_TM + rank
        pos_flat = pos.reshape(-1).astype(jnp.int32)
        n_valid = tile_end[-1:].astype(jnp.int32)
        tile_ids = jnp.minimum(jnp.arange(n_tiles, dtype=jnp.int32), n_valid[0] - 1)
        tile_expert = jnp.minimum(jnp.sum(tile_end[None, :] <= tile_ids[:, None], axis=1),
                                  N_EXPERTS - 1).astype(jnp.int32)

        pos_chunks = jnp.concatenate(
            [pos_flat, jnp.full((n_chunks * INV_CHUNK - n_assign,), n_rows, jnp.int32)]
        ).reshape(n_chunks, INV_CHUNK)
        row_tok = _invert_call(cnt, tile_start.astype(jnp.int32), pos_chunks, n_rows + INV_CHUNK,
                               EXPERT_TM)
        experts = jnp.arange(N_EXPERTS, dtype=jnp.int32)
        used = tiles_e > 0
        later_used = used[None, :] & (experts[None, :] > experts[:, None])
        next_e = jnp.min(jnp.where(later_used, experts[None, :], N_EXPERTS), axis=1)
        next_e = jnp.where(next_e == N_EXPERTS, -1, next_e)
        run_e = jnp.cumsum(used) - used
        tile_next = next_e[tile_expert].astype(jnp.int32)
        tile_parity = (run_e[tile_expert] % 2).astype(jnp.int32)
        ys = _expert_call(tile_expert, tile_next, tile_parity, n_valid, row_tok, h1_tiles,
                          w_gate_up, b_gate_up[l], w_down, b_down[l], l, n_tiles, EXPERT_TM)
        h = _combine_call(pos_flat, ys, gates, h1, ln_ffn_g[l], ln_ffn_b[l], alpha, tm_route)

    return h.reshape(b, lp, d)[:, :seq, :]
```

```python
import functools

import jax
import jax.numpy as jnp
import numpy as np
from jax import lax
from jax.experimental import pallas as pl
from jax.experimental.pallas import tpu as pltpu

F32 = jnp.float32
BF16 = jnp.bfloat16

N_META = 16
SB_HEADS = 4
SB_HEAD_DIM = 64
SB_WIDTH = SB_HEADS * SB_HEAD_DIM
MLA_HEADS = 8
MLA_NOPE_DIM = 64
MLA_ROPE_DIM = 32
MLA_V_DIM = 64
MLA_Q_RANK = 256
MLA_KV_RANK = 128
MLA_WIDTH = MLA_HEADS * MLA_V_DIM
ROPE_THETA = 10000.0
CONV_CH = 256
CONV_K = 31
N_EXPERTS = 32
TOP_K = 4
SWIGLU_LIMIT = 7.0
SWIGLU_ALPHA = 1.702
LN_EPS = 1e-5
RMS_EPS = 1e-6

LANES = 128
HEAD_PAIR = LANES // SB_HEAD_DIM
ROPE_HALF = MLA_ROPE_DIM // 2
VMEM_LIMIT = 48 * 1024 * 1024
EXPERT_VMEM_LIMIT = 56 * 1024 * 1024
NEG_BIG = -1e30
LOG2E = 1.4426950408889634
ATTN_TK = 256

IN_SB = 3 * SB_WIDTH
IN_CQ = IN_SB
IN_CKV = IN_CQ + MLA_Q_RANK
IN_KPE = IN_CKV + MLA_KV_RANK
IN_CA = IN_KPE + LANES
IN_CG = IN_CA + CONV_CH
IN_TOTAL = IN_CG + CONV_CH
MQ_W = MLA_HEADS * MLA_NOPE_DIM + 2 * LANES
MK_W = MLA_HEADS * MLA_NOPE_DIM + LANES


def _row_tile(n, cap, mult=16):
    best = None
    for t in range(mult, min(n, cap) + 1, mult):
        if n % t == 0:
            best = t
    assert best is not None, (n, cap, mult)
    return best


def _layer_norm(x, g, b):
    mu = jnp.mean(x, axis=-1, keepdims=True)
    xc = x - mu
    var = jnp.mean(xc * xc, axis=-1, keepdims=True)
    return xc * lax.rsqrt(var + LN_EPS) * g + b


def _rms_norm(x, g):
    return x * lax.rsqrt(jnp.mean(x * x, axis=-1, keepdims=True) + RMS_EPS) * g


def _sigmoid(x):
    return 1.0 / (1.0 + jnp.exp(-x))


def _ln_kernel(x_ref, g_ref, b_ref, o_ref):
    o_ref[...] = _layer_norm(x_ref[...], g_ref[...], b_ref[...])


def _ln_call(x2d, g, b, tm):
    t, d = x2d.shape
    return pl.pallas_call(
        _ln_kernel,
        out_shape=jax.ShapeDtypeStruct((t, d), F32),
        grid=(t // tm,),
        in_specs=[pl.BlockSpec((tm, d), lambda i: (i, 0)),
                  pl.BlockSpec((1, d), lambda i: (0, 0)),
                  pl.BlockSpec((1, d), lambda i: (0, 0))],
        out_specs=pl.BlockSpec((tm, d), lambda i: (i, 0)),
        compiler_params=pltpu.CompilerParams(dimension_semantics=("parallel",),
                                             vmem_limit_bytes=VMEM_LIMIT),
        name="ln_in",
    )(x2d, g.reshape(1, d), b.reshape(1, d))


def _inproj_kernel(h_ref, w_ref, qg_ref, wuq_ref, kvg_ref, wukv_ref, cos_ref, sin_ref,
                   sbqk_ref, sbv_ref, mq_ref, mk_ref, mv_ref, cu_ref):
    h = h_ref[0].astype(BF16)
    proj = jnp.dot(h, w_ref[...], preferred_element_type=F32)
    sbqk_ref[0, :, :SB_WIDTH] = (proj[:, :SB_WIDTH] * (SB_HEAD_DIM ** -0.5 * LOG2E)).astype(BF16)
    sbqk_ref[0, :, SB_WIDTH:] = proj[:, SB_WIDTH:2 * SB_WIDTH].astype(BF16)
    sbv_ref[0] = proj[:, 2 * SB_WIDTH:IN_SB].astype(BF16)
    cos = cos_ref[...]
    sin = sin_ref[...]

    def rot(x):
        return x * cos + pltpu.roll(x, LANES // 2, 1) * sin

    cq = _rms_norm(proj[:, IN_CQ:IN_CKV], qg_ref[...])
    qm = jnp.dot(cq.astype(BF16), wuq_ref[...], preferred_element_type=F32)
    qm = qm * ((MLA_NOPE_DIM + MLA_ROPE_DIM) ** -0.5 * LOG2E)
    nope_w = MLA_HEADS * MLA_NOPE_DIM
    mq_ref[0, :, :nope_w] = qm[:, :nope_w].astype(BF16)
    mq_ref[0, :, nope_w:nope_w + LANES] = rot(qm[:, nope_w:nope_w + LANES]).astype(BF16)
    mq_ref[0, :, nope_w + LANES:] = rot(qm[:, nope_w + LANES:]).astype(BF16)

    ckv = _rms_norm(proj[:, IN_CKV:IN_KPE], kvg_ref[...])
    kv = jnp.dot(ckv.astype(BF16), wukv_ref[...], preferred_element_type=F32)
    mk_ref[0, :, :nope_w] = kv[:, :nope_w].astype(BF16)
    mk_ref[0, :, nope_w:] = rot(proj[:, IN_KPE:IN_CA]).astype(BF16)
    mv_ref[0] = kv[:, nope_w:].astype(BF16)

    cu_ref[0] = proj[:, IN_CA:IN_CG] * _sigmoid(proj[:, IN_CG:IN_TOTAL])


def _inproj_call(h3, w_in_p, qg, wuq_p, kvg, wukv_p, cos_t, sin_t, tm):
    b, lp, d = h3.shape
    nt = lp // tm
    const2 = lambda bi, ti: (0, 0)
    tok3 = lambda bi, ti: (bi, ti, 0)
    outs = (jax.ShapeDtypeStruct((b, lp, 2 * SB_WIDTH), BF16),
            jax.ShapeDtypeStruct((b, lp, SB_WIDTH), BF16),
            jax.ShapeDtypeStruct((b, lp, MQ_W), BF16),
            jax.ShapeDtypeStruct((b, lp, MK_W), BF16),
            jax.ShapeDtypeStruct((b, lp, MLA_WIDTH), BF16),
            jax.ShapeDtypeStruct((b, lp, CONV_CH), F32))
    return pl.pallas_call(
        _inproj_kernel,
        out_shape=outs,
        grid=(b, nt),
        in_specs=[pl.BlockSpec((1, tm, d), tok3),
                  pl.BlockSpec(w_in_p.shape, const2),
                  pl.BlockSpec((1, MLA_Q_RANK), const2),
                  pl.BlockSpec(wuq_p.shape, const2),
                  pl.BlockSpec((1, MLA_KV_RANK), const2),
                  pl.BlockSpec(wukv_p.shape, const2),
                  pl.BlockSpec((tm, LANES), lambda bi, ti: (ti, 0)),
                  pl.BlockSpec((tm, LANES), lambda bi, ti: (ti, 0))],
        out_specs=tuple(pl.BlockSpec((1, tm, s.shape[2]), tok3) for s in outs),
        compiler_params=pltpu.CompilerParams(dimension_semantics=("parallel", "parallel"),
                                             vmem_limit_bytes=VMEM_LIMIT),
        name="in_proj",
    )(h3, w_in_p, qg.reshape(1, -1), wuq_p, kvg.reshape(1, -1), wukv_p, cos_t, sin_t)


def _log_stay_take2(z):
    nz = -z
    soft = jnp.log2(1.0 + jnp.exp2(jnp.minimum(z, nz)))
    return jnp.minimum(nz, 0.0) - soft, jnp.minimum(z, 0.0) - soft


def _sb_kernel(q_ref, k_ref, vt_ref, vmeta_ref, o_ref, acc_ref, c_ref, *, tq, seq):
    i = pl.program_id(1)
    nq = seq // tq
    npair = SB_HEADS // HEAD_PAIR
    lane = lax.broadcasted_iota(jnp.int32, (1, LANES), 1)
    head_sel = (lane < SB_HEAD_DIM, lane >= SB_HEAD_DIM)
    top_rows = lax.broadcasted_iota(jnp.int32, (LANES, 1), 0) < SB_HEAD_DIM
    meta = pl.ds(seq, N_META)

    def iota2(rows, cols):
        return (lax.broadcasted_iota(jnp.int32, (rows, cols), 0),
                lax.broadcasted_iota(jnp.int32, (rows, cols), 1))

    def tri_down(w):
        r, c = iota2(w, w)
        return jnp.where(c > r, 1.0, 0.0).astype(BF16)

    def tile_all(q, krows, mask, ut, first):
        zs = []
        for p in range(npair):
            cols = slice(p * LANES, (p + 1) * LANES)
            kt = k_ref[0, krows, cols]
            for hd in range(HEAD_PAIR):
                qh = jnp.where(head_sel[hd], q[:, cols], jnp.zeros((1, 1), BF16))
                zs.append(lax.dot_general(kt, qh, (((1,), (1,)), ((), ())),
                                          preferred_element_type=F32))
        stays, takes = [], []
        for h in range(SB_HEADS):
            ls, lt = _log_stay_take2(zs[h])
            if mask is not None:
                ls = jnp.where(mask, ls, 0.0)
            stays.append(ls.astype(BF16))
            takes.append(lt)
        newer = [jnp.dot(ut, st, preferred_element_type=F32) for st in stays]
        ws = []
        for h in range(SB_HEADS):
            lw = takes[h] + newer[h] if first else takes[h] + newer[h] + c_ref[h]
            w = jnp.exp2(lw)
            if mask is not None:
                w = jnp.where(mask, w, 0.0)
            ws.append(w.astype(BF16))
            total = newer[h][0:1, :] + stays[h][0:1, :].astype(F32)
            c_ref[h] = total if first else c_ref[h] + total
        for p in range(npair):
            vt = vt_ref[0, p * LANES:(p + 1) * LANES, krows]
            res = [jnp.dot(vt, ws[p * HEAD_PAIR + hd], preferred_element_type=F32)
                   for hd in range(HEAD_PAIR)]
            contrib = jnp.where(top_rows, res[0], res[1])
            acc_ref[p] = contrib if first else acc_ref[p] + contrib

    @pl.when(i < nq)
    def _():
        q = q_ref[0]
        ut = tri_down(ATTN_TK)
        r, c = iota2(ATTN_TK, tq)
        per_q = tq // ATTN_TK
        for d in range(per_q - 1, -1, -1):
            tile_all(q, pl.ds(pl.multiple_of(i * tq + d * ATTN_TK, ATTN_TK), ATTN_TK),
                     r + d * ATTN_TK < c, ut, d == per_q - 1)

        def body(t, carry):
            start = pl.multiple_of((i * per_q - 1 - t) * ATTN_TK, ATTN_TK)
            tile_all(q, pl.ds(start, ATTN_TK), None, ut, False)
            return carry

        lax.fori_loop(0, i * per_q, body, 0)
        tile_all(q, meta, None, tri_down(N_META), False)
        for p in range(npair):
            o_ref[0, :, p * LANES:(p + 1) * LANES] = acc_ref[p].T.astype(o_ref.dtype)

    @pl.when(i == nq)
    def _():
        r, c = iota2(N_META, N_META)
        mask = c < r
        ut = jnp.where(r > c, 1.0, 0.0).astype(BF16)
        for p in range(npair):
            cols = slice(p * LANES, (p + 1) * LANES)
            q = q_ref[0, 0:N_META, cols]
            kt = k_ref[0, meta, cols]
            vm = vmeta_ref[0, :, cols]
            res = []
            for hd in range(HEAD_PAIR):
                qh = jnp.where(head_sel[hd], q, jnp.zeros((1, 1), BF16))
                z = lax.dot_general(qh, kt, (((1,), (1,)), ((), ())), preferred_element_type=F32)
                ls, lt = _log_stay_take2(z)
                newer = jnp.dot(jnp.where(mask, ls, 0.0).astype(BF16), ut, preferred_element_type=F32)
                w = jnp.where(mask, jnp.exp2(lt + newer), 0.0)
                res.append(jnp.dot(w.astype(BF16), vm, preferred_element_type=F32))
            o_ref[0, 0:N_META, cols] = jnp.where(head_sel[0], res[0], res[1]).astype(o_ref.dtype)


def _sb_call(sbqk, sbv_t, sbv_meta, tq):
    b, lp, _ = sbqk.shape
    seq = lp - N_META
    nq = seq // tq
    return pl.pallas_call(
        functools.partial(_sb_kernel, tq=tq, seq=seq),
        out_shape=jax.ShapeDtypeStruct((b, lp, SB_WIDTH), BF16),
        grid=(b, nq + 1),
        in_specs=[pl.BlockSpec((1, tq, SB_WIDTH), lambda bi, i: (bi, i, 0)),
                  pl.BlockSpec((1, lp, SB_WIDTH), lambda bi, i: (bi, 0, 1)),
                  pl.BlockSpec((1, SB_WIDTH, lp), lambda bi, i: (bi, 0, 0)),
                  pl.BlockSpec((1, N_META, SB_WIDTH), lambda bi, i: (bi, 0, 0))],
        out_specs=pl.BlockSpec((1, tq, SB_WIDTH), lambda bi, i: (bi, i, 0)),
        scratch_shapes=[pltpu.VMEM((SB_HEADS // HEAD_PAIR, LANES, tq), F32),
                        pltpu.VMEM((SB_HEADS, 1, tq), F32)],
        compiler_params=pltpu.CompilerParams(
            dimension_semantics=("parallel", "arbitrary"),
            vmem_limit_bytes=VMEM_LIMIT),
        name="sb_attn",
    )(sbqk, sbqk, sbv_t, sbv_meta)


def _mla_kernel(q_ref, k_ref, vt_ref, vmeta_ref, o_ref, acc_ref, m_ref, *, tq, seq):
    i = pl.program_id(1)
    nq = seq // tq
    npair = MLA_HEADS // HEAD_PAIR
    nope_w = MLA_HEADS * MLA_NOPE_DIM
    grp = LANES // 2 // ROPE_HALF
    lane = lax.broadcasted_iota(jnp.int32, (1, LANES), 1)
    lane2 = lax.broadcasted_iota(jnp.int32, (1, 2 * LANES), 1)
    top_rows = lax.broadcasted_iota(jnp.int32, (LANES, 1), 0) < MLA_V_DIM
    meta = pl.ds(seq, N_META)
    one = jnp.ones((1, 1), BF16)

    def head_lanes(h):
        n0 = (h % HEAD_PAIR) * MLA_NOPE_DIM
        r0 = LANES + (h % grp) * ROPE_HALF
        r1 = r0 + LANES // 2
        return (((lane2 >= n0) & (lane2 < n0 + MLA_NOPE_DIM))
                | ((lane2 >= r0) & (lane2 < r0 + ROPE_HALF))
                | ((lane2 >= r1) & (lane2 < r1 + ROPE_HALF)))

    def qcat(q, p):
        rope0 = nope_w + (p * HEAD_PAIR // grp) * LANES
        return jnp.concatenate([q[:, p * LANES:(p + 1) * LANES], q[:, rope0:rope0 + LANES]], axis=-1)

    def kcat(krows, p):
        return jnp.concatenate([k_ref[0, krows, p * LANES:(p + 1) * LANES],
                                k_ref[0, krows, nope_w:nope_w + LANES]], axis=-1)

    def iota2(rows, cols):
        return (lax.broadcasted_iota(jnp.int32, (rows, cols), 0),
                lax.broadcasted_iota(jnp.int32, (rows, cols), 1))

    def tile_all(q, krows, mask, first):
        scores = []
        for p in range(npair):
            kc = kcat(krows, p)
            qc = qcat(q, p)
            for hd in range(HEAD_PAIR):
                qh = jnp.where(head_lanes(p * HEAD_PAIR + hd), qc, jnp.zeros((1, 1), BF16))
                scores.append(lax.dot_general(kc, qh, (((1,), (1,)), ((), ())),
                                              preferred_element_type=F32))
        pexps, alphas = [], []
        for h in range(MLA_HEADS):
            s = scores[h] if mask is None else jnp.where(mask, scores[h], NEG_BIG)
            mx = jnp.max(s, axis=0, keepdims=True)
            if first:
                m_new = mx
                alphas.append(None)
            else:
                m_old = m_ref[h]
                m_new = jnp.maximum(m_old, mx)
                alphas.append(jnp.exp2(m_old - m_new))
            m_ref[h] = m_new
            pexps.append(jnp.exp2(s - m_new).astype(BF16))
        for h in range(MLA_HEADS):
            p, hd = divmod(h, HEAD_PAIR)
            vt = vt_ref[0, p * LANES:(p + 1) * LANES, krows]
            vte = jnp.where(top_rows, vt, one) if hd == 0 else jnp.where(top_rows, one, vt)
            pv = jnp.dot(vte, pexps[h], preferred_element_type=F32)
            acc_ref[h] = pv if first else acc_ref[h] * alphas[h] + pv

    @pl.when(i < nq)
    def _():
        q = q_ref[0]
        r, c = iota2(ATTN_TK, tq)
        per_q = tq // ATTN_TK
        for d in range(per_q):
            tile_all(q, pl.ds(pl.multiple_of(i * tq + d * ATTN_TK, ATTN_TK), ATTN_TK),
                     r + d * ATTN_TK <= c, d == 0)

        def body(t, carry):
            tile_all(q, pl.ds(pl.multiple_of(t * ATTN_TK, ATTN_TK), ATTN_TK), None, False)
            return carry

        lax.fori_loop(0, i * per_q, body, 0)
        tile_all(q, meta, None, False)
        for p in range(npair):
            a = acc_ref[p * HEAD_PAIR]
            b = acc_ref[p * HEAD_PAIR + 1]
            out_t = jnp.where(top_rows, a * (1.0 / a[MLA_V_DIM:MLA_V_DIM + 1, :]), b * (1.0 / b[0:1, :]))
            o_ref[0, :, p * LANES:(p + 1) * LANES] = out_t.T.astype(o_ref.dtype)

    @pl.when(i == nq)
    def _():
        q = q_ref[0, 0:N_META, :]
        r, c = iota2(N_META, N_META)
        for p in range(npair):
            kc = kcat(meta, p)
            qc = qcat(q, p)
            vm = vmeta_ref[0, :, p * LANES:(p + 1) * LANES]
            res = []
            for hd in range(HEAD_PAIR):
                qh = jnp.where(head_lanes(p * HEAD_PAIR + hd), qc, jnp.zeros((1, 1), BF16))
                s = lax.dot_general(qh, kc, (((1,), (1,)), ((), ())), preferred_element_type=F32)
                s = jnp.where(c <= r, s, NEG_BIG)
                pexp = jnp.exp2(s - jnp.max(s, axis=1, keepdims=True))
                pv = jnp.dot(pexp.astype(BF16), vm, preferred_element_type=F32)
                res.append(pv * (1.0 / jnp.sum(pexp, axis=1, keepdims=True)))
            o_ref[0, 0:N_META, p * LANES:(p + 1) * LANES] = jnp.where(
                lane < MLA_V_DIM, res[0], res[1]).astype(o_ref.dtype)


def _mla_call(mq, mk, mv_t, mv_meta, tq):
    b, lp, _ = mq.shape
    seq = lp - N_META
    nq = seq // tq
    return pl.pallas_call(
        functools.partial(_mla_kernel, tq=tq, seq=seq),
        out_shape=jax.ShapeDtypeStruct((b, lp, MLA_WIDTH), BF16),
        grid=(b, nq + 1),
        in_specs=[pl.BlockSpec((1, tq, MQ_W), lambda bi, i: (bi, i, 0)),
                  pl.BlockSpec((1, lp, MK_W), lambda bi, i: (bi, 0, 0)),
                  pl.BlockSpec((1, MLA_WIDTH, lp), lambda bi, i: (bi, 0, 0)),
                  pl.BlockSpec((1, N_META, MLA_WIDTH), lambda bi, i: (bi, 0, 0))],
        out_specs=pl.BlockSpec((1, tq, MLA_WIDTH), lambda bi, i: (bi, i, 0)),
        scratch_shapes=[pltpu.VMEM((MLA_HEADS, LANES, tq), F32),
                        pltpu.VMEM((MLA_HEADS, 1, tq), F32)],
        compiler_params=pltpu.CompilerParams(
            dimension_semantics=("parallel", "arbitrary"),
            vmem_limit_bytes=VMEM_LIMIT),
        name="mla_attn",
    )(mq, mk, mv_t, mv_meta)


CONV_PAD = 32
CONV_ROWS = 128
SUBLANES = 8
CONV_WIN_EXTRA = CONV_PAD


def _conv_kernel(u_ref, w_ref, b_ref, g_ref, beta_ref, o_ref, buf_ref, *, seq):
    buf_ref[0:CONV_PAD, :] = jnp.zeros((CONV_PAD, CONV_CH), F32)
    buf_ref[CONV_PAD:CONV_PAD + N_META, :] = u_ref[0, seq:seq + N_META, :]
    buf_ref[CONV_PAD + N_META:CONV_PAD + N_META + seq, :] = u_ref[0, 0:seq, :]
    tail = CONV_PAD + N_META + seq
    buf_ref[tail:tail + SUBLANES, :] = jnp.zeros((SUBLANES, CONV_CH), F32)
    w = w_ref[...]
    lead = CONV_PAD - (CONV_K - 1)

    def finish(acc):
        y = _layer_norm(acc + b_ref[...], g_ref[...], beta_ref[...])
        return (y * _sigmoid(y)).astype(o_ref.dtype)

    def conv_rows(first_pos, rows):
        win = buf_ref[pl.ds(first_pos, rows + CONV_WIN_EXTRA + SUBLANES), :]
        acc = jnp.zeros((rows, CONV_CH), F32)
        for sh in range(SUBLANES):
            part = None
            for o in range(lead, lead + CONV_K):
                if o % SUBLANES != sh:
                    continue
                k = o - lead
                term = win[o - sh:o - sh + rows + SUBLANES, :] * w[k:k + 1, :]
                part = term if part is None else part + term
            acc = acc + part[sh:sh + rows, :]
        return finish(acc)

    def body(c, _):
        r0 = pl.multiple_of(c * CONV_ROWS, CONV_ROWS)
        o_ref[0, pl.ds(r0, CONV_ROWS), :] = conv_rows(pl.multiple_of(r0 + N_META, SUBLANES), CONV_ROWS)
        return 0

    lax.fori_loop(0, seq // CONV_ROWS, body, 0)
    o_ref[0, seq:seq + N_META, :] = conv_rows(0, N_META)


def _conv_call(cu, conv_w, conv_b, ln_g, ln_b):
    b, lp, _ = cu.shape
    seq = lp - N_META
    const2 = lambda bi: (0, 0)
    return pl.pallas_call(
        functools.partial(_conv_kernel, seq=seq),
        out_shape=jax.ShapeDtypeStruct((b, lp, CONV_CH), BF16),
        grid=(b,),
        in_specs=[pl.BlockSpec((1, lp, CONV_CH), lambda bi: (bi, 0, 0)),
                  pl.BlockSpec((CONV_K, CONV_CH), const2),
                  pl.BlockSpec((1, CONV_CH), const2),
                  pl.BlockSpec((1, CONV_CH), const2),
                  pl.BlockSpec((1, CONV_CH), const2)],
        out_specs=pl.BlockSpec((1, lp, CONV_CH), lambda bi: (bi, 0, 0)),
        scratch_shapes=[pltpu.VMEM((CONV_PAD + lp + SUBLANES, CONV_CH), F32)],
        compiler_params=pltpu.CompilerParams(dimension_semantics=("parallel",),
                                             vmem_limit_bytes=VMEM_LIMIT),
        name="conv",
    )(cu, conv_w, conv_b.reshape(1, -1), ln_g.reshape(1, -1), ln_b.reshape(1, -1))


ROUTE_ROWS = 8


def _mix_kernel(sb_ref, mla_ref, cv_ref, h_ref, gg_ref, wo_ref, lg_ref, lb_ref, rw_ref, rb_ref,
                h1_ref, h1t_ref, idx_ref, gate_ref, rank_ref, cnt_ref, carry_ref, *, alpha, tm):
    step = pl.program_id(0)

    @pl.when(step == 0)
    def _():
        carry_ref[...] = jnp.zeros_like(carry_ref)

    gg = gg_ref[...]
    y = jnp.concatenate(
        [_rms_norm(sb_ref[...].astype(F32), gg[:, :SB_WIDTH]),
         _rms_norm(mla_ref[...].astype(F32), gg[:, SB_WIDTH:SB_WIDTH + MLA_WIDTH]),
         _rms_norm(cv_ref[...].astype(F32), gg[:, SB_WIDTH + MLA_WIDTH:])], axis=-1)
    mix = jnp.dot(y.astype(BF16), wo_ref[...], preferred_element_type=F32)
    h1 = _layer_norm(alpha * h_ref[...] + mix, lg_ref[...], lb_ref[...])
    h1_ref[...] = h1
    _store_token_tiles(h1t_ref, h1)

    logits = jnp.dot(h1.astype(BF16), rw_ref[...], preferred_element_type=F32) + rb_ref[...]
    vals = logits.T[:N_EXPERTS, :]
    eiota = lax.broadcasted_iota(jnp.int32, (N_EXPERTS, 1), 0).astype(F32)
    sels, tops, idxs = [], [], []
    for _ in range(TOP_K):
        m = jnp.max(vals, axis=0, keepdims=True)
        idx = jnp.min(jnp.where(vals == m, eiota, float(N_EXPERTS)), axis=0, keepdims=True)
        sel = eiota == idx
        vals = jnp.where(sel, -jnp.inf, vals)
        sels.append(sel)
        tops.append(m)
        idxs.append(idx)
    exps = [jnp.exp(t - tops[0]) for t in tops]
    denom = exps[0] + exps[1] + exps[2] + exps[3]

    chosen = jnp.zeros((N_EXPERTS, tm), F32)
    for sel in sels:
        chosen = chosen + jnp.where(sel, 1.0, 0.0)
    r = lax.broadcasted_iota(jnp.int32, (tm, tm), 0)
    c = lax.broadcasted_iota(jnp.int32, (tm, tm), 1)
    before = jnp.where(r < c, 1.0, 0.0).astype(BF16)
    earlier = jnp.dot(chosen.astype(BF16), before, preferred_element_type=F32) + carry_ref[...]

    row = lax.broadcasted_iota(jnp.int32, (ROUTE_ROWS, 1), 0)
    idx_out = jnp.zeros((ROUTE_ROWS, tm), jnp.int32)
    gate_out = jnp.zeros((ROUTE_ROWS, tm), F32)
    rank_out = jnp.zeros((ROUTE_ROWS, tm), jnp.int32)
    for k in range(TOP_K):
        rank_k = jnp.sum(jnp.where(sels[k], earlier, 0.0), axis=0, keepdims=True)
        idx_out = jnp.where(row == k, idxs[k].astype(jnp.int32), idx_out)
        gate_out = jnp.where(row == k, exps[k] / denom, gate_out)
        rank_out = jnp.where(row == k, rank_k.astype(jnp.int32), rank_out)
    idx_ref[0] = idx_out
    gate_ref[0] = gate_out
    rank_ref[0] = rank_out

    carry_ref[...] = carry_ref[...] + jnp.sum(chosen, axis=1, keepdims=True)
    cnt_ref[...] = carry_ref[...]


def _mix_call(sb_out, mla_out, conv_out, h2d, grp_g, w_out_b, ln_g, ln_b, router_w_b, router_b,
              alpha, tm):
    t, d = h2d.shape
    row = lambda i: (i, 0)
    const2 = lambda i: (0, 0)
    chunks = d // LANES
    steps = t // tm
    route = lambda i: (i, 0, 0)
    outs = (jax.ShapeDtypeStruct((t, d), F32),
            jax.ShapeDtypeStruct((t * chunks, LANES), F32),
            jax.ShapeDtypeStruct((steps, ROUTE_ROWS, tm), jnp.int32),
            jax.ShapeDtypeStruct((steps, ROUTE_ROWS, tm), F32),
            jax.ShapeDtypeStruct((steps, ROUTE_ROWS, tm), jnp.int32),
            jax.ShapeDtypeStruct((N_EXPERTS, 1), F32))
    rw_pad = jnp.pad(router_w_b, ((0, 0), (0, LANES - N_EXPERTS)))
    rb_pad = jnp.pad(router_b.reshape(1, N_EXPERTS), ((0, 0), (0, LANES - N_EXPERTS)))
    return pl.pallas_call(
        functools.partial(_mix_kernel, alpha=alpha, tm=tm),
        out_shape=outs,
        grid=(t // tm,),
        in_specs=[pl.BlockSpec((tm, SB_WIDTH), row),
                  pl.BlockSpec((tm, MLA_WIDTH), row),
                  pl.BlockSpec((tm, CONV_CH), row),
                  pl.BlockSpec((tm, d), row),
                  pl.BlockSpec((1, d), const2),
                  pl.BlockSpec((d, d), const2),
                  pl.BlockSpec((1, d), const2),
                  pl.BlockSpec((1, d), const2),
                  pl.BlockSpec((d, LANES), const2),
                  pl.BlockSpec((1, LANES), const2)],
        out_specs=(pl.BlockSpec((tm, d), row),
                   pl.BlockSpec((tm * chunks, LANES), row),
                   pl.BlockSpec((1, ROUTE_ROWS, tm), route),
                   pl.BlockSpec((1, ROUTE_ROWS, tm), route),
                   pl.BlockSpec((1, ROUTE_ROWS, tm), route),
                   pl.BlockSpec((N_EXPERTS, 1), const2)),
        scratch_shapes=[pltpu.VMEM((N_EXPERTS, 1), F32)],
        compiler_params=pltpu.CompilerParams(dimension_semantics=("arbitrary",),
                                             vmem_limit_bytes=VMEM_LIMIT),
        name="mix_router",
    )(sb_out, mla_out, conv_out, h2d, grp_g.reshape(1, d), w_out_b, ln_g.reshape(1, d),
      ln_b.reshape(1, d), rw_pad, rb_pad)


GATHER_AHEAD = 2
GATHER_BUFS = GATHER_AHEAD + 1


def _store_token_tiles(ref, val):
    tm, d = val.shape
    chunks = d // LANES
    for c in range(chunks):
        ref[pl.ds(c, tm, stride=chunks), :] = val[:, c * LANES:(c + 1) * LANES]


def _load_token_tiles(ref, tm, chunks):
    return jnp.concatenate([ref[pl.ds(c, tm, stride=chunks), :] for c in range(chunks)], axis=-1)


INV_CHUNK = 1024


def _invert_kernel(cnt_ref, start_ref, pos_hbm, tok_ref, chunk_ref, sem, *, n_chunks, n_out, tm):
    def fill(lo, hi):
        def body(r, _):
            tok_ref[r] = 0
            return 0
        lax.fori_loop(lo, hi, body, 0)

    def pad_rows(e, used_end):
        tiles_e = (cnt_ref[e] + (tm - 1)) // tm
        fill(start_ref[e] * tm + cnt_ref[e], (start_ref[e] + tiles_e) * tm)
        return jnp.maximum(used_end, (start_ref[e] + tiles_e) * tm)

    used_end = lax.fori_loop(0, N_EXPERTS, pad_rows, 0)
    fill(used_end, n_out)

    def chunk_copy(ch, slot):
        return pltpu.make_async_copy(pos_hbm.at[ch], chunk_ref.at[pl.ds(slot * INV_CHUNK, INV_CHUNK)],
                                     sem.at[slot])

    chunk_copy(0, 0).start()
    toks = INV_CHUNK // TOP_K

    def chunk(ch, _):
        slot = ch % 2
        chunk_copy(ch, slot).wait()

        @pl.when(ch + 1 < n_chunks)
        def _():
            chunk_copy(ch + 1, 1 - slot).start()

        first = slot * INV_CHUNK

        def body(tl, _):
            for k in range(TOP_K):
                tok_ref[chunk_ref[first + tl * TOP_K + k]] = ch * toks + tl
            return 0

        lax.fori_loop(0, toks, body, 0, unroll=16)
        return 0

    lax.fori_loop(0, n_chunks, chunk, 0)


def _invert_call(cnt, tile_start, pos_chunks, n_out, tm):
    n_chunks = pos_chunks.shape[0]
    return pl.pallas_call(
        functools.partial(_invert_kernel, n_chunks=n_chunks, n_out=n_out, tm=tm),
        out_shape=jax.ShapeDtypeStruct((n_out,), jnp.int32),
        grid_spec=pltpu.PrefetchScalarGridSpec(
            num_scalar_prefetch=2,
            grid=(1,),
            in_specs=[pl.BlockSpec(memory_space=pl.ANY)],
            out_specs=pl.BlockSpec(memory_space=pltpu.SMEM),
            scratch_shapes=[pltpu.SMEM((2 * INV_CHUNK,), jnp.int32),
                            pltpu.SemaphoreType.DMA((2,))]),
        compiler_params=pltpu.CompilerParams(dimension_semantics=("arbitrary",)),
        name="moe_row_table",
    )(cnt, tile_start, pos_chunks)


def _expert_kernel(te_ref, nxt_ref, par_ref, nv_ref, tok_ref, h_hbm, wgu_hbm, bgu_ref, wd_hbm,
                   bd_ref, y_ref, xbuf, wgu_f32, wd_f32, wgu_bf, wd_bf, sem, wsem,
                   *, layer, tm, d, d_ff):
    i = pl.program_id(0)
    nv = nv_ref[0]
    chunks = d // LANES

    def weight_copies(e, wslot):
        return (pltpu.make_async_copy(wgu_hbm.at[layer, e], wgu_f32.at[wslot], wsem.at[0, wslot]),
                pltpu.make_async_copy(wd_hbm.at[layer, e], wd_f32.at[wslot], wsem.at[1, wslot]))

    def row_copy(tile, r, slot):
        tok = tok_ref[tile * tm + r]
        return pltpu.make_async_copy(
            h_hbm.at[pl.ds(pl.multiple_of(tok * chunks, chunks), chunks), :],
            xbuf.at[slot, pl.ds(r * chunks, chunks), :], sem.at[slot])

    def wait_gather(slot):
        pltpu.make_async_copy(h_hbm.at[pl.ds(0, tm * chunks), :], xbuf.at[slot], sem.at[slot]).wait()

    @pl.when(i == 0)
    def _():
        for ahead in range(GATHER_AHEAD):
            def body(r, _):
                row_copy(ahead, r, ahead).start()
                return 0
            lax.fori_loop(0, tm, body, 0, unroll=8)
        for cp in weight_copies(te_ref[0], par_ref[0]):
            cp.start()

    @pl.when((i < nv) & ((i == 0) | (te_ref[i] != te_ref[jnp.maximum(i - 1, 0)])))
    def _():
        wslot = par_ref[i]
        for cp in weight_copies(te_ref[i], wslot):
            cp.wait()
        wgu_bf[...] = wgu_f32[wslot].astype(BF16)
        wd_bf[...] = wd_f32[wslot].astype(BF16)

        @pl.when(nxt_ref[i] >= 0)
        def _():
            for cp in weight_copies(nxt_ref[i], 1 - wslot):
                cp.start()

    for slot in range(GATHER_BUFS):
        @pl.when((i < nv) & (i % GATHER_BUFS == slot))
        def _():
            wait_gather(slot)
            x = _load_token_tiles(xbuf.at[slot], tm, chunks).astype(BF16)
            gu = jnp.dot(x, wgu_bf[...], preferred_element_type=F32) + bgu_ref[0]
            g = jnp.minimum(gu[:, :d_ff], SWIGLU_LIMIT)
            up = jnp.clip(gu[:, d_ff:], -SWIGLU_LIMIT, SWIGLU_LIMIT)
            act = (up + 1.0) * (g * _sigmoid(SWIGLU_ALPHA * g))
            y = jnp.dot(act.astype(BF16), wd_bf[...], preferred_element_type=F32) + bd_ref[0]
            _store_token_tiles(y_ref, y)
            for r in range(tm):
                row_copy(i + GATHER_AHEAD, r, (slot + GATHER_AHEAD) % GATHER_BUFS).start()

        @pl.when((i >= nv) & (i < nv + GATHER_AHEAD) & (i % GATHER_BUFS == slot))
        def _():
            wait_gather(slot)

    @pl.when(i >= nv)
    def _():
        y_ref[...] = jnp.zeros_like(y_ref)


def _expert_call(tile_expert, tile_next, tile_parity, n_valid, row_tok, h_tiles, wgu_all, bgu,
                 wd_all, bd, layer, n_tiles, tm):
    d_ff, d = wd_all.shape[2:]
    chunks = d // LANES
    bsel = lambda i, te, nxt, par, nv, tok: (te[i], 0, 0)
    return pl.pallas_call(
        functools.partial(_expert_kernel, layer=layer, tm=tm, d=d, d_ff=d_ff),
        out_shape=jax.ShapeDtypeStruct((n_tiles * tm * chunks, LANES), F32),
        grid_spec=pltpu.PrefetchScalarGridSpec(
            num_scalar_prefetch=5,
            grid=(n_tiles,),
            in_specs=[pl.BlockSpec(memory_space=pl.ANY),
                      pl.BlockSpec(memory_space=pl.ANY),
                      pl.BlockSpec((1, 1, 2 * d_ff), bsel),
                      pl.BlockSpec(memory_space=pl.ANY),
                      pl.BlockSpec((1, 1, d), bsel)],
            out_specs=pl.BlockSpec((tm * chunks, LANES), lambda i, te, nxt, par, nv, tok: (i, 0)),
            scratch_shapes=[pltpu.VMEM((GATHER_BUFS, tm * chunks, LANES), F32),
                            pltpu.VMEM((2, d, 2 * d_ff), F32),
                            pltpu.VMEM((2, d_ff, d), F32),
                            pltpu.VMEM((d, 2 * d_ff), BF16),
                            pltpu.VMEM((d_ff, d), BF16),
                            pltpu.SemaphoreType.DMA((GATHER_BUFS,)),
                            pltpu.SemaphoreType.DMA((2, 2))]),
        compiler_params=pltpu.CompilerParams(dimension_semantics=("arbitrary",),
                                             vmem_limit_bytes=EXPERT_VMEM_LIMIT),
        name="moe_experts",
    )(tile_expert, tile_next, tile_parity, n_valid, row_tok, h_tiles, wgu_all,
      bgu.reshape(N_EXPERTS, 1, -1), wd_all, bd.reshape(N_EXPERTS, 1, -1))


def _combine_kernel(pos_ref, ys_hbm, gate_ref, h_ref, lg_ref, lb_ref, o_ref, buf, sem,
                    *, alpha, tm, d):
    i = pl.program_id(0)
    chunks = d // LANES

    n_steps = pl.num_programs(0)

    def row_copy(step, r, k, slot):
        src = pos_ref[step * (tm * TOP_K) + r * TOP_K + k]
        return pltpu.make_async_copy(
            ys_hbm.at[pl.ds(pl.multiple_of(src * chunks, chunks), chunks), :],
            buf.at[slot, k, pl.ds(r * chunks, chunks), :], sem.at[slot])

    def wait_rows(slot):
        for k in range(TOP_K):
            pltpu.make_async_copy(ys_hbm.at[pl.ds(0, tm * chunks), :], buf.at[slot, k],
                                  sem.at[slot]).wait()

    @pl.when(i == 0)
    def _():
        for ahead in range(GATHER_AHEAD):
            def body(r, _):
                for k in range(TOP_K):
                    row_copy(jnp.minimum(ahead, n_steps - 1), r, k, ahead).start()
                return 0
            lax.fori_loop(0, tm, body, 0, unroll=2)

    for slot in range(GATHER_BUFS):
        @pl.when(i % GATHER_BUFS == slot)
        def _():
            wait_rows(slot)
            gate = gate_ref[...]
            ffn = _load_token_tiles(buf.at[slot, 0], tm, chunks) * gate[:, 0:1]
            for k in range(1, TOP_K):
                ffn = ffn + _load_token_tiles(buf.at[slot, k], tm, chunks) * gate[:, k:k + 1]
            o_ref[...] = _layer_norm(alpha * h_ref[...] + ffn, lg_ref[...], lb_ref[...])
            nxt = jnp.minimum(i + GATHER_AHEAD, n_steps - 1)
            for r in range(tm):
                for k in range(TOP_K):
                    row_copy(nxt, r, k, (slot + GATHER_AHEAD) % GATHER_BUFS).start()

        @pl.when((i % GATHER_BUFS == slot) & (i + 1 == n_steps))
        def _():
            for ahead in range(1, GATHER_BUFS):
                wait_rows((slot + ahead) % GATHER_BUFS)


def _combine_call(pos_flat, ys, gates, h2d, ln_g, ln_b, alpha, tm):
    t, d = h2d.shape
    chunks = d // LANES
    row = lambda i, pos: (i, 0)
    const2 = lambda i, pos: (0, 0)
    return pl.pallas_call(
        functools.partial(_combine_kernel, alpha=alpha, tm=tm, d=d),
        out_shape=jax.ShapeDtypeStruct((t, d), F32),
        grid_spec=pltpu.PrefetchScalarGridSpec(
            num_scalar_prefetch=1,
            grid=(t // tm,),
            in_specs=[pl.BlockSpec(memory_space=pl.ANY),
                      pl.BlockSpec((tm, LANES), row),
                      pl.BlockSpec((tm, d), row),
                      pl.BlockSpec((1, d), const2),
                      pl.BlockSpec((1, d), const2)],
            out_specs=pl.BlockSpec((tm, d), row),
            scratch_shapes=[pltpu.VMEM((GATHER_BUFS, TOP_K, tm * chunks, LANES), F32),
                            pltpu.SemaphoreType.DMA((GATHER_BUFS,))]),
        compiler_params=pltpu.CompilerParams(dimension_semantics=("arbitrary",),
                                             vmem_limit_bytes=VMEM_LIMIT),
        name="moe_combine",
    )(pos_flat, ys, gates, h2d, ln_g.reshape(1, d), ln_b.reshape(1, d))


def _in_proj_columns():
    o_cq = 3 * SB_WIDTH
    o_ckv = o_cq + MLA_Q_RANK
    o_kpe = o_ckv + MLA_KV_RANK
    o_ca = o_kpe + MLA_ROPE_DIM
    o_cg = o_ca + CONV_CH
    reps = LANES // 2 // ROPE_HALF
    kpe = np.concatenate([np.tile(o_kpe + np.arange(ROPE_HALF), reps),
                          np.tile(o_kpe + ROPE_HALF + np.arange(ROPE_HALF), reps)])
    return np.concatenate([np.arange(o_cq), o_cq + np.arange(MLA_Q_RANK),
                           o_ckv + np.arange(MLA_KV_RANK), kpe,
                           o_ca + np.arange(CONV_CH), o_cg + np.arange(CONV_CH)])


def _uq_columns():
    per = MLA_NOPE_DIM + MLA_ROPE_DIM
    nope = np.concatenate([h * per + np.arange(MLA_NOPE_DIM) for h in range(MLA_HEADS)])
    rope = []
    grp = LANES // 2 // ROPE_HALF
    for g0 in range(0, MLA_HEADS, grp):
        for half in range(2):
            for h in range(g0, g0 + grp):
                rope.append(h * per + MLA_NOPE_DIM + half * ROPE_HALF + np.arange(ROPE_HALF))
    return np.concatenate([nope] + rope)


def _ukv_columns():
    per = MLA_NOPE_DIM + MLA_V_DIM
    kn = np.concatenate([h * per + np.arange(MLA_NOPE_DIM) for h in range(MLA_HEADS)])
    vv = np.concatenate([h * per + MLA_NOPE_DIM + np.arange(MLA_V_DIM) for h in range(MLA_HEADS)])
    return np.concatenate([kn, vv])


def _rope_tables(seq):
    lp = seq + N_META
    inv = 1.0 / (ROPE_THETA ** (jnp.arange(0, MLA_ROPE_DIM, 2, dtype=F32) / MLA_ROPE_DIM))
    pos = jnp.concatenate([N_META + jnp.arange(seq, dtype=F32), jnp.arange(N_META, dtype=F32)])
    ang = pos[:, None] * inv[None, :]
    reps = LANES // 2 // ROPE_HALF
    cos = jnp.tile(jnp.cos(ang), (1, 2 * reps))
    sin = jnp.tile(jnp.sin(ang), (1, reps))
    assert cos.shape == (lp, LANES)
    return cos, jnp.concatenate([-sin, sin], axis=1)


ATTN_TQ = 512
EXPERT_TM = 512
ROUTE_TM = 128


def kernel(x, meta_tokens, ln_in_g, ln_in_b, w_in, q_norm_g, w_uq, kv_norm_g, w_ukv, conv_w, conv_b,
           conv_ln_g, conv_ln_b, grp_norm_g, w_out, ln_mix_g, ln_mix_b, router_w, router_b,
           w_gate_up, b_gate_up, w_down, b_down, ln_ffn_g, ln_ffn_b):
    b, seq, d = x.shape
    depth = w_in.shape[0]
    lp = seq + N_META
    t = b * lp
    alpha = float((2 * depth) ** 0.25)
    tq = min(ATTN_TQ, seq)
    assert seq % tq == 0 and tq % ATTN_TK == 0 and seq % CONV_ROWS == 0
    tm_tok = _row_tile(t, 512, LANES)
    tm_seq = _row_tile(lp, 1024)
    tm_route = _row_tile(t, ROUTE_TM, 8)
    n_assign = t * TOP_K
    n_tiles = -(-n_assign // EXPERT_TM) + N_EXPERTS + GATHER_AHEAD
    n_rows = n_tiles * EXPERT_TM
    n_chunks = -(-n_assign // INV_CHUNK)

    meta = jnp.broadcast_to(meta_tokens[None].astype(x.dtype), (b, N_META, d))
    xin = jnp.concatenate([x, meta], axis=1).reshape(t, d)
    h = _ln_call(xin, ln_in_g, ln_in_b, tm_tok)
    cos_t, sin_t = _rope_tables(seq)

    in_cols = _in_proj_columns()
    uq_cols, ukv_cols = _uq_columns(), _ukv_columns()

    for l in range(depth):
        w_in_p = w_in[l][:, in_cols].astype(BF16)
        sbqk, sbv, mq, mk, mv, cu = _inproj_call(
            h.reshape(b, lp, d), w_in_p, q_norm_g[l], w_uq[l][:, uq_cols].astype(BF16),
            kv_norm_g[l], w_ukv[l][:, ukv_cols].astype(BF16), cos_t, sin_t, tm_seq)
        sb_out = _sb_call(sbqk, sbv.transpose(0, 2, 1), sbv[:, seq:, :], tq)
        mla_out = _mla_call(mq, mk, mv.transpose(0, 2, 1), mv[:, seq:, :], tq)
        conv_out = _conv_call(cu, conv_w[l], conv_b[l], conv_ln_g[l], conv_ln_b[l])
        h1, h1_tiles, idx, gates, rank, counts = _mix_call(
            sb_out.reshape(t, -1), mla_out.reshape(t, -1), conv_out.reshape(t, -1), h,
            grp_norm_g[l], w_out[l].astype(BF16), ln_mix_g[l], ln_mix_b[l],
            router_w[l].astype(BF16), router_b[l], alpha, tm_tok)

        def per_token(a):
            return a[:, :TOP_K, :].transpose(0, 2, 1).reshape(t, TOP_K)

        idx, rank = per_token(idx), per_token(rank)
        gates = jnp.pad(per_token(gates), ((0, 0), (0, LANES - TOP_K)))
        cnt = counts[:, 0].astype(jnp.int32)
        tiles_e = (cnt + EXPERT_TM - 1) // EXPERT_TM
        tile_end = jnp.cumsum(tiles_e)
        tile_start = tile_end - tiles_e
        pos = tile_start[idx] * EXPERT_TM + rank
        pos_flat = pos.reshape(-1).astype(jnp.int32)
        n_valid = tile_end[-1:].astype(jnp.int32)
        tile_ids = jnp.minimum(jnp.arange(n_tiles, dtype=jnp.int32), n_valid[0] - 1)
        tile_expert = jnp.minimum(jnp.sum(tile_end[None, :] <= tile_ids[:, None], axis=1),
                                  N_EXPERTS - 1).astype(jnp.int32)

        pos_chunks = jnp.concatenate(
            [pos_flat, jnp.full((n_chunks * INV_CHUNK - n_assign,), n_rows, jnp.int32)]
        ).reshape(n_chunks, INV_CHUNK)
        row_tok = _invert_call(cnt, tile_start.astype(jnp.int32), pos_chunks, n_rows + INV_CHUNK,
                               EXPERT_TM)
        experts = jnp.arange(N_EXPERTS, dtype=jnp.int32)
        used = tiles_e > 0
        later_used = used[None, :] & (experts[None, :] > experts[:, None])
        next_e = jnp.min(jnp.where(later_used, experts[None, :], N_EXPERTS), axis=1)
        next_e = jnp.where(next_e == N_EXPERTS, -1, next_e)
        run_e = jnp.cumsum(used) - used
        tile_next = next_e[tile_expert].astype(jnp.int32)
        tile_parity = (run_e[tile_expert] % 2).astype(jnp.int32)
        ys = _expert_call(tile_expert, tile_next, tile_parity, n_valid, row_tok, h1_tiles,
                          w_gate_up, b_gate_up[l], w_down, b_down[l], l, n_tiles, EXPERT_TM)
        h = _combine_call(pos_flat, ys, gates, h1, ln_ffn_g[l], ln_ffn_b[l], alpha, tm_route)

    return h.reshape(b, lp, d)[:, :seq, :]
```

```python
import functools

import jax
import jax.numpy as jnp
import numpy as np
from jax import lax
from jax.experimental import pallas as pl
from jax.experimental.pallas import tpu as pltpu

F32 = jnp.float32
BF16 = jnp.bfloat16

N_META = 16
SB_HEADS = 4
SB_HEAD_DIM = 64
SB_WIDTH = SB_HEADS * SB_HEAD_DIM
MLA_HEADS = 8
MLA_NOPE_DIM = 64
MLA_ROPE_DIM = 32
MLA_V_DIM = 64
MLA_Q_RANK = 256
MLA_KV_RANK = 128
MLA_WIDTH = MLA_HEADS * MLA_V_DIM
ROPE_THETA = 10000.0
CONV_CH = 256
CONV_K = 31
N_EXPERTS = 32
TOP_K = 4
SWIGLU_LIMIT = 7.0
SWIGLU_ALPHA = 1.702
LN_EPS = 1e-5
RMS_EPS = 1e-6

LANES = 128
HEAD_PAIR = LANES // SB_HEAD_DIM
ROPE_HALF = MLA_ROPE_DIM // 2
VMEM_LIMIT = 48 * 1024 * 1024
EXPERT_VMEM_LIMIT = 56 * 1024 * 1024
NEG_BIG = -1e30
LOG2E = 1.4426950408889634
ATTN_TK = 256

IN_SB = 3 * SB_WIDTH
IN_CQ = IN_SB
IN_CKV = IN_CQ + MLA_Q_RANK
IN_KPE = IN_CKV + MLA_KV_RANK
IN_CA = IN_KPE + LANES
IN_CG = IN_CA + CONV_CH
IN_TOTAL = IN_CG + CONV_CH
MQ_W = MLA_HEADS * MLA_NOPE_DIM + 2 * LANES
MK_W = MLA_HEADS * MLA_NOPE_DIM + LANES


def _row_tile(n, cap, mult=16):
    best = None
    for t in range(mult, min(n, cap) + 1, mult):
        if n % t == 0:
            best = t
    assert best is not None, (n, cap, mult)
    return best


def _layer_norm(x, g, b):
    mu = jnp.mean(x, axis=-1, keepdims=True)
    xc = x - mu
    var = jnp.mean(xc * xc, axis=-1, keepdims=True)
    return xc * lax.rsqrt(var + LN_EPS) * g + b


def _rms_norm(x, g):
    return x * lax.rsqrt(jnp.mean(x * x, axis=-1, keepdims=True) + RMS_EPS) * g


def _sigmoid(x):
    return 1.0 / (1.0 + jnp.exp(-x))


def _ln_kernel(x_ref, g_ref, b_ref, o_ref):
    o_ref[...] = _layer_norm(x_ref[...], g_ref[...], b_ref[...])


def _ln_call(x2d, g, b, tm):
    t, d = x2d.shape
    return pl.pallas_call(
        _ln_kernel,
        out_shape=jax.ShapeDtypeStruct((t, d), F32),
        grid=(t // tm,),
        in_specs=[pl.BlockSpec((tm, d), lambda i: (i, 0)),
                  pl.BlockSpec((1, d), lambda i: (0, 0)),
                  pl.BlockSpec((1, d), lambda i: (0, 0))],
        out_specs=pl.BlockSpec((tm, d), lambda i: (i, 0)),
        compiler_params=pltpu.CompilerParams(dimension_semantics=("parallel",),
                                             vmem_limit_bytes=VMEM_LIMIT),
        name="ln_in",
    )(x2d, g.reshape(1, d), b.reshape(1, d))


def _inproj_kernel(h_ref, w_ref, qg_ref, wuq_ref, kvg_ref, wukv_ref, cos_ref, sin_ref,
                   sbqk_ref, sbv_ref, mq_ref, mk_ref, mv_ref, cu_ref):
    h = h_ref[0].astype(BF16)
    proj = jnp.dot(h, w_ref[...], preferred_element_type=F32)
    sbqk_ref[0, :, :SB_WIDTH] = (proj[:, :SB_WIDTH] * (SB_HEAD_DIM ** -0.5 * LOG2E)).astype(BF16)
    sbqk_ref[0, :, SB_WIDTH:] = proj[:, SB_WIDTH:2 * SB_WIDTH].astype(BF16)
    sbv_ref[0] = proj[:, 2 * SB_WIDTH:IN_SB].astype(BF16)
    cos = cos_ref[...]
    sin = sin_ref[...]

    def rot(x):
        return x * cos + pltpu.roll(x, LANES // 2, 1) * sin

    cq = _rms_norm(proj[:, IN_CQ:IN_CKV], qg_ref[...])
    qm = jnp.dot(cq.astype(BF16), wuq_ref[...], preferred_element_type=F32)
    qm = qm * ((MLA_NOPE_DIM + MLA_ROPE_DIM) ** -0.5 * LOG2E)
    nope_w = MLA_HEADS * MLA_NOPE_DIM
    mq_ref[0, :, :nope_w] = qm[:, :nope_w].astype(BF16)
    mq_ref[0, :, nope_w:nope_w + LANES] = rot(qm[:, nope_w:nope_w + LANES]).astype(BF16)
    mq_ref[0, :, nope_w + LANES:] = rot(qm[:, nope_w + LANES:]).astype(BF16)

    ckv = _rms_norm(proj[:, IN_CKV:IN_KPE], kvg_ref[...])
    kv = jnp.dot(ckv.astype(BF16), wukv_ref[...], preferred_element_type=F32)
    mk_ref[0, :, :nope_w] = kv[:, :nope_w].astype(BF16)
    mk_ref[0, :, nope_w:] = rot(proj[:, IN_KPE:IN_CA]).astype(BF16)
    mv_ref[0] = kv[:, nope_w:].astype(BF16)

    cu_ref[0] = proj[:, IN_CA:IN_CG] * _sigmoid(proj[:, IN_CG:IN_TOTAL])


def _inproj_call(h3, w_in_p, qg, wuq_p, kvg, wukv_p, cos_t, sin_t, tm):
    b, lp, d = h3.shape
    nt = lp // tm
    const2 = lambda bi, ti: (0, 0)
    tok3 = lambda bi, ti: (bi, ti, 0)
    outs = (jax.ShapeDtypeStruct((b, lp, 2 * SB_WIDTH), BF16),
            jax.ShapeDtypeStruct((b, lp, SB_WIDTH), BF16),
            jax.ShapeDtypeStruct((b, lp, MQ_W), BF16),
            jax.ShapeDtypeStruct((b, lp, MK_W), BF16),
            jax.ShapeDtypeStruct((b, lp, MLA_WIDTH), BF16),
            jax.ShapeDtypeStruct((b, lp, CONV_CH), F32))
    return pl.pallas_call(
        _inproj_kernel,
        out_shape=outs,
        grid=(b, nt),
        in_specs=[pl.BlockSpec((1, tm, d), tok3),
                  pl.BlockSpec(w_in_p.shape, const2),
                  pl.BlockSpec((1, MLA_Q_RANK), const2),
                  pl.BlockSpec(wuq_p.shape, const2),
                  pl.BlockSpec((1, MLA_KV_RANK), const2),
                  pl.BlockSpec(wukv_p.shape, const2),
                  pl.BlockSpec((tm, LANES), lambda bi, ti: (ti, 0)),
                  pl.BlockSpec((tm, LANES), lambda bi, ti: (ti, 0))],
        out_specs=tuple(pl.BlockSpec((1, tm, s.shape[2]), tok3) for s in outs),
        compiler_params=pltpu.CompilerParams(dimension_semantics=("parallel", "parallel"),
                                             vmem_limit_bytes=VMEM_LIMIT),
        name="in_proj",
    )(h3, w_in_p, qg.reshape(1, -1), wuq_p, kvg.reshape(1, -1), wukv_p, cos_t, sin_t)


def _log_stay_take2(z):
    nz = -z
    soft = jnp.log2(1.0 + jnp.exp2(jnp.minimum(z, nz)))
    return jnp.minimum(nz, 0.0) - soft, jnp.minimum(z, 0.0) - soft


def _sb_kernel(q_ref, k_ref, vt_ref, vmeta_ref, o_ref, acc_ref, c_ref, *, tq, seq):
    i = pl.program_id(1)
    nq = seq // tq
    npair = SB_HEADS // HEAD_PAIR
    lane = lax.broadcasted_iota(jnp.int32, (1, LANES), 1)
    head_sel = (lane < SB_HEAD_DIM, lane >= SB_HEAD_DIM)
    top_rows = lax.broadcasted_iota(jnp.int32, (LANES, 1), 0) < SB_HEAD_DIM
    meta = pl.ds(seq, N_META)

    def iota2(rows, cols):
        return (lax.broadcasted_iota(jnp.int32, (rows, cols), 0),
                lax.broadcasted_iota(jnp.int32, (rows, cols), 1))

    def tri_down(w):
        r, c = iota2(w, w)
        return jnp.where(c > r, 1.0, 0.0).astype(BF16)

    def tile_all(q, krows, mask, ut, first):
        zs = []
        for p in range(npair):
            cols = slice(p * LANES, (p + 1) * LANES)
            kt = k_ref[0, krows, cols]
            for hd in range(HEAD_PAIR):
                qh = jnp.where(head_sel[hd], q[:, cols], jnp.zeros((1, 1), BF16))
                zs.append(lax.dot_general(kt, qh, (((1,), (1,)), ((), ())),
                                          preferred_element_type=F32))
        stays, takes = [], []
        for h in range(SB_HEADS):
            ls, lt = _log_stay_take2(zs[h])
            if mask is not None:
                ls = jnp.where(mask, ls, 0.0)
            stays.append(ls.astype(BF16))
            takes.append(lt)
        newer = [jnp.dot(ut, st, preferred_element_type=F32) for st in stays]
        ws = []
        for h in range(SB_HEADS):
            lw = takes[h] + newer[h] if first else takes[h] + newer[h] + c_ref[h]
            w = jnp.exp2(lw)
            if mask is not None:
                w = jnp.where(mask, w, 0.0)
            ws.append(w.astype(BF16))
            total = newer[h][0:1, :] + stays[h][0:1, :].astype(F32)
            c_ref[h] = total if first else c_ref[h] + total
        for p in range(npair):
            vt = vt_ref[0, p * LANES:(p + 1) * LANES, krows]
            res = [jnp.dot(vt, ws[p * HEAD_PAIR + hd], preferred_element_type=F32)
                   for hd in range(HEAD_PAIR)]
            contrib = jnp.where(top_rows, res[0], res[1])
            acc_ref[p] = contrib if first else acc_ref[p] + contrib

    @pl.when(i < nq)
    def _():
        q = q_ref[0]
        ut = tri_down(ATTN_TK)
        r, c = iota2(ATTN_TK, tq)
        per_q = tq // ATTN_TK
        for d in range(per_q - 1, -1, -1):
            tile_all(q, pl.ds(pl.multiple_of(i * tq + d * ATTN_TK, ATTN_TK), ATTN_TK),
                     r + d * ATTN_TK < c, ut, d == per_q - 1)

        def body(t, carry):
            start = pl.multiple_of((i * per_q - 1 - t) * ATTN_TK, ATTN_TK)
            tile_all(q, pl.ds(start, ATTN_TK), None, ut, False)
            return carry

        lax.fori_loop(0, i * per_q, body, 0)
        tile_all(q, meta, None, tri_down(N_META), False)
        for p in range(npair):
            o_ref[0, :, p * LANES:(p + 1) * LANES] = acc_ref[p].T.astype(o_ref.dtype)

    @pl.when(i == nq)
    def _():
        r, c = iota2(N_META, N_META)
        mask = c < r
        ut = jnp.where(r > c, 1.0, 0.0).astype(BF16)
        for p in range(npair):
            cols = slice(p * LANES, (p + 1) * LANES)
            q = q_ref[0, 0:N_META, cols]
            kt = k_ref[0, meta, cols]
            vm = vmeta_ref[0, :, cols]
            res = []
            for hd in range(HEAD_PAIR):
                qh = jnp.where(head_sel[hd], q, jnp.zeros((1, 1), BF16))
                z = lax.dot_general(qh, kt, (((1,), (1,)), ((), ())), preferred_element_type=F32)
                ls, lt = _log_stay_take2(z)
                newer = jnp.dot(jnp.where(mask, ls, 0.0).astype(BF16), ut, preferred_element_type=F32)
                w = jnp.where(mask, jnp.exp2(lt + newer), 0.0)
                res.append(jnp.dot(w.astype(BF16), vm, preferred_element_type=F32))
            o_ref[0, 0:N_META, cols] = jnp.where(head_sel[0], res[0], res[1]).astype(o_ref.dtype)


def _sb_call(sbqk, sbv_t, sbv_meta, tq):
    b, lp, _ = sbqk.shape
    seq = lp - N_META
    nq = seq // tq
    return pl.pallas_call(
        functools.partial(_sb_kernel, tq=tq, seq=seq),
        out_shape=jax.ShapeDtypeStruct((b, lp, SB_WIDTH), BF16),
        grid=(b, nq + 1),
        in_specs=[pl.BlockSpec((1, tq, SB_WIDTH), lambda bi, i: (bi, i, 0)),
                  pl.BlockSpec((1, lp, SB_WIDTH), lambda bi, i: (bi, 0, 1)),
                  pl.BlockSpec((1, SB_WIDTH, lp), lambda bi, i: (bi, 0, 0)),
                  pl.BlockSpec((1, N_META, SB_WIDTH), lambda bi, i: (bi, 0, 0))],
        out_specs=pl.BlockSpec((1, tq, SB_WIDTH), lambda bi, i: (bi, i, 0)),
        scratch_shapes=[pltpu.VMEM((SB_HEADS // HEAD_PAIR, LANES, tq), F32),
                        pltpu.VMEM((SB_HEADS, 1, tq), F32)],
        compiler_params=pltpu.CompilerParams(
            dimension_semantics=("parallel", "arbitrary"),
            vmem_limit_bytes=VMEM_LIMIT),
        name="sb_attn",
    )(sbqk, sbqk, sbv_t, sbv_meta)


def _mla_kernel(q_ref, k_ref, vt_ref, vmeta_ref, o_ref, acc_ref, m_ref, *, tq, seq):
    i = pl.program_id(1)
    nq = seq // tq
    npair = MLA_HEADS // HEAD_PAIR
    nope_w = MLA_HEADS * MLA_NOPE_DIM
    grp = LANES // 2 // ROPE_HALF
    lane = lax.broadcasted_iota(jnp.int32, (1, LANES), 1)
    lane2 = lax.broadcasted_iota(jnp.int32, (1, 2 * LANES), 1)
    top_rows = lax.broadcasted_iota(jnp.int32, (LANES, 1), 0) < MLA_V_DIM
    meta = pl.ds(seq, N_META)
    one = jnp.ones((1, 1), BF16)

    def head_lanes(h):
        n0 = (h % HEAD_PAIR) * MLA_NOPE_DIM
        r0 = LANES + (h % grp) * ROPE_HALF
        r1 = r0 + LANES // 2
        return (((lane2 >= n0) & (lane2 < n0 + MLA_NOPE_DIM))
                | ((lane2 >= r0) & (lane2 < r0 + ROPE_HALF))
                | ((lane2 >= r1) & (lane2 < r1 + ROPE_HALF)))

    def qcat(q, p):
        rope0 = nope_w + (p * HEAD_PAIR // grp) * LANES
        return jnp.concatenate([q[:, p * LANES:(p + 1) * LANES], q[:, rope0:rope0 + LANES]], axis=-1)

    def kcat(krows, p):
        return jnp.concatenate([k_ref[0, krows, p * LANES:(p + 1) * LANES],
                                k_ref[0, krows, nope_w:nope_w + LANES]], axis=-1)

    def iota2(rows, cols):
        return (lax.broadcasted_iota(jnp.int32, (rows, cols), 0),
                lax.broadcasted_iota(jnp.int32, (rows, cols), 1))

    def tile_all(q, krows, mask, first):
        scores = []
        for p in range(npair):
            kc = kcat(krows, p)
            qc = qcat(q, p)
            for hd in range(HEAD_PAIR):
                qh = jnp.where(head_lanes(p * HEAD_PAIR + hd), qc, jnp.zeros((1, 1), BF16))
                scores.append(lax.dot_general(kc, qh, (((1,), (1,)), ((), ())),
                                              preferred_element_type=F32))
        pexps, alphas = [], []
        for h in range(MLA_HEADS):
            s = scores[h] if mask is None else jnp.where(mask, scores[h], NEG_BIG)
            mx = jnp.max(s, axis=0, keepdims=True)
            if first:
                m_new = mx
                alphas.append(None)
            else:
                m_old = m_ref[h]
                m_new = jnp.maximum(m_old, mx)
                alphas.append(jnp.exp2(m_old - m_new))
            m_ref[h] = m_new
            pexps.append(jnp.exp2(s - m_new).astype(BF16))
        for h in range(MLA_HEADS):
            p, hd = divmod(h, HEAD_PAIR)
            vt = vt_ref[0, p * LANES:(p + 1) * LANES, krows]
            vte = jnp.where(top_rows, vt, one) if hd == 0 else jnp.where(top_rows, one, vt)
            pv = jnp.dot(vte, pexps[h], preferred_element_type=F32)
            acc_ref[h] = pv if first else acc_ref[h] * alphas[h] + pv

    @pl.when(i < nq)
    def _():
        q = q_ref[0]
        r, c = iota2(ATTN_TK, tq)
        per_q = tq // ATTN_TK
        for d in range(per_q):
            tile_all(q, pl.ds(pl.multiple_of(i * tq + d * ATTN_TK, ATTN_TK), ATTN_TK),
                     r + d * ATTN_TK <= c, d == 0)

        def body(t, carry):
            tile_all(q, pl.ds(pl.multiple_of(t * ATTN_TK, ATTN_TK), ATTN_TK), None, False)
            return carry

        lax.fori_loop(0, i * per_q, body, 0)
        tile_all(q, meta, None, False)
        for p in range(npair):
            a = acc_ref[p * HEAD_PAIR]
            b = acc_ref[p * HEAD_PAIR + 1]
            out_t = jnp.where(top_rows, a * (1.0 / a[MLA_V_DIM:MLA_V_DIM + 1, :]), b * (1.0 / b[0:1, :]))
            o_ref[0, :, p * LANES:(p + 1) * LANES] = out_t.T.astype(o_ref.dtype)

    @pl.when(i == nq)
    def _():
        q = q_ref[0, 0:N_META, :]
        r, c = iota2(N_META, N_META)
        for p in range(npair):
            kc = kcat(meta, p)
            qc = qcat(q, p)
            vm = vmeta_ref[0, :, p * LANES:(p + 1) * LANES]
            res = []
            for hd in range(HEAD_PAIR):
                qh = jnp.where(head_lanes(p * HEAD_PAIR + hd), qc, jnp.zeros((1, 1), BF16))
                s = lax.dot_general(qh, kc, (((1,), (1,)), ((), ())), preferred_element_type=F32)
                s = jnp.where(c <= r, s, NEG_BIG)
                pexp = jnp.exp2(s - jnp.max(s, axis=1, keepdims=True))
                pv = jnp.dot(pexp.astype(BF16), vm, preferred_element_type=F32)
                res.append(pv * (1.0 / jnp.sum(pexp, axis=1, keepdims=True)))
            o_ref[0, 0:N_META, p * LANES:(p + 1) * LANES] = jnp.where(
                lane < MLA_V_DIM, res[0], res[1]).astype(o_ref.dtype)


def _mla_call(mq, mk, mv_t, mv_meta, tq):
    b, lp, _ = mq.shape
    seq = lp - N_META
    nq = seq // tq
    return pl.pallas_call(
        functools.partial(_mla_kernel, tq=tq, seq=seq),
        out_shape=jax.ShapeDtypeStruct((b, lp, MLA_WIDTH), BF16),
        grid=(b, nq + 1),
        in_specs=[pl.BlockSpec((1, tq, MQ_W), lambda bi, i: (bi, i, 0)),
                  pl.BlockSpec((1, lp, MK_W), lambda bi, i: (bi, 0, 0)),
                  pl.BlockSpec((1, MLA_WIDTH, lp), lambda bi, i: (bi, 0, 0)),
                  pl.BlockSpec((1, N_META, MLA_WIDTH), lambda bi, i: (bi, 0, 0))],
        out_specs=pl.BlockSpec((1, tq, MLA_WIDTH), lambda bi, i: (bi, i, 0)),
        scratch_shapes=[pltpu.VMEM((MLA_HEADS, LANES, tq), F32),
                        pltpu.VMEM((MLA_HEADS, 1, tq), F32)],
        compiler_params=pltpu.CompilerParams(
            dimension_semantics=("parallel", "arbitrary"),
            vmem_limit_bytes=VMEM_LIMIT),
        name="mla_attn",
    )(mq, mk, mv_t, mv_meta)


CONV_PAD = 32
CONV_ROWS = 128
SUBLANES = 8
CONV_WIN_EXTRA = CONV_PAD


def _conv_kernel(u_ref, w_ref, b_ref, g_ref, beta_ref, o_ref, buf_ref, *, seq):
    buf_ref[0:CONV_PAD, :] = jnp.zeros((CONV_PAD, CONV_CH), F32)
    buf_ref[CONV_PAD:CONV_PAD + N_META, :] = u_ref[0, seq:seq + N_META, :]
    buf_ref[CONV_PAD + N_META:CONV_PAD + N_META + seq, :] = u_ref[0, 0:seq, :]
    tail = CONV_PAD + N_META + seq
    buf_ref[tail:tail + SUBLANES, :] = jnp.zeros((SUBLANES, CONV_CH), F32)
    w = w_ref[...]
    lead = CONV_PAD - (CONV_K - 1)

    def finish(acc):
        y = _layer_norm(acc + b_ref[...], g_ref[...], beta_ref[...])
        return (y * _sigmoid(y)).astype(o_ref.dtype)

    def conv_rows(first_pos, rows):
        win = buf_ref[pl.ds(first_pos, rows + CONV_WIN_EXTRA + SUBLANES), :]
        acc = jnp.zeros((rows, CONV_CH), F32)
        for sh in range(SUBLANES):
            part = None
            for o in range(lead, lead + CONV_K):
                if o % SUBLANES != sh:
                    continue
                k = o - lead
                term = win[o - sh:o - sh + rows + SUBLANES, :] * w[k:k + 1, :]
                part = term if part is None else part + term
            acc = acc + part[sh:sh + rows, :]
        return finish(acc)

    def body(c, _):
        r0 = pl.multiple_of(c * CONV_ROWS, CONV_ROWS)
        o_ref[0, pl.ds(r0, CONV_ROWS), :] = conv_rows(pl.multiple_of(r0 + N_META, SUBLANES), CONV_ROWS)
        return 0

    lax.fori_loop(0, seq // CONV_ROWS, body, 0)
    o_ref[0, seq:seq + N_META, :] = conv_rows(0, N_META)


def _conv_call(cu, conv_w, conv_b, ln_g, ln_b):
    b, lp, _ = cu.shape
    seq = lp - N_META
    const2 = lambda bi: (0, 0)
    return pl.pallas_call(
        functools.partial(_conv_kernel, seq=seq),
        out_shape=jax.ShapeDtypeStruct((b, lp, CONV_CH), BF16),
        grid=(b,),
        in_specs=[pl.BlockSpec((1, lp, CONV_CH), lambda bi: (bi, 0, 0)),
                  pl.BlockSpec((CONV_K, CONV_CH), const2),
                  pl.BlockSpec((1, CONV_CH), const2),
                  pl.BlockSpec((1, CONV_CH), const2),
                  pl.BlockSpec((1, CONV_CH), const2)],
        out_specs=pl.BlockSpec((1, lp, CONV_CH), lambda bi: (bi, 0, 0)),
        scratch_shapes=[pltpu.VMEM((CONV_PAD + lp + SUBLANES, CONV_CH), F32)],
        compiler_params=pltpu.CompilerParams(dimension_semantics=("parallel",),
                                             vmem_limit_bytes=VMEM_LIMIT),
        name="conv",
    )(cu, conv_w, conv_b.reshape(1, -1), ln_g.reshape(1, -1), ln_b.reshape(1, -1))


ROUTE_ROWS = 8


def _mix_kernel(sb_ref, mla_ref, cv_ref, h_ref, gg_ref, wo_ref, lg_ref, lb_ref, rw_ref, rb_ref,
                h1_ref, h1t_ref, idx_ref, gate_ref, rank_ref, cnt_ref, carry_ref, *, alpha, tm):
    step = pl.program_id(0)

    @pl.when(step == 0)
    def _():
        carry_ref[...] = jnp.zeros_like(carry_ref)

    gg = gg_ref[...]
    y = jnp.concatenate(
        [_rms_norm(sb_ref[...].astype(F32), gg[:, :SB_WIDTH]),
         _rms_norm(mla_ref[...].astype(F32), gg[:, SB_WIDTH:SB_WIDTH + MLA_WIDTH]),
         _rms_norm(cv_ref[...].astype(F32), gg[:, SB_WIDTH + MLA_WIDTH:])], axis=-1)
    mix = jnp.dot(y.astype(BF16), wo_ref[...], preferred_element_type=F32)
    h1 = _layer_norm(alpha * h_ref[...] + mix, lg_ref[...], lb_ref[...])
    h1_ref[...] = h1
    _store_token_tiles(h1t_ref, h1)

    logits = jnp.dot(h1.astype(BF16), rw_ref[...], preferred_element_type=F32) + rb_ref[...]
    vals = logits.T[:N_EXPERTS, :]
    eiota = lax.broadcasted_iota(jnp.int32, (N_EXPERTS, 1), 0).astype(F32)
    sels, tops, idxs = [], [], []
    for _ in range(TOP_K):
        m = jnp.max(vals, axis=0, keepdims=True)
        idx = jnp.min(jnp.where(vals == m, eiota, float(N_EXPERTS)), axis=0, keepdims=True)
        sel = eiota == idx
        vals = jnp.where(sel, -jnp.inf, vals)
        sels.append(sel)
        tops.append(m)
        idxs.append(idx)
    exps = [jnp.exp(t - tops[0]) for t in tops]
    denom = exps[0] + exps[1] + exps[2] + exps[3]

    chosen = jnp.zeros((N_EXPERTS, tm), F32)
    for sel in sels:
        chosen = chosen + jnp.where(sel, 1.0, 0.0)
    r = lax.broadcasted_iota(jnp.int32, (tm, tm), 0)
    c = lax.broadcasted_iota(jnp.int32, (tm, tm), 1)
    before = jnp.where(r < c, 1.0, 0.0).astype(BF16)
    earlier = jnp.dot(chosen.astype(BF16), before, preferred_element_type=F32) + carry_ref[...]

    row = lax.broadcasted_iota(jnp.int32, (ROUTE_ROWS, 1), 0)
    idx_out = jnp.zeros((ROUTE_ROWS, tm), jnp.int32)
    gate_out = jnp.zeros((ROUTE_ROWS, tm), F32)
    rank_out = jnp.zeros((ROUTE_ROWS, tm), jnp.int32)
    for k in range(TOP_K):
        rank_k = jnp.sum(jnp.where(sels[k], earlier, 0.0), axis=0, keepdims=True)
        idx_out = jnp.where(row == k, idxs[k].astype(jnp.int32), idx_out)
        gate_out = jnp.where(row == k, exps[k] / denom, gate_out)
        rank_out = jnp.where(row == k, rank_k.astype(jnp.int32), rank_out)
    idx_ref[0] = idx_out
    gate_ref[0] = gate_out
    rank_ref[0] = rank_out

    carry_ref[...] = carry_ref[...] + jnp.sum(chosen, axis=1, keepdims=True)
    cnt_ref[...] = carry_ref[...]


def _mix_call(sb_out, mla_out, conv_out, h2d, grp_g, w_out_b, ln_g, ln_b, router_w_b, router_b,
              alpha, tm):
    t, d = h2d.shape
    row = lambda i: (i, 0)
    const2 = lambda i: (0, 0)
    chunks = d // LANES
    steps = t // tm
    route = lambda i: (i, 0, 0)
    outs = (jax.ShapeDtypeStruct((t, d), F32),
            jax.ShapeDtypeStruct((t * chunks, LANES), F32),
            jax.ShapeDtypeStruct((steps, ROUTE_ROWS, tm), jnp.int32),
            jax.ShapeDtypeStruct((steps, ROUTE_ROWS, tm), F32),
            jax.ShapeDtypeStruct((steps, ROUTE_ROWS, tm), jnp.int32),
            jax.ShapeDtypeStruct((N_EXPERTS, 1), F32))
    rw_pad = jnp.pad(router_w_b, ((0, 0), (0, LANES - N_EXPERTS)))
    rb_pad = jnp.pad(router_b.reshape(1, N_EXPERTS), ((0, 0), (0, LANES - N_EXPERTS)))
    return pl.pallas_call(
        functools.partial(_mix_kernel, alpha=alpha, tm=tm),
        out_shape=outs,
        grid=(t // tm,),
        in_specs=[pl.BlockSpec((tm, SB_WIDTH), row),
                  pl.BlockSpec((tm, MLA_WIDTH), row),
                  pl.BlockSpec((tm, CONV_CH), row),
                  pl.BlockSpec((tm, d), row),
                  pl.BlockSpec((1, d), const2),
                  pl.BlockSpec((d, d), const2),
                  pl.BlockSpec((1, d), const2),
                  pl.BlockSpec((1, d), const2),
                  pl.BlockSpec((d, LANES), const2),
                  pl.BlockSpec((1, LANES), const2)],
        out_specs=(pl.BlockSpec((tm, d), row),
                   pl.BlockSpec((tm * chunks, LANES), row),
                   pl.BlockSpec((1, ROUTE_ROWS, tm), route),
                   pl.BlockSpec((1, ROUTE_ROWS, tm), route),
                   pl.BlockSpec((1, ROUTE_ROWS, tm), route),
                   pl.BlockSpec((N_EXPERTS, 1), const2)),
        scratch_shapes=[pltpu.VMEM((N_EXPERTS, 1), F32)],
        compiler_params=pltpu.CompilerParams(dimension_semantics=("arbitrary",),
                                             vmem_limit_bytes=VMEM_LIMIT),
        name="mix_router",
    )(sb_out, mla_out, conv_out, h2d, grp_g.reshape(1, d), w_out_b, ln_g.reshape(1, d),
      ln_b.reshape(1, d), rw_pad, rb_pad)


GATHER_AHEAD = 2
GATHER_BUFS = GATHER_AHEAD + 1


def _store_token_tiles(ref, val):
    tm, d = val.shape
    chunks = d // LANES
    for c in range(chunks):
        ref[pl.ds(c, tm, stride=chunks), :] = val[:, c * LANES:(c + 1) * LANES]


def _load_token_tiles(ref, tm, chunks):
    return jnp.concatenate([ref[pl.ds(c, tm, stride=chunks), :] for c in range(chunks)], axis=-1)


INV_CHUNK = 1024


def _invert_kernel(cnt_ref, start_ref, pos_hbm, tok_ref, chunk_ref, sem, *, n_chunks, n_out, tm):
    def fill(lo, hi):
        def body(r, _):
            tok_ref[r] = 0
            return 0
        lax.fori_loop(lo, hi, body, 0)

    def pad_rows(e, used_end):
        tiles_e = (cnt_ref[e] + (tm - 1)) // tm
        fill(start_ref[e] * tm + cnt_ref[e], (start_ref[e] + tiles_e) * tm)
        return jnp.maximum(used_end, (start_ref[e] + tiles_e) * tm)

    used_end = lax.fori_loop(0, N_EXPERTS, pad_rows, 0)
    fill(used_end, n_out)

    def chunk_copy(ch, slot):
        return pltpu.make_async_copy(pos_hbm.at[ch], chunk_ref.at[pl.ds(slot * INV_CHUNK, INV_CHUNK)],
                                     sem.at[slot])

    chunk_copy(0, 0).start()
    toks = INV_CHUNK // TOP_K

    def chunk(ch, _):
        slot = ch % 2
        chunk_copy(ch, slot).wait()

        @pl.when(ch + 1 < n_chunks)
        def _():
            chunk_copy(ch + 1, 1 - slot).start()

        first = slot * INV_CHUNK

        def body(tl, _):
            for k in range(TOP_K):
                tok_ref[chunk_ref[first + tl * TOP_K + k]] = ch * toks + tl
            return 0

        lax.fori_loop(0, toks, body, 0, unroll=16)
        return 0

    lax.fori_loop(0, n_chunks, chunk, 0)


def _invert_call(cnt, tile_start, pos_chunks, n_out, tm):
    n_chunks = pos_chunks.shape[0]
    return pl.pallas_call(
        functools.partial(_invert_kernel, n_chunks=n_chunks, n_out=n_out, tm=tm),
        out_shape=jax.ShapeDtypeStruct((n_out,), jnp.int32),
        grid_spec=pltpu.PrefetchScalarGridSpec(
            num_scalar_prefetch=2,
            grid=(1,),
            in_specs=[pl.BlockSpec(memory_space=pl.ANY)],
            out_specs=pl.BlockSpec(memory_space=pltpu.SMEM),
            scratch_shapes=[pltpu.SMEM((2 * INV_CHUNK,), jnp.int32),
                            pltpu.SemaphoreType.DMA((2,))]),
        compiler_params=pltpu.CompilerParams(dimension_semantics=("arbitrary",)),
        name="moe_row_table",
    )(cnt, tile_start, pos_chunks)


def _expert_kernel(te_ref, nxt_ref, par_ref, nv_ref, tok_ref, h_hbm, wgu_hbm, bgu_ref, wd_hbm,
                   bd_ref, y_ref, xbuf, wgu_f32, wd_f32, wgu_bf, wd_bf, sem, wsem,
                   *, layer, tm, d, d_ff):
    i = pl.program_id(0)
    nv = nv_ref[0]
    chunks = d // LANES

    def weight_copies(e, wslot):
        return (pltpu.make_async_copy(wgu_hbm.at[layer, e], wgu_f32.at[wslot], wsem.at[0, wslot]),
                pltpu.make_async_copy(wd_hbm.at[layer, e], wd_f32.at[wslot], wsem.at[1, wslot]))

    def row_copy(tile, r, slot):
        tok = tok_ref[tile * tm + r]
        return pltpu.make_async_copy(
            h_hbm.at[pl.ds(pl.multiple_of(tok * chunks, chunks), chunks), :],
            xbuf.at[slot, pl.ds(r * chunks, chunks), :], sem.at[slot])

    def wait_gather(slot):
        pltpu.make_async_copy(h_hbm.at[pl.ds(0, tm * chunks), :], xbuf.at[slot], sem.at[slot]).wait()

    @pl.when(i == 0)
    def _():
        for ahead in range(GATHER_AHEAD):
            def body(r, _):
                row_copy(ahead, r, ahead).start()
                return 0
            lax.fori_loop(0, tm, body, 0, unroll=8)
        for cp in weight_copies(te_ref[0], par_ref[0]):
            cp.start()

    @pl.when((i < nv) & ((i == 0) | (te_ref[i] != te_ref[jnp.maximum(i - 1, 0)])))
    def _():
        wslot = par_ref[i]
        for cp in weight_copies(te_ref[i], wslot):
            cp.wait()
        wgu_bf[...] = wgu_f32[wslot].astype(BF16)
        wd_bf[...] = wd_f32[wslot].astype(BF16)

        @pl.when(nxt_ref[i] >= 0)
        def _():
            for cp in weight_copies(nxt_ref[i], 1 - wslot):
                cp.start()

    for slot in range(GATHER_BUFS):
        @pl.when((i < nv) & (i % GATHER_BUFS == slot))
        def _():
            wait_gather(slot)
            x = _load_token_tiles(xbuf.at[slot], tm, chunks).astype(BF16)
            gu = jnp.dot(x, wgu_bf[...], preferred_element_type=F32) + bgu_ref[0]
            g = jnp.minimum(gu[:, :d_ff], SWIGLU_LIMIT)
            up = jnp.clip(gu[:, d_ff:], -SWIGLU_LIMIT, SWIGLU_LIMIT)
            act = (up + 1.0) * (g * _sigmoid(SWIGLU_ALPHA * g))
            y = jnp.dot(act.astype(BF16), wd_bf[...], preferred_element_type=F32) + bd_ref[0]
            _store_token_tiles(y_ref, y)
            for r in range(tm):
                row_copy(i + GATHER_AHEAD, r, (slot + GATHER_AHEAD) % GATHER_BUFS).start()

        @pl.when((i >= nv) & (i < nv + GATHER_AHEAD) & (i % GATHER_BUFS == slot))
        def _():
            wait_gather(slot)

    @pl.when(i >= nv)
    def _():
        y_ref[...] = jnp.zeros_like(y_ref)


def _expert_call(tile_expert, tile_next, tile_parity, n_valid, row_tok, h_tiles, wgu_all, bgu,
                 wd_all, bd, layer, n_tiles, tm):
    d_ff, d = wd_all.shape[2:]
    chunks = d // LANES
    bsel = lambda i, te, nxt, par, nv, tok: (te[i], 0, 0)
    return pl.pallas_call(
        functools.partial(_expert_kernel, layer=layer, tm=tm, d=d, d_ff=d_ff),
        out_shape=jax.ShapeDtypeStruct((n_tiles * tm * chunks, LANES), F32),
        grid_spec=pltpu.PrefetchScalarGridSpec(
            num_scalar_prefetch=5,
            grid=(n_tiles,),
            in_specs=[pl.BlockSpec(memory_space=pl.ANY),
                      pl.BlockSpec(memory_space=pl.ANY),
                      pl.BlockSpec((1, 1, 2 * d_ff), bsel),
                      pl.BlockSpec(memory_space=pl.ANY),
                      pl.BlockSpec((1, 1, d), bsel)],
            out_specs=pl.BlockSpec((tm * chunks, LANES), lambda i, te, nxt, par, nv, tok: (i, 0)),
            scratch_shapes=[pltpu.VMEM((GATHER_BUFS, tm * chunks, LANES), F32),
                            pltpu.VMEM((2, d, 2 * d_ff), F32),
                            pltpu.VMEM((2, d_ff, d), F32),
                            pltpu.VMEM((d, 2 * d_ff), BF16),
                            pltpu.VMEM((d_ff, d), BF16),
                            pltpu.SemaphoreType.DMA((GATHER_BUFS,)),
                            pltpu.SemaphoreType.DMA((2, 2))]),
        compiler_params=pltpu.CompilerParams(dimension_semantics=("arbitrary",),
                                             vmem_limit_bytes=EXPERT_VMEM_LIMIT),
        name="moe_experts",
    )(tile_expert, tile_next, tile_parity, n_valid, row_tok, h_tiles, wgu_all,
      bgu.reshape(N_EXPERTS, 1, -1), wd_all, bd.reshape(N_EXPERTS, 1, -1))


def _combine_kernel(pos_ref, ys_hbm, gate_ref, h_ref, lg_ref, lb_ref, o_ref, buf, sem,
                    *, alpha, tm, d):
    i = pl.program_id(0)
    chunks = d // LANES

    n_steps = pl.num_programs(0)

    def row_copy(step, r, k, slot):
        src = pos_ref[step * (tm * TOP_K) + r * TOP_K + k]
        return pltpu.make_async_copy(
            ys_hbm.at[pl.ds(pl.multiple_of(src * chunks, chunks), chunks), :],
            buf.at[slot, k, pl.ds(r * chunks, chunks), :], sem.at[slot])

    def wait_rows(slot):
        for k in range(TOP_K):
            pltpu.make_async_copy(ys_hbm.at[pl.ds(0, tm * chunks), :], buf.at[slot, k],
                                  sem.at[slot]).wait()

    @pl.when(i == 0)
    def _():
        for ahead in range(GATHER_AHEAD):
            def body(r, _):
                for k in range(TOP_K):
                    row_copy(jnp.minimum(ahead, n_steps - 1), r, k, ahead).start()
                return 0
            lax.fori_loop(0, tm, body, 0, unroll=2)

    for slot in range(GATHER_BUFS):
        @pl.when(i % GATHER_BUFS == slot)
        def _():
            wait_rows(slot)
            gate = gate_ref[...]
            ffn = _load_token_tiles(buf.at[slot, 0], tm, chunks) * gate[:, 0:1]
            for k in range(1, TOP_K):
                ffn = ffn + _load_token_tiles(buf.at[slot, k], tm, chunks) * gate[:, k:k + 1]
            o_ref[...] = _layer_norm(alpha * h_ref[...] + ffn, lg_ref[...], lb_ref[...])
            nxt = jnp.minimum(i + GATHER_AHEAD, n_steps - 1)
            for r in range(tm):
                for k in range(TOP_K):
                    row_copy(nxt, r, k, (slot + GATHER_AHEAD) % GATHER_BUFS).start()

        @pl.when((i % GATHER_BUFS == slot) & (i + 1 == n_steps))
        def _():
            for ahead in range(1, GATHER_BUFS):
                wait_rows((slot + ahead) % GATHER_BUFS)


def _combine_call(pos_flat, ys, gates, h2d, ln_g, ln_b, alpha, tm):
    t, d = h2d.shape
    chunks = d // LANES
    row = lambda i, pos: (i, 0)
    const2 = lambda i, pos: (0, 0)
    return pl.pallas_call(
        functools.partial(_combine_kernel, alpha=alpha, tm=tm, d=d),
        out_shape=jax.ShapeDtypeStruct((t, d), F32),
        grid_spec=pltpu.PrefetchScalarGridSpec(
            num_scalar_prefetch=1,
            grid=(t // tm,),
            in_specs=[pl.BlockSpec(memory_space=pl.ANY),
                      pl.BlockSpec((tm, LANES), row),
                      pl.BlockSpec((tm, d), row),
                      pl.BlockSpec((1, d), const2),
                      pl.BlockSpec((1, d), const2)],
            out_specs=pl.BlockSpec((tm, d), row),
            scratch_shapes=[pltpu.VMEM((GATHER_BUFS, TOP_K, tm * chunks, LANES), F32),
                            pltpu.SemaphoreType.DMA((GATHER_BUFS,))]),
        compiler_params=pltpu.CompilerParams(dimension_semantics=("arbitrary",),
                                             vmem_limit_bytes=VMEM_LIMIT),
        name="moe_combine",
    )(pos_flat, ys, gates, h2d, ln_g.reshape(1, d), ln_b.reshape(1, d))


def _in_proj_columns():
    o_cq = 3 * SB_WIDTH
    o_ckv = o_cq + MLA_Q_RANK
    o_kpe = o_ckv + MLA_KV_RANK
    o_ca = o_kpe + MLA_ROPE_DIM
    o_cg = o_ca + CONV_CH
    reps = LANES // 2 // ROPE_HALF
    kpe = np.concatenate([np.tile(o_kpe + np.arange(ROPE_HALF), reps),
                          np.tile(o_kpe + ROPE_HALF + np.arange(ROPE_HALF), reps)])
    return np.concatenate([np.arange(o_cq), o_cq + np.arange(MLA_Q_RANK),
                           o_ckv + np.arange(MLA_KV_RANK), kpe,
                           o_ca + np.arange(CONV_CH), o_cg + np.arange(CONV_CH)])


def _uq_columns():
    per = MLA_NOPE_DIM + MLA_ROPE_DIM
    nope = np.concatenate([h * per + np.arange(MLA_NOPE_DIM) for h in range(MLA_HEADS)])
    rope = []
    grp = LANES // 2 // ROPE_HALF
    for g0 in range(0, MLA_HEADS, grp):
        for half in range(2):
            for h in range(g0, g0 + grp):
                rope.append(h * per + MLA_NOPE_DIM + half * ROPE_HALF + np.arange(ROPE_HALF))
    return np.concatenate([nope] + rope)


def _ukv_columns():
    per = MLA_NOPE_DIM + MLA_V_DIM
    kn = np.concatenate([h * per + np.arange(MLA_NOPE_DIM) for h in range(MLA_HEADS)])
    vv = np.concatenate([h * per + MLA_NOPE_DIM + np.arange(MLA_V_DIM) for h in range(MLA_HEADS)])
    return np.concatenate([kn, vv])


def _rope_tables(seq):
    lp = seq + N_META
    inv = 1.0 / (ROPE_THETA ** (jnp.arange(0, MLA_ROPE_DIM, 2, dtype=F32) / MLA_ROPE_DIM))
    pos = jnp.concatenate([N_META + jnp.arange(seq, dtype=F32), jnp.arange(N_META, dtype=F32)])
    ang = pos[:, None] * inv[None, :]
    reps = LANES // 2 // ROPE_HALF
    cos = jnp.tile(jnp.cos(ang), (1, 2 * reps))
    sin = jnp.tile(jnp.sin(ang), (1, reps))
    assert cos.shape == (lp, LANES)
    return cos, jnp.concatenate([-sin, sin], axis=1)


ATTN_TQ = 512
EXPERT_TM = 256
ROUTE_TM = 128


def kernel(x, meta_tokens, ln_in_g, ln_in_b, w_in, q_norm_g, w_uq, kv_norm_g, w_ukv, conv_w, conv_b,
           conv_ln_g, conv_ln_b, grp_norm_g, w_out, ln_mix_g, ln_mix_b, router_w, router_b,
           w_gate_up, b_gate_up, w_down, b_down, ln_ffn_g, ln_ffn_b):
    b, seq, d = x.shape
    depth = w_in.shape[0]
    lp = seq + N_META
    t = b * lp
    alpha = float((2 * depth) ** 0.25)
    tq = min(ATTN_TQ, seq)
    assert seq % tq == 0 and tq % ATTN_TK == 0 and seq % CONV_ROWS == 0
    tm_tok = _row_tile(t, 512, LANES)
    tm_seq = _row_tile(lp, 1024)
    tm_route = _row_tile(t, ROUTE_TM, 8)
    n_assign = t * TOP_K
    n_tiles = -(-n_assign // EXPERT_TM) + N_EXPERTS + GATHER_AHEAD
    n_rows = n_tiles * EXPERT_TM
    n_chunks = -(-n_assign // INV_CHUNK)

    meta = jnp.broadcast_to(meta_tokens[None].astype(x.dtype), (b, N_META, d))
    xin = jnp.concatenate([x, meta], axis=1).reshape(t, d)
    h = _ln_call(xin, ln_in_g, ln_in_b, tm_tok)
    cos_t, sin_t = _rope_tables(seq)

    in_cols = _in_proj_columns()
    uq_cols, ukv_cols = _uq_columns(), _ukv_columns()

    for l in range(depth):
        w_in_p = w_in[l][:, in_cols].astype(BF16)
        sbqk, sbv, mq, mk, mv, cu = _inproj_call(
            h.reshape(b, lp, d), w_in_p, q_norm_g[l], w_uq[l][:, uq_cols].astype(BF16),
            kv_norm_g[l], w_ukv[l][:, ukv_cols].astype(BF16), cos_t, sin_t, tm_seq)
        sb_out = _sb_call(sbqk, sbv.transpose(0, 2, 1), sbv[:, seq:, :], tq)
        mla_out = _mla_call(mq, mk, mv.transpose(0, 2, 1), mv[:, seq:, :], tq)
        conv_out = _conv_call(cu, conv_w[l], conv_b[l], conv_ln_g[l], conv_ln_b[l])
        h1, h1_tiles, idx, gates, rank, counts = _mix_call(
            sb_out.reshape(t, -1), mla_out.reshape(t, -1), conv_out.reshape(t, -1), h,
            grp_norm_g[l], w_out[l].astype(BF16), ln_mix_g[l], ln_mix_b[l],
            router_w[l].astype(BF16), router_b[l], alpha, tm_tok)

        def per_token(a):
            return a[:, :TOP_K, :].transpose(0, 2, 1).reshape(t, TOP_K)

        idx, rank = per_token(idx), per_token(rank)
        gates = jnp.pad(per_token(gates), ((0, 0), (0, LANES - TOP_K)))
        cnt = counts[:, 0].astype(jnp.int32)
        tiles_e = (cnt + EXPERT_TM - 1) // EXPERT_TM
        tile_end = jnp.cumsum(tiles_e)
        tile_start = tile_end - tiles_e
        pos = tile_start[idx] * EXPERT_TM + rank
        pos_flat = pos.reshape(-1).astype(jnp.int32)
        n_valid = tile_end[-1:].astype(jnp.int32)
        tile_ids = jnp.minimum(jnp.arange(n_tiles, dtype=jnp.int32), n_valid[0] - 1)
        tile_expert = jnp.minimum(jnp.sum(tile_end[None, :] <= tile_ids[:, None], axis=1),
                                  N_EXPERTS - 1).astype(jnp.int32)

        pos_chunks = jnp.concatenate(
            [pos_flat, jnp.full((n_chunks * INV_CHUNK - n_assign,), n_rows, jnp.int32)]
        ).reshape(n_chunks, INV_CHUNK)
        row_tok = _invert_call(cnt, tile_start.astype(jnp.int32), pos_chunks, n_rows + INV_CHUNK,
                               EXPERT_TM)
        experts = jnp.arange(N_EXPERTS, dtype=jnp.int32)
        used = tiles_e > 0
        later_used = used[None, :] & (experts[None, :] > experts[:, None])
        next_e = jnp.min(jnp.where(later_used, experts[None, :], N_EXPERTS), axis=1)
        next_e = jnp.where(next_e == N_EXPERTS, -1, next_e)
        run_e = jnp.cumsum(used) - used
        tile_next = next_e[tile_expert].astype(jnp.int32)
        tile_parity = (run_e[tile_expert] % 2).astype(jnp.int32)
        ys = _expert_call(tile_expert, tile_next, tile_parity, n_valid, row_tok, h1_tiles,
                          w_gate_up, b_gate_up[l], w_down, b_down[l], l, n_tiles, EXPERT_TM)
        h = _combine_call(pos_flat, ys, gates, h1, ln_ffn_g[l], ln_ffn_b[l], alpha, tm_route)

    return h.reshape(b, lp, d)[:, :seq, :]
```

```python
import functools

import jax
import jax.numpy as jnp
import numpy as np
from jax import lax
from jax.experimental import pallas as pl
from jax.experimental.pallas import tpu as pltpu

F32 = jnp.float32
BF16 = jnp.bfloat16

N_META = 16
SB_HEADS = 4
SB_HEAD_DIM = 64
SB_WIDTH = SB_HEADS * SB_HEAD_DIM
MLA_HEADS = 8
MLA_NOPE_DIM = 64
MLA_ROPE_DIM = 32
MLA_V_DIM = 64
MLA_Q_RANK = 256
MLA_KV_RANK = 128
MLA_WIDTH = MLA_HEADS * MLA_V_DIM
ROPE_THETA = 10000.0
CONV_CH = 256
CONV_K = 31
N_EXPERTS = 32
TOP_K = 4
SWIGLU_LIMIT = 7.0
SWIGLU_ALPHA = 1.702
LN_EPS = 1e-5
RMS_EPS = 1e-6

LANES = 128
HEAD_PAIR = LANES // SB_HEAD_DIM
ROPE_HALF = MLA_ROPE_DIM // 2
VMEM_LIMIT = 48 * 1024 * 1024
EXPERT_VMEM_LIMIT = 56 * 1024 * 1024
NEG_BIG = -1e30
LOG2E = 1.4426950408889634
ATTN_TK = 256

IN_SB = 3 * SB_WIDTH
IN_CQ = IN_SB
IN_CKV = IN_CQ + MLA_Q_RANK
IN_KPE = IN_CKV + MLA_KV_RANK
IN_CA = IN_KPE + LANES
IN_CG = IN_CA + CONV_CH
IN_TOTAL = IN_CG + CONV_CH
MQ_W = MLA_HEADS * MLA_NOPE_DIM + 2 * LANES
MK_W = MLA_HEADS * MLA_NOPE_DIM + LANES


def _row_tile(n, cap, mult=16):
    best = None
    for t in range(mult, min(n, cap) + 1, mult):
        if n % t == 0:
            best = t
    assert best is not None, (n, cap, mult)
    return best


def _layer_norm(x, g, b):
    mu = jnp.mean(x, axis=-1, keepdims=True)
    xc = x - mu
    var = jnp.mean(xc * xc, axis=-1, keepdims=True)
    return xc * lax.rsqrt(var + LN_EPS) * g + b


def _rms_norm(x, g):
    return x * lax.rsqrt(jnp.mean(x * x, axis=-1, keepdims=True) + RMS_EPS) * g


def _sigmoid(x):
    return 1.0 / (1.0 + jnp.exp(-x))


def _ln_kernel(x_ref, meta_ref, g_ref, b_ref, o_ref, *, nx):
    i = pl.program_id(1)

    @pl.when(i < nx)
    def _():
        o_ref[0] = _layer_norm(x_ref[0], g_ref[...], b_ref[...])

    @pl.when(i == nx)
    def _():
        o_ref[0, 0:N_META, :] = _layer_norm(meta_ref[...], g_ref[...], b_ref[...])


def _ln_call(x, meta_tokens, g, b, tm):
    bsz, seq, d = x.shape
    nx = seq // tm
    const2 = lambda bi, i: (0, 0)
    return pl.pallas_call(
        functools.partial(_ln_kernel, nx=nx),
        out_shape=jax.ShapeDtypeStruct((bsz, seq + N_META, d), F32),
        grid=(bsz, nx + 1),
        in_specs=[pl.BlockSpec((1, tm, d), lambda bi, i: (bi, jnp.minimum(i, nx - 1), 0)),
                  pl.BlockSpec((N_META, d), const2),
                  pl.BlockSpec((1, d), const2),
                  pl.BlockSpec((1, d), const2)],
        out_specs=pl.BlockSpec((1, tm, d), lambda bi, i: (bi, i, 0)),
        compiler_params=pltpu.CompilerParams(dimension_semantics=("parallel", "arbitrary"),
                                             vmem_limit_bytes=VMEM_LIMIT),
        name="ln_in",
    )(x, meta_tokens.astype(x.dtype), g.reshape(1, d), b.reshape(1, d))


def _inproj_kernel(h_ref, w_ref, qg_ref, wuq_ref, kvg_ref, wukv_ref, cos_ref, sin_ref,
                   sbqk_ref, sbv_ref, mq_ref, mk_ref, mv_ref, cu_ref):
    h = h_ref[0].astype(BF16)
    proj = jnp.dot(h, w_ref[...], preferred_element_type=F32)
    sbqk_ref[0, :, :SB_WIDTH] = (proj[:, :SB_WIDTH] * (SB_HEAD_DIM ** -0.5 * LOG2E)).astype(BF16)
    sbqk_ref[0, :, SB_WIDTH:] = proj[:, SB_WIDTH:2 * SB_WIDTH].astype(BF16)
    sbv_ref[0] = proj[:, 2 * SB_WIDTH:IN_SB].astype(BF16)
    cos = cos_ref[...]
    sin = sin_ref[...]

    def rot(x):
        return x * cos + pltpu.roll(x, LANES // 2, 1) * sin

    cq = _rms_norm(proj[:, IN_CQ:IN_CKV], qg_ref[...])
    qm = jnp.dot(cq.astype(BF16), wuq_ref[...], preferred_element_type=F32)
    qm = qm * ((MLA_NOPE_DIM + MLA_ROPE_DIM) ** -0.5 * LOG2E)
    nope_w = MLA_HEADS * MLA_NOPE_DIM
    mq_ref[0, :, :nope_w] = qm[:, :nope_w].astype(BF16)
    mq_ref[0, :, nope_w:nope_w + LANES] = rot(qm[:, nope_w:nope_w + LANES]).astype(BF16)
    mq_ref[0, :, nope_w + LANES:] = rot(qm[:, nope_w + LANES:]).astype(BF16)

    ckv = _rms_norm(proj[:, IN_CKV:IN_KPE], kvg_ref[...])
    kv = jnp.dot(ckv.astype(BF16), wukv_ref[...], preferred_element_type=F32)
    mk_ref[0, :, :nope_w] = kv[:, :nope_w].astype(BF16)
    mk_ref[0, :, nope_w:] = rot(proj[:, IN_KPE:IN_CA]).astype(BF16)
    mv_ref[0] = kv[:, nope_w:].astype(BF16)

    cu_ref[0] = proj[:, IN_CA:IN_CG] * _sigmoid(proj[:, IN_CG:IN_TOTAL])


def _inproj_call(h3, w_in_p, qg, wuq_p, kvg, wukv_p, cos_t, sin_t, tm):
    b, lp, d = h3.shape
    nt = lp // tm
    const2 = lambda bi, ti: (0, 0)
    tok3 = lambda bi, ti: (bi, ti, 0)
    outs = (jax.ShapeDtypeStruct((b, lp, 2 * SB_WIDTH), BF16),
            jax.ShapeDtypeStruct((b, lp, SB_WIDTH), BF16),
            jax.ShapeDtypeStruct((b, lp, MQ_W), BF16),
            jax.ShapeDtypeStruct((b, lp, MK_W), BF16),
            jax.ShapeDtypeStruct((b, lp, MLA_WIDTH), BF16),
            jax.ShapeDtypeStruct((b, lp, CONV_CH), F32))
    return pl.pallas_call(
        _inproj_kernel,
        out_shape=outs,
        grid=(b, nt),
        in_specs=[pl.BlockSpec((1, tm, d), tok3),
                  pl.BlockSpec(w_in_p.shape, const2),
                  pl.BlockSpec((1, MLA_Q_RANK), const2),
                  pl.BlockSpec(wuq_p.shape, const2),
                  pl.BlockSpec((1, MLA_KV_RANK), const2),
                  pl.BlockSpec(wukv_p.shape, const2),
                  pl.BlockSpec((tm, LANES), lambda bi, ti: (ti, 0)),
                  pl.BlockSpec((tm, LANES), lambda bi, ti: (ti, 0))],
        out_specs=tuple(pl.BlockSpec((1, tm, s.shape[2]), tok3) for s in outs),
        compiler_params=pltpu.CompilerParams(dimension_semantics=("parallel", "parallel"),
                                             vmem_limit_bytes=VMEM_LIMIT),
        name="in_proj",
    )(h3, w_in_p, qg.reshape(1, -1), wuq_p, kvg.reshape(1, -1), wukv_p, cos_t, sin_t)


def _log_stay_take2(z):
    nz = -z
    soft = jnp.log2(1.0 + jnp.exp2(jnp.minimum(z, nz)))
    return jnp.minimum(nz, 0.0) - soft, jnp.minimum(z, 0.0) - soft


def _sb_kernel(q_ref, k_ref, vt_ref, vmeta_ref, o_ref, acc_ref, c_ref, *, tq, seq):
    i = pl.program_id(1)
    nq = seq // tq
    npair = SB_HEADS // HEAD_PAIR
    lane = lax.broadcasted_iota(jnp.int32, (1, LANES), 1)
    head_sel = (lane < SB_HEAD_DIM, lane >= SB_HEAD_DIM)
    top_rows = lax.broadcasted_iota(jnp.int32, (LANES, 1), 0) < SB_HEAD_DIM
    meta = pl.ds(seq, N_META)

    def iota2(rows, cols):
        return (lax.broadcasted_iota(jnp.int32, (rows, cols), 0),
                lax.broadcasted_iota(jnp.int32, (rows, cols), 1))

    def tri_down(w):
        r, c = iota2(w, w)
        return jnp.where(c > r, 1.0, 0.0).astype(BF16)

    def tile_all(q, krows, mask, ut, first):
        zs = []
        for p in range(npair):
            cols = slice(p * LANES, (p + 1) * LANES)
            kt = k_ref[0, krows, cols]
            for hd in range(HEAD_PAIR):
                qh = jnp.where(head_sel[hd], q[:, cols], jnp.zeros((1, 1), BF16))
                zs.append(lax.dot_general(kt, qh, (((1,), (1,)), ((), ())),
                                          preferred_element_type=F32))
        stays, takes = [], []
        for h in range(SB_HEADS):
            ls, lt = _log_stay_take2(zs[h])
            if mask is not None:
                ls = jnp.where(mask, ls, 0.0)
            stays.append(ls.astype(BF16))
            takes.append(lt)
        newer = [jnp.dot(ut, st, preferred_element_type=F32) for st in stays]
        ws = []
        for h in range(SB_HEADS):
            lw = takes[h] + newer[h] if first else takes[h] + newer[h] + c_ref[h]
            w = jnp.exp2(lw)
            if mask is not None:
                w = jnp.where(mask, w, 0.0)
            ws.append(w.astype(BF16))
            total = newer[h][0:1, :] + stays[h][0:1, :].astype(F32)
            c_ref[h] = total if first else c_ref[h] + total
        for p in range(npair):
            vt = vt_ref[0, p * LANES:(p + 1) * LANES, krows]
            res = [jnp.dot(vt, ws[p * HEAD_PAIR + hd], preferred_element_type=F32)
                   for hd in range(HEAD_PAIR)]
            contrib = jnp.where(top_rows, res[0], res[1])
            acc_ref[p] = contrib if first else acc_ref[p] + contrib

    @pl.when(i < nq)
    def _():
        q = q_ref[0]
        ut = tri_down(ATTN_TK)
        r, c = iota2(ATTN_TK, tq)
        per_q = tq // ATTN_TK
        for d in range(per_q - 1, -1, -1):
            tile_all(q, pl.ds(pl.multiple_of(i * tq + d * ATTN_TK, ATTN_TK), ATTN_TK),
                     r + d * ATTN_TK < c, ut, d == per_q - 1)

        def body(t, carry):
            start = pl.multiple_of((i * per_q - 1 - t) * ATTN_TK, ATTN_TK)
            tile_all(q, pl.ds(start, ATTN_TK), None, ut, False)
            return carry

        lax.fori_loop(0, i * per_q, body, 0)
        tile_all(q, meta, None, tri_down(N_META), False)
        for p in range(npair):
            o_ref[0, :, p * LANES:(p + 1) * LANES] = acc_ref[p].T.astype(o_ref.dtype)

    @pl.when(i == nq)
    def _():
        r, c = iota2(N_META, N_META)
        mask = c < r
        ut = jnp.where(r > c, 1.0, 0.0).astype(BF16)
        for p in range(npair):
            cols = slice(p * LANES, (p + 1) * LANES)
            q = q_ref[0, 0:N_META, cols]
            kt = k_ref[0, meta, cols]
            vm = vmeta_ref[0, :, cols]
            res = []
            for hd in range(HEAD_PAIR):
                qh = jnp.where(head_sel[hd], q, jnp.zeros((1, 1), BF16))
                z = lax.dot_general(qh, kt, (((1,), (1,)), ((), ())), preferred_element_type=F32)
                ls, lt = _log_stay_take2(z)
                newer = jnp.dot(jnp.where(mask, ls, 0.0).astype(BF16), ut, preferred_element_type=F32)
                w = jnp.where(mask, jnp.exp2(lt + newer), 0.0)
                res.append(jnp.dot(w.astype(BF16), vm, preferred_element_type=F32))
            o_ref[0, 0:N_META, cols] = jnp.where(head_sel[0], res[0], res[1]).astype(o_ref.dtype)


def _sb_call(sbqk, sbv_t, sbv_meta, tq):
    b, lp, _ = sbqk.shape
    seq = lp - N_META
    nq = seq // tq
    return pl.pallas_call(
        functools.partial(_sb_kernel, tq=tq, seq=seq),
        out_shape=jax.ShapeDtypeStruct((b, lp, SB_WIDTH), BF16),
        grid=(b, nq + 1),
        in_specs=[pl.BlockSpec((1, tq, SB_WIDTH), lambda bi, i: (bi, i, 0)),
                  pl.BlockSpec((1, lp, SB_WIDTH), lambda bi, i: (bi, 0, 1)),
                  pl.BlockSpec((1, SB_WIDTH, lp), lambda bi, i: (bi, 0, 0)),
                  pl.BlockSpec((1, N_META, SB_WIDTH), lambda bi, i: (bi, 0, 0))],
        out_specs=pl.BlockSpec((1, tq, SB_WIDTH), lambda bi, i: (bi, i, 0)),
        scratch_shapes=[pltpu.VMEM((SB_HEADS // HEAD_PAIR, LANES, tq), F32),
                        pltpu.VMEM((SB_HEADS, 1, tq), F32)],
        compiler_params=pltpu.CompilerParams(
            dimension_semantics=("parallel", "arbitrary"),
            vmem_limit_bytes=VMEM_LIMIT),
        name="sb_attn",
    )(sbqk, sbqk, sbv_t, sbv_meta)


def _mla_kernel(q_ref, k_ref, vt_ref, vmeta_ref, o_ref, acc_ref, m_ref, *, tq, seq):
    i = pl.program_id(1)
    nq = seq // tq
    npair = MLA_HEADS // HEAD_PAIR
    nope_w = MLA_HEADS * MLA_NOPE_DIM
    grp = LANES // 2 // ROPE_HALF
    lane = lax.broadcasted_iota(jnp.int32, (1, LANES), 1)
    lane2 = lax.broadcasted_iota(jnp.int32, (1, 2 * LANES), 1)
    top_rows = lax.broadcasted_iota(jnp.int32, (LANES, 1), 0) < MLA_V_DIM
    meta = pl.ds(seq, N_META)
    one = jnp.ones((1, 1), BF16)

    def head_lanes(h):
        n0 = (h % HEAD_PAIR) * MLA_NOPE_DIM
        r0 = LANES + (h % grp) * ROPE_HALF
        r1 = r0 + LANES // 2
        return (((lane2 >= n0) & (lane2 < n0 + MLA_NOPE_DIM))
                | ((lane2 >= r0) & (lane2 < r0 + ROPE_HALF))
                | ((lane2 >= r1) & (lane2 < r1 + ROPE_HALF)))

    def qcat(q, p):
        rope0 = nope_w + (p * HEAD_PAIR // grp) * LANES
        return jnp.concatenate([q[:, p * LANES:(p + 1) * LANES], q[:, rope0:rope0 + LANES]], axis=-1)

    def kcat(krows, p):
        return jnp.concatenate([k_ref[0, krows, p * LANES:(p + 1) * LANES],
                                k_ref[0, krows, nope_w:nope_w + LANES]], axis=-1)

    def iota2(rows, cols):
        return (lax.broadcasted_iota(jnp.int32, (rows, cols), 0),
                lax.broadcasted_iota(jnp.int32, (rows, cols), 1))

    def tile_all(q, krows, mask, first):
        scores = []
        for p in range(npair):
            kc = kcat(krows, p)
            qc = qcat(q, p)
            for hd in range(HEAD_PAIR):
                qh = jnp.where(head_lanes(p * HEAD_PAIR + hd), qc, jnp.zeros((1, 1), BF16))
                scores.append(lax.dot_general(kc, qh, (((1,), (1,)), ((), ())),
                                              preferred_element_type=F32))
        pexps, alphas = [], []
        for h in range(MLA_HEADS):
            s = scores[h] if mask is None else jnp.where(mask, scores[h], NEG_BIG)
            mx = jnp.max(s, axis=0, keepdims=True)
            if first:
                m_new = mx
                alphas.append(None)
            else:
                m_old = m_ref[h]
                m_new = jnp.maximum(m_old, mx)
                alphas.append(jnp.exp2(m_old - m_new))
            m_ref[h] = m_new
            pexps.append(jnp.exp2(s - m_new).astype(BF16))
        for h in range(MLA_HEADS):
            p, hd = divmod(h, HEAD_PAIR)
            vt = vt_ref[0, p * LANES:(p + 1) * LANES, krows]
            vte = jnp.where(top_rows, vt, one) if hd == 0 else jnp.where(top_rows, one, vt)
            pv = jnp.dot(vte, pexps[h], preferred_element_type=F32)
            acc_ref[h] = pv if first else acc_ref[h] * alphas[h] + pv

    @pl.when(i < nq)
    def _():
        q = q_ref[0]
        r, c = iota2(ATTN_TK, tq)
        per_q = tq // ATTN_TK
        for d in range(per_q):
            tile_all(q, pl.ds(pl.multiple_of(i * tq + d * ATTN_TK, ATTN_TK), ATTN_TK),
                     r + d * ATTN_TK <= c, d == 0)

        def body(t, carry):
            tile_all(q, pl.ds(pl.multiple_of(t * ATTN_TK, ATTN_TK), ATTN_TK), None, False)
            return carry

        lax.fori_loop(0, i * per_q, body, 0)
        tile_all(q, meta, None, False)
        for p in range(npair):
            a = acc_ref[p * HEAD_PAIR]
            b = acc_ref[p * HEAD_PAIR + 1]
            out_t = jnp.where(top_rows, a * (1.0 / a[MLA_V_DIM:MLA_V_DIM + 1, :]), b * (1.0 / b[0:1, :]))
            o_ref[0, :, p * LANES:(p + 1) * LANES] = out_t.T.astype(o_ref.dtype)

    @pl.when(i == nq)
    def _():
        q = q_ref[0, 0:N_META, :]
        r, c = iota2(N_META, N_META)
        for p in range(npair):
            kc = kcat(meta, p)
            qc = qcat(q, p)
            vm = vmeta_ref[0, :, p * LANES:(p + 1) * LANES]
            res = []
            for hd in range(HEAD_PAIR):
                qh = jnp.where(head_lanes(p * HEAD_PAIR + hd), qc, jnp.zeros((1, 1), BF16))
                s = lax.dot_general(qh, kc, (((1,), (1,)), ((), ())), preferred_element_type=F32)
                s = jnp.where(c <= r, s, NEG_BIG)
                pexp = jnp.exp2(s - jnp.max(s, axis=1, keepdims=True))
                pv = jnp.dot(pexp.astype(BF16), vm, preferred_element_type=F32)
                res.append(pv * (1.0 / jnp.sum(pexp, axis=1, keepdims=True)))
            o_ref[0, 0:N_META, p * LANES:(p + 1) * LANES] = jnp.where(
                lane < MLA_V_DIM, res[0], res[1]).astype(o_ref.dtype)


def _mla_call(mq, mk, mv_t, mv_meta, tq):
    b, lp, _ = mq.shape
    seq = lp - N_META
    nq = seq // tq
    return pl.pallas_call(
        functools.partial(_mla_kernel, tq=tq, seq=seq),
        out_shape=jax.ShapeDtypeStruct((b, lp, MLA_WIDTH), BF16),
        grid=(b, nq + 1),
        in_specs=[pl.BlockSpec((1, tq, MQ_W), lambda bi, i: (bi, i, 0)),
                  pl.BlockSpec((1, lp, MK_W), lambda bi, i: (bi, 0, 0)),
                  pl.BlockSpec((1, MLA_WIDTH, lp), lambda bi, i: (bi, 0, 0)),
                  pl.BlockSpec((1, N_META, MLA_WIDTH), lambda bi, i: (bi, 0, 0))],
        out_specs=pl.BlockSpec((1, tq, MLA_WIDTH), lambda bi, i: (bi, i, 0)),
        scratch_shapes=[pltpu.VMEM((MLA_HEADS, LANES, tq), F32),
                        pltpu.VMEM((MLA_HEADS, 1, tq), F32)],
        compiler_params=pltpu.CompilerParams(
            dimension_semantics=("parallel", "arbitrary"),
            vmem_limit_bytes=VMEM_LIMIT),
        name="mla_attn",
    )(mq, mk, mv_t, mv_meta)


CONV_PAD = 32
CONV_ROWS = 128
SUBLANES = 8
CONV_WIN_EXTRA = CONV_PAD


def _conv_kernel(u_ref, w_ref, b_ref, g_ref, beta_ref, o_ref, buf_ref, *, seq):
    buf_ref[0:CONV_PAD, :] = jnp.zeros((CONV_PAD, CONV_CH), F32)
    buf_ref[CONV_PAD:CONV_PAD + N_META, :] = u_ref[0, seq:seq + N_META, :]
    buf_ref[CONV_PAD + N_META:CONV_PAD + N_META + seq, :] = u_ref[0, 0:seq, :]
    tail = CONV_PAD + N_META + seq
    buf_ref[tail:tail + SUBLANES, :] = jnp.zeros((SUBLANES, CONV_CH), F32)
    w = w_ref[...]
    lead = CONV_PAD - (CONV_K - 1)

    def finish(acc):
        y = _layer_norm(acc + b_ref[...], g_ref[...], beta_ref[...])
        return (y * _sigmoid(y)).astype(o_ref.dtype)

    def conv_rows(first_pos, rows):
        win = buf_ref[pl.ds(first_pos, rows + CONV_WIN_EXTRA + SUBLANES), :]
        acc = jnp.zeros((rows, CONV_CH), F32)
        for sh in range(SUBLANES):
            part = None
            for o in range(lead, lead + CONV_K):
                if o % SUBLANES != sh:
                    continue
                k = o - lead
                term = win[o - sh:o - sh + rows + SUBLANES, :] * w[k:k + 1, :]
                part = term if part is None else part + term
            acc = acc + part[sh:sh + rows, :]
        return finish(acc)

    def body(c, _):
        r0 = pl.multiple_of(c * CONV_ROWS, CONV_ROWS)
        o_ref[0, pl.ds(r0, CONV_ROWS), :] = conv_rows(pl.multiple_of(r0 + N_META, SUBLANES), CONV_ROWS)
        return 0

    lax.fori_loop(0, seq // CONV_ROWS, body, 0)
    o_ref[0, seq:seq + N_META, :] = conv_rows(0, N_META)


def _conv_call(cu, conv_w, conv_b, ln_g, ln_b):
    b, lp, _ = cu.shape
    seq = lp - N_META
    const2 = lambda bi: (0, 0)
    return pl.pallas_call(
        functools.partial(_conv_kernel, seq=seq),
        out_shape=jax.ShapeDtypeStruct((b, lp, CONV_CH), BF16),
        grid=(b,),
        in_specs=[pl.BlockSpec((1, lp, CONV_CH), lambda bi: (bi, 0, 0)),
                  pl.BlockSpec((CONV_K, CONV_CH), const2),
                  pl.BlockSpec((1, CONV_CH), const2),
                  pl.BlockSpec((1, CONV_CH), const2),
                  pl.BlockSpec((1, CONV_CH), const2)],
        out_specs=pl.BlockSpec((1, lp, CONV_CH), lambda bi: (bi, 0, 0)),
        scratch_shapes=[pltpu.VMEM((CONV_PAD + lp + SUBLANES, CONV_CH), F32)],
        compiler_params=pltpu.CompilerParams(dimension_semantics=("parallel",),
                                             vmem_limit_bytes=VMEM_LIMIT),
        name="conv",
    )(cu, conv_w, conv_b.reshape(1, -1), ln_g.reshape(1, -1), ln_b.reshape(1, -1))


ROUTE_ROWS = 8


def _mix_kernel(sb_ref, mla_ref, cv_ref, h_ref, gg_ref, wo_ref, lg_ref, lb_ref, rw_ref, rb_ref,
                h1_ref, h1t_ref, idx_ref, gate_ref, rank_ref, cnt_ref, carry_ref, *, alpha, tm):
    step = pl.program_id(0)

    @pl.when(step == 0)
    def _():
        carry_ref[...] = jnp.zeros_like(carry_ref)

    gg = gg_ref[...]
    y = jnp.concatenate(
        [_rms_norm(sb_ref[...].astype(F32), gg[:, :SB_WIDTH]),
         _rms_norm(mla_ref[...].astype(F32), gg[:, SB_WIDTH:SB_WIDTH + MLA_WIDTH]),
         _rms_norm(cv_ref[...].astype(F32), gg[:, SB_WIDTH + MLA_WIDTH:])], axis=-1)
    mix = jnp.dot(y.astype(BF16), wo_ref[...], preferred_element_type=F32)
    h1 = _layer_norm(alpha * h_ref[...] + mix, lg_ref[...], lb_ref[...])
    h1_ref[...] = h1
    _store_token_tiles(h1t_ref, h1)

    logits = jnp.dot(h1.astype(BF16), rw_ref[...], preferred_element_type=F32) + rb_ref[...]
    vals = logits.T[:N_EXPERTS, :]
    eiota = lax.broadcasted_iota(jnp.int32, (N_EXPERTS, 1), 0).astype(F32)
    sels, tops, idxs = [], [], []
    for _ in range(TOP_K):
        m = jnp.max(vals, axis=0, keepdims=True)
        idx = jnp.min(jnp.where(vals == m, eiota, float(N_EXPERTS)), axis=0, keepdims=True)
        sel = eiota == idx
        vals = jnp.where(sel, -jnp.inf, vals)
        sels.append(sel)
        tops.append(m)
        idxs.append(idx)
    exps = [jnp.exp(t - tops[0]) for t in tops]
    denom = exps[0] + exps[1] + exps[2] + exps[3]

    chosen = jnp.zeros((N_EXPERTS, tm), F32)
    for sel in sels:
        chosen = chosen + jnp.where(sel, 1.0, 0.0)
    r = lax.broadcasted_iota(jnp.int32, (tm, tm), 0)
    c = lax.broadcasted_iota(jnp.int32, (tm, tm), 1)
    before = jnp.where(r < c, 1.0, 0.0).astype(BF16)
    earlier = jnp.dot(chosen.astype(BF16), before, preferred_element_type=F32) + carry_ref[...]

    row = lax.broadcasted_iota(jnp.int32, (ROUTE_ROWS, 1), 0)
    idx_out = jnp.zeros((ROUTE_ROWS, tm), jnp.int32)
    gate_out = jnp.zeros((ROUTE_ROWS, tm), F32)
    rank_out = jnp.zeros((ROUTE_ROWS, tm), jnp.int32)
    for k in range(TOP_K):
        rank_k = jnp.sum(jnp.where(sels[k], earlier, 0.0), axis=0, keepdims=True)
        idx_out = jnp.where(row == k, idxs[k].astype(jnp.int32), idx_out)
        gate_out = jnp.where(row == k, exps[k] / denom, gate_out)
        rank_out = jnp.where(row == k, rank_k.astype(jnp.int32), rank_out)
    idx_ref[0] = idx_out
    gate_ref[0] = gate_out
    rank_ref[0] = rank_out

    carry_ref[...] = carry_ref[...] + jnp.sum(chosen, axis=1, keepdims=True)
    cnt_ref[...] = carry_ref[...]


def _mix_call(sb_out, mla_out, conv_out, h2d, grp_g, w_out_b, ln_g, ln_b, router_w_b, router_b,
              alpha, tm):
    t, d = h2d.shape
    row = lambda i: (i, 0)
    const2 = lambda i: (0, 0)
    chunks = d // LANES
    steps = t // tm
    route = lambda i: (i, 0, 0)
    outs = (jax.ShapeDtypeStruct((t, d), F32),
            jax.ShapeDtypeStruct((t * chunks, LANES), F32),
            jax.ShapeDtypeStruct((steps, ROUTE_ROWS, tm), jnp.int32),
            jax.ShapeDtypeStruct((steps, ROUTE_ROWS, tm), F32),
            jax.ShapeDtypeStruct((steps, ROUTE_ROWS, tm), jnp.int32),
            jax.ShapeDtypeStruct((N_EXPERTS, 1), F32))
    rw_pad = jnp.pad(router_w_b, ((0, 0), (0, LANES - N_EXPERTS)))
    rb_pad = jnp.pad(router_b.reshape(1, N_EXPERTS), ((0, 0), (0, LANES - N_EXPERTS)))
    return pl.pallas_call(
        functools.partial(_mix_kernel, alpha=alpha, tm=tm),
        out_shape=outs,
        grid=(t // tm,),
        in_specs=[pl.BlockSpec((tm, SB_WIDTH), row),
                  pl.BlockSpec((tm, MLA_WIDTH), row),
                  pl.BlockSpec((tm, CONV_CH), row),
                  pl.BlockSpec((tm, d), row),
                  pl.BlockSpec((1, d), const2),
                  pl.BlockSpec((d, d), const2),
                  pl.BlockSpec((1, d), const2),
                  pl.BlockSpec((1, d), const2),
                  pl.BlockSpec((d, LANES), const2),
                  pl.BlockSpec((1, LANES), const2)],
        out_specs=(pl.BlockSpec((tm, d), row),
                   pl.BlockSpec((tm * chunks, LANES), row),
                   pl.BlockSpec((1, ROUTE_ROWS, tm), route),
                   pl.BlockSpec((1, ROUTE_ROWS, tm), route),
                   pl.BlockSpec((1, ROUTE_ROWS, tm), route),
                   pl.BlockSpec((N_EXPERTS, 1), const2)),
        scratch_shapes=[pltpu.VMEM((N_EXPERTS, 1), F32)],
        compiler_params=pltpu.CompilerParams(dimension_semantics=("arbitrary",),
                                             vmem_limit_bytes=VMEM_LIMIT),
        name="mix_router",
    )(sb_out, mla_out, conv_out, h2d, grp_g.reshape(1, d), w_out_b, ln_g.reshape(1, d),
      ln_b.reshape(1, d), rw_pad, rb_pad)


GATHER_AHEAD = 2
GATHER_BUFS = GATHER_AHEAD + 1


def _store_token_tiles(ref, val):
    tm, d = val.shape
    chunks = d // LANES
    for c in range(chunks):
        ref[pl.ds(c, tm, stride=chunks), :] = val[:, c * LANES:(c + 1) * LANES]


def _load_token_tiles(ref, tm, chunks):
    return jnp.concatenate([ref[pl.ds(c, tm, stride=chunks), :] for c in range(chunks)], axis=-1)


INV_CHUNK = 1024


def _invert_kernel(cnt_ref, start_ref, pos_hbm, tok_ref, chunk_ref, sem, *, n_chunks, n_out, tm):
    def fill(lo, hi):
        def body(r, _):
            tok_ref[r] = 0
            return 0
        lax.fori_loop(lo, hi, body, 0)

    def pad_rows(e, used_end):
        tiles_e = (cnt_ref[e] + (tm - 1)) // tm
        fill(start_ref[e] * tm + cnt_ref[e], (start_ref[e] + tiles_e) * tm)
        return jnp.maximum(used_end, (start_ref[e] + tiles_e) * tm)

    used_end = lax.fori_loop(0, N_EXPERTS, pad_rows, 0)
    fill(used_end, n_out)

    def chunk_copy(ch, slot):
        return pltpu.make_async_copy(pos_hbm.at[ch], chunk_ref.at[pl.ds(slot * INV_CHUNK, INV_CHUNK)],
                                     sem.at[slot])

    chunk_copy(0, 0).start()
    toks = INV_CHUNK // TOP_K

    def chunk(ch, _):
        slot = ch % 2
        chunk_copy(ch, slot).wait()

        @pl.when(ch + 1 < n_chunks)
        def _():
            chunk_copy(ch + 1, 1 - slot).start()

        first = slot * INV_CHUNK

        def body(tl, _):
            for k in range(TOP_K):
                tok_ref[chunk_ref[first + tl * TOP_K + k]] = ch * toks + tl
            return 0

        lax.fori_loop(0, toks, body, 0, unroll=16)
        return 0

    lax.fori_loop(0, n_chunks, chunk, 0)


def _invert_call(cnt, tile_start, pos_chunks, n_out, tm):
    n_chunks = pos_chunks.shape[0]
    return pl.pallas_call(
        functools.partial(_invert_kernel, n_chunks=n_chunks, n_out=n_out, tm=tm),
        out_shape=jax.ShapeDtypeStruct((n_out,), jnp.int32),
        grid_spec=pltpu.PrefetchScalarGridSpec(
            num_scalar_prefetch=2,
            grid=(1,),
            in_specs=[pl.BlockSpec(memory_space=pl.ANY)],
            out_specs=pl.BlockSpec(memory_space=pltpu.SMEM),
            scratch_shapes=[pltpu.SMEM((2 * INV_CHUNK,), jnp.int32),
                            pltpu.SemaphoreType.DMA((2,))]),
        compiler_params=pltpu.CompilerParams(dimension_semantics=("arbitrary",)),
        name="moe_row_table",
    )(cnt, tile_start, pos_chunks)


def _expert_kernel(te_ref, nxt_ref, par_ref, nv_ref, tok_ref, h_hbm, wgu_hbm, bgu_ref, wd_hbm,
                   bd_ref, y_ref, xbuf, wgu_f32, wd_f32, wgu_bf, wd_bf, sem, wsem,
                   *, layer, tm, d, d_ff):
    i = pl.program_id(0)
    nv = nv_ref[0]
    chunks = d // LANES

    def weight_copies(e, wslot):
        return (pltpu.make_async_copy(wgu_hbm.at[layer, e], wgu_f32.at[wslot], wsem.at[0, wslot]),
                pltpu.make_async_copy(wd_hbm.at[layer, e], wd_f32.at[wslot], wsem.at[1, wslot]))

    def row_copy(tile, r, slot):
        tok = tok_ref[tile * tm + r]
        return pltpu.make_async_copy(
            h_hbm.at[pl.ds(pl.multiple_of(tok * chunks, chunks), chunks), :],
            xbuf.at[slot, pl.ds(r * chunks, chunks), :], sem.at[slot])

    def wait_gather(slot):
        pltpu.make_async_copy(h_hbm.at[pl.ds(0, tm * chunks), :], xbuf.at[slot], sem.at[slot]).wait()

    @pl.when(i == 0)
    def _():
        for ahead in range(GATHER_AHEAD):
            def body(r, _):
                row_copy(ahead, r, ahead).start()
                return 0
            lax.fori_loop(0, tm, body, 0, unroll=8)
        for cp in weight_copies(te_ref[0], par_ref[0]):
            cp.start()

    @pl.when((i < nv) & ((i == 0) | (te_ref[i] != te_ref[jnp.maximum(i - 1, 0)])))
    def _():
        wslot = par_ref[i]
        for cp in weight_copies(te_ref[i], wslot):
            cp.wait()
        wgu_bf[...] = wgu_f32[wslot].astype(BF16)
        wd_bf[...] = wd_f32[wslot].astype(BF16)

        @pl.when(nxt_ref[i] >= 0)
        def _():
            for cp in weight_copies(nxt_ref[i], 1 - wslot):
                cp.start()

    for slot in range(GATHER_BUFS):
        @pl.when((i < nv) & (i % GATHER_BUFS == slot))
        def _():
            wait_gather(slot)
            x = _load_token_tiles(xbuf.at[slot], tm, chunks).astype(BF16)
            gu = jnp.dot(x, wgu_bf[...], preferred_element_type=F32) + bgu_ref[0]
            g = jnp.minimum(gu[:, :d_ff], SWIGLU_LIMIT)
            up = jnp.clip(gu[:, d_ff:], -SWIGLU_LIMIT, SWIGLU_LIMIT)
            act = (up + 1.0) * (g * _sigmoid(SWIGLU_ALPHA * g))
            y = jnp.dot(act.astype(BF16), wd_bf[...], preferred_element_type=F32) + bd_ref[0]
            _store_token_tiles(y_ref, y)
            for r in range(tm):
                row_copy(i + GATHER_AHEAD, r, (slot + GATHER_AHEAD) % GATHER_BUFS).start()

        @pl.when((i >= nv) & (i < nv + GATHER_AHEAD) & (i % GATHER_BUFS == slot))
        def _():
            wait_gather(slot)

    @pl.when(i >= nv)
    def _():
        y_ref[...] = jnp.zeros_like(y_ref)


def _expert_call(tile_expert, tile_next, tile_parity, n_valid, row_tok, h_tiles, wgu_all, bgu,
                 wd_all, bd, layer, n_tiles, tm):
    d_ff, d = wd_all.shape[2:]
    chunks = d // LANES
    bsel = lambda i, te, nxt, par, nv, tok: (te[i], 0, 0)
    return pl.pallas_call(
        functools.partial(_expert_kernel, layer=layer, tm=tm, d=d, d_ff=d_ff),
        out_shape=jax.ShapeDtypeStruct((n_tiles * tm * chunks, LANES), F32),
        grid_spec=pltpu.PrefetchScalarGridSpec(
            num_scalar_prefetch=5,
            grid=(n_tiles,),
            in_specs=[pl.BlockSpec(memory_space=pl.ANY),
                      pl.BlockSpec(memory_space=pl.ANY),
                      pl.BlockSpec((1, 1, 2 * d_ff), bsel),
                      pl.BlockSpec(memory_space=pl.ANY),
                      pl.BlockSpec((1, 1, d), bsel)],
            out_specs=pl.BlockSpec((tm * chunks, LANES), lambda i, te, nxt, par, nv, tok: (i, 0)),
            scratch_shapes=[pltpu.VMEM((GATHER_BUFS, tm * chunks, LANES), F32),
                            pltpu.VMEM((2, d, 2 * d_ff), F32),
                            pltpu.VMEM((2, d_ff, d), F32),
                            pltpu.VMEM((d, 2 * d_ff), BF16),
                            pltpu.VMEM((d_ff, d), BF16),
                            pltpu.SemaphoreType.DMA((GATHER_BUFS,)),
                            pltpu.SemaphoreType.DMA((2, 2))]),
        compiler_params=pltpu.CompilerParams(dimension_semantics=("arbitrary",),
                                             vmem_limit_bytes=EXPERT_VMEM_LIMIT),
        name="moe_experts",
    )(tile_expert, tile_next, tile_parity, n_valid, row_tok, h_tiles, wgu_all,
      bgu.reshape(N_EXPERTS, 1, -1), wd_all, bd.reshape(N_EXPERTS, 1, -1))


def _combine_kernel(pos_ref, ys_hbm, gate_ref, h_ref, lg_ref, lb_ref, o_ref, buf, sem,
                    *, alpha, tm, d):
    i = pl.program_id(0)
    chunks = d // LANES

    n_steps = pl.num_programs(0)

    def row_copy(step, r, k, slot):
        src = pos_ref[step * (tm * TOP_K) + r * TOP_K + k]
        return pltpu.make_async_copy(
            ys_hbm.at[pl.ds(pl.multiple_of(src * chunks, chunks), chunks), :],
            buf.at[slot, k, pl.ds(r * chunks, chunks), :], sem.at[slot])

    def wait_rows(slot):
        for k in range(TOP_K):
            pltpu.make_async_copy(ys_hbm.at[pl.ds(0, tm * chunks), :], buf.at[slot, k],
                                  sem.at[slot]).wait()

    @pl.when(i == 0)
    def _():
        for ahead in range(GATHER_AHEAD):
            def body(r, _):
                for k in range(TOP_K):
                    row_copy(jnp.minimum(ahead, n_steps - 1), r, k, ahead).start()
                return 0
            lax.fori_loop(0, tm, body, 0, unroll=2)

    for slot in range(GATHER_BUFS):
        @pl.when(i % GATHER_BUFS == slot)
        def _():
            wait_rows(slot)
            gate = gate_ref[...]
            ffn = _load_token_tiles(buf.at[slot, 0], tm, chunks) * gate[:, 0:1]
            for k in range(1, TOP_K):
                ffn = ffn + _load_token_tiles(buf.at[slot, k], tm, chunks) * gate[:, k:k + 1]
            o_ref[...] = _layer_norm(alpha * h_ref[...] + ffn, lg_ref[...], lb_ref[...])
            nxt = jnp.minimum(i + GATHER_AHEAD, n_steps - 1)
            for r in range(tm):
                for k in range(TOP_K):
                    row_copy(nxt, r, k, (slot + GATHER_AHEAD) % GATHER_BUFS).start()

        @pl.when((i % GATHER_BUFS == slot) & (i + 1 == n_steps))
        def _():
            for ahead in range(1, GATHER_BUFS):
                wait_rows((slot + ahead) % GATHER_BUFS)


def _combine_call(pos_flat, ys, gates, h2d, ln_g, ln_b, alpha, tm):
    t, d = h2d.shape
    chunks = d // LANES
    row = lambda i, pos: (i, 0)
    const2 = lambda i, pos: (0, 0)
    return pl.pallas_call(
        functools.partial(_combine_kernel, alpha=alpha, tm=tm, d=d),
        out_shape=jax.ShapeDtypeStruct((t, d), F32),
        grid_spec=pltpu.PrefetchScalarGridSpec(
            num_scalar_prefetch=1,
            grid=(t // tm,),
            in_specs=[pl.BlockSpec(memory_space=pl.ANY),
                      pl.BlockSpec((tm, LANES), row),
                      pl.BlockSpec((tm, d), row),
                      pl.BlockSpec((1, d), const2),
                      pl.BlockSpec((1, d), const2)],
            out_specs=pl.BlockSpec((tm, d), row),
            scratch_shapes=[pltpu.VMEM((GATHER_BUFS, TOP_K, tm * chunks, LANES), F32),
                            pltpu.SemaphoreType.DMA((GATHER_BUFS,))]),
        compiler_params=pltpu.CompilerParams(dimension_semantics=("arbitrary",),
                                             vmem_limit_bytes=VMEM_LIMIT),
        name="moe_combine",
    )(pos_flat, ys, gates, h2d, ln_g.reshape(1, d), ln_b.reshape(1, d))


def _in_proj_columns():
    o_cq = 3 * SB_WIDTH
    o_ckv = o_cq + MLA_Q_RANK
    o_kpe = o_ckv + MLA_KV_RANK
    o_ca = o_kpe + MLA_ROPE_DIM
    o_cg = o_ca + CONV_CH
    reps = LANES // 2 // ROPE_HALF
    kpe = np.concatenate([np.tile(o_kpe + np.arange(ROPE_HALF), reps),
                          np.tile(o_kpe + ROPE_HALF + np.arange(ROPE_HALF), reps)])
    return np.concatenate([np.arange(o_cq), o_cq + np.arange(MLA_Q_RANK),
                           o_ckv + np.arange(MLA_KV_RANK), kpe,
                           o_ca + np.arange(CONV_CH), o_cg + np.arange(CONV_CH)])


def _uq_columns():
    per = MLA_NOPE_DIM + MLA_ROPE_DIM
    nope = np.concatenate([h * per + np.arange(MLA_NOPE_DIM) for h in range(MLA_HEADS)])
    rope = []
    grp = LANES // 2 // ROPE_HALF
    for g0 in range(0, MLA_HEADS, grp):
        for half in range(2):
            for h in range(g0, g0 + grp):
                rope.append(h * per + MLA_NOPE_DIM + half * ROPE_HALF + np.arange(ROPE_HALF))
    return np.concatenate([nope] + rope)


def _ukv_columns():
    per = MLA_NOPE_DIM + MLA_V_DIM
    kn = np.concatenate([h * per + np.arange(MLA_NOPE_DIM) for h in range(MLA_HEADS)])
    vv = np.concatenate([h * per + MLA_NOPE_DIM + np.arange(MLA_V_DIM) for h in range(MLA_HEADS)])
    return np.concatenate([kn, vv])


def _rope_tables(seq):
    lp = seq + N_META
    inv = 1.0 / (ROPE_THETA ** (jnp.arange(0, MLA_ROPE_DIM, 2, dtype=F32) / MLA_ROPE_DIM))
    pos = jnp.concatenate([N_META + jnp.arange(seq, dtype=F32), jnp.arange(N_META, dtype=F32)])
    ang = pos[:, None] * inv[None, :]
    reps = LANES // 2 // ROPE_HALF
    cos = jnp.tile(jnp.cos(ang), (1, 2 * reps))
    sin = jnp.tile(jnp.sin(ang), (1, reps))
    assert cos.shape == (lp, LANES)
    return cos, jnp.concatenate([-sin, sin], axis=1)


ATTN_TQ = 512
EXPERT_TM = 256
ROUTE_TM = 128


def kernel(x, meta_tokens, ln_in_g, ln_in_b, w_in, q_norm_g, w_uq, kv_norm_g, w_ukv, conv_w, conv_b,
           conv_ln_g, conv_ln_b, grp_norm_g, w_out, ln_mix_g, ln_mix_b, router_w, router_b,
           w_gate_up, b_gate_up, w_down, b_down, ln_ffn_g, ln_ffn_b):
    b, seq, d = x.shape
    depth = w_in.shape[0]
    lp = seq + N_META
    t = b * lp
    alpha = float((2 * depth) ** 0.25)
    tq = min(ATTN_TQ, seq)
    assert seq % tq == 0 and tq % ATTN_TK == 0 and seq % CONV_ROWS == 0
    tm_tok = _row_tile(t, 512, LANES)
    tm_seq = _row_tile(lp, 1024)
    tm_route = _row_tile(t, ROUTE_TM, 8)
    n_assign = t * TOP_K
    n_tiles = -(-n_assign // EXPERT_TM) + N_EXPERTS + GATHER_AHEAD
    n_rows = n_tiles * EXPERT_TM
    n_chunks = -(-n_assign // INV_CHUNK)

    h = _ln_call(x, meta_tokens, ln_in_g, ln_in_b, tq).reshape(t, d)
    cos_t, sin_t = _rope_tables(seq)

    in_cols = _in_proj_columns()
    uq_cols, ukv_cols = _uq_columns(), _ukv_columns()

    for l in range(depth):
        w_in_p = w_in[l][:, in_cols].astype(BF16)
        sbqk, sbv, mq, mk, mv, cu = _inproj_call(
            h.reshape(b, lp, d), w_in_p, q_norm_g[l], w_uq[l][:, uq_cols].astype(BF16),
            kv_norm_g[l], w_ukv[l][:, ukv_cols].astype(BF16), cos_t, sin_t, tm_seq)
        sb_out = _sb_call(sbqk, sbv.transpose(0, 2, 1), sbv[:, seq:, :], tq)
        mla_out = _mla_call(mq, mk, mv.transpose(0, 2, 1), mv[:, seq:, :], tq)
        conv_out = _conv_call(cu, conv_w[l], conv_b[l], conv_ln_g[l], conv_ln_b[l])
        h1, h1_tiles, idx, gates, rank, counts = _mix_call(
            sb_out.reshape(t, -1), mla_out.reshape(t, -1), conv_out.reshape(t, -1), h,
            grp_norm_g[l], w_out[l].astype(BF16), ln_mix_g[l], ln_mix_b[l],
            router_w[l].astype(BF16), router_b[l], alpha, tm_tok)

        def per_token(a):
            return a[:, :TOP_K, :].transpose(0, 2, 1).reshape(t, TOP_K)

        idx, rank = per_token(idx), per_token(rank)
        gates = jnp.pad(per_token(gates), ((0, 0), (0, LANES - TOP_K)))
        cnt = counts[:, 0].astype(jnp.int32)
        tiles_e = (cnt + EXPERT_TM - 1) // EXPERT_TM
        tile_end = jnp.cumsum(tiles_e)
        tile_start = tile_end - tiles_e
        pos = tile_start[idx] * EXPERT_TM + rank
        pos_flat = pos.reshape(-1).astype(jnp.int32)
        n_valid = tile_end[-1:].astype(jnp.int32)
        tile_ids = jnp.minimum(jnp.arange(n_tiles, dtype=jnp.int32), n_valid[0] - 1)
        tile_expert = jnp.minimum(jnp.sum(tile_end[None, :] <= tile_ids[:, None], axis=1),
                                  N_EXPERTS - 1).astype(jnp.int32)

        pos_chunks = jnp.concatenate(
            [pos_flat, jnp.full((n_chunks * INV_CHUNK - n_assign,), n_rows, jnp.int32)]
        ).reshape(n_chunks, INV_CHUNK)
        row_tok = _invert_call(cnt, tile_start.astype(jnp.int32), pos_chunks, n_rows + INV_CHUNK,
                               EXPERT_TM)
        experts = jnp.arange(N_EXPERTS, dtype=jnp.int32)
        used = tiles_e > 0
        later_used = used[None, :] & (experts[None, :] > experts[:, None])
        next_e = jnp.min(jnp.where(later_used, experts[None, :], N_EXPERTS), axis=1)
        next_e = jnp.where(next_e == N_EXPERTS, -1, next_e)
        run_e = jnp.cumsum(used) - used
        tile_next = next_e[tile_expert].astype(jnp.int32)
        tile_parity = (run_e[tile_expert] % 2).astype(jnp.int32)
        ys = _expert_call(tile_expert, tile_next, tile_parity, n_valid, row_tok, h1_tiles,
                          w_gate_up, b_gate_up[l], w_down, b_down[l], l, n_tiles, EXPERT_TM)
        h = _combine_call(pos_flat, ys, gates, h1, ln_ffn_g[l], ln_ffn_b[l], alpha, tm_route)

    return h.reshape(b, lp, d)[:, :seq, :]
```

```python
import functools

import jax
import jax.numpy as jnp
import numpy as np
from jax import lax
from jax.experimental import pallas as pl
from jax.experimental.pallas import tpu as pltpu

F32 = jnp.float32
BF16 = jnp.bfloat16

N_META = 16
SB_HEADS = 4
SB_HEAD_DIM = 64
SB_WIDTH = SB_HEADS * SB_HEAD_DIM
MLA_HEADS = 8
MLA_NOPE_DIM = 64
MLA_ROPE_DIM = 32
MLA_V_DIM = 64
MLA_Q_RANK = 256
MLA_KV_RANK = 128
MLA_WIDTH = MLA_HEADS * MLA_V_DIM
ROPE_THETA = 10000.0
CONV_CH = 256
CONV_K = 31
N_EXPERTS = 32
TOP_K = 4
SWIGLU_LIMIT = 7.0
SWIGLU_ALPHA = 1.702
LN_EPS = 1e-5
RMS_EPS = 1e-6

LANES = 128
HEAD_PAIR = LANES // SB_HEAD_DIM
ROPE_HALF = MLA_ROPE_DIM // 2
VMEM_LIMIT = 48 * 1024 * 1024
EXPERT_VMEM_LIMIT = 56 * 1024 * 1024
NEG_BIG = -1e30
LOG2E = 1.4426950408889634
ATTN_TK = 256

IN_SB = 3 * SB_WIDTH
IN_CQ = IN_SB
IN_CKV = IN_CQ + MLA_Q_RANK
IN_KPE = IN_CKV + MLA_KV_RANK
IN_CA = IN_KPE + LANES
IN_CG = IN_CA + CONV_CH
IN_TOTAL = IN_CG + CONV_CH
MQ_W = MLA_HEADS * MLA_NOPE_DIM + 2 * LANES
MK_W = MLA_HEADS * MLA_NOPE_DIM + LANES


def _row_tile(n, cap, mult=16):
    best = None
    for t in range(mult, min(n, cap) + 1, mult):
        if n % t == 0:
            best = t
    assert best is not None, (n, cap, mult)
    return best


def _layer_norm(x, g, b):
    mu = jnp.mean(x, axis=-1, keepdims=True)
    xc = x - mu
    var = jnp.mean(xc * xc, axis=-1, keepdims=True)
    return xc * lax.rsqrt(var + LN_EPS) * g + b


def _rms_norm(x, g):
    return x * lax.rsqrt(jnp.mean(x * x, axis=-1, keepdims=True) + RMS_EPS) * g


def _sigmoid(x):
    return 1.0 / (1.0 + jnp.exp(-x))


def _ln_kernel(x_ref, meta_ref, g_ref, b_ref, o_ref, *, nx):
    i = pl.program_id(1)

    @pl.when(i < nx)
    def _():
        o_ref[0] = _layer_norm(x_ref[0], g_ref[...], b_ref[...])

    @pl.when(i == nx)
    def _():
        o_ref[0, 0:N_META, :] = _layer_norm(meta_ref[...], g_ref[...], b_ref[...])


def _ln_call(x, meta_tokens, g, b, tm):
    bsz, seq, d = x.shape
    nx = seq // tm
    const2 = lambda bi, i: (0, 0)
    return pl.pallas_call(
        functools.partial(_ln_kernel, nx=nx),
        out_shape=jax.ShapeDtypeStruct((bsz, seq + N_META, d), F32),
        grid=(bsz, nx + 1),
        in_specs=[pl.BlockSpec((1, tm, d), lambda bi, i: (bi, jnp.minimum(i, nx - 1), 0)),
                  pl.BlockSpec((N_META, d), const2),
                  pl.BlockSpec((1, d), const2),
                  pl.BlockSpec((1, d), const2)],
        out_specs=pl.BlockSpec((1, tm, d), lambda bi, i: (bi, i, 0)),
        compiler_params=pltpu.CompilerParams(dimension_semantics=("parallel", "arbitrary"),
                                             vmem_limit_bytes=VMEM_LIMIT),
        name="ln_in",
    )(x, meta_tokens.astype(x.dtype), g.reshape(1, d), b.reshape(1, d))


def _inproj_kernel(h_ref, w_ref, qg_ref, wuq_ref, kvg_ref, wukv_ref, cos_ref, sin_ref,
                   sbqk_ref, sbv_ref, mq_ref, mk_ref, mv_ref, cu_ref):
    h = h_ref[0].astype(BF16)
    proj = jnp.dot(h, w_ref[...], preferred_element_type=F32)
    sbqk_ref[0, :, :SB_WIDTH] = (proj[:, :SB_WIDTH] * (SB_HEAD_DIM ** -0.5 * LOG2E)).astype(BF16)
    sbqk_ref[0, :, SB_WIDTH:] = proj[:, SB_WIDTH:2 * SB_WIDTH].astype(BF16)
    sbv_ref[0] = proj[:, 2 * SB_WIDTH:IN_SB].astype(BF16)
    cos = cos_ref[...]
    sin = sin_ref[...]

    def rot(x):
        return x * cos + pltpu.roll(x, LANES // 2, 1) * sin

    cq = _rms_norm(proj[:, IN_CQ:IN_CKV], qg_ref[...])
    qm = jnp.dot(cq.astype(BF16), wuq_ref[...], preferred_element_type=F32)
    qm = qm * ((MLA_NOPE_DIM + MLA_ROPE_DIM) ** -0.5 * LOG2E)
    nope_w = MLA_HEADS * MLA_NOPE_DIM
    mq_ref[0, :, :nope_w] = qm[:, :nope_w].astype(BF16)
    mq_ref[0, :, nope_w:nope_w + LANES] = rot(qm[:, nope_w:nope_w + LANES]).astype(BF16)
    mq_ref[0, :, nope_w + LANES:] = rot(qm[:, nope_w + LANES:]).astype(BF16)

    ckv = _rms_norm(proj[:, IN_CKV:IN_KPE], kvg_ref[...])
    kv = jnp.dot(ckv.astype(BF16), wukv_ref[...], preferred_element_type=F32)
    mk_ref[0, :, :nope_w] = kv[:, :nope_w].astype(BF16)
    mk_ref[0, :, nope_w:] = rot(proj[:, IN_KPE:IN_CA]).astype(BF16)
    mv_ref[0] = kv[:, nope_w:].astype(BF16)

    cu_ref[0] = proj[:, IN_CA:IN_CG] * _sigmoid(proj[:, IN_CG:IN_TOTAL])


def _inproj_call(h3, w_in_p, qg, wuq_p, kvg, wukv_p, cos_t, sin_t, tm):
    b, lp, d = h3.shape
    nt = lp // tm
    const2 = lambda bi, ti: (0, 0)
    tok3 = lambda bi, ti: (bi, ti, 0)
    outs = (jax.ShapeDtypeStruct((b, lp, 2 * SB_WIDTH), BF16),
            jax.ShapeDtypeStruct((b, lp, SB_WIDTH), BF16),
            jax.ShapeDtypeStruct((b, lp, MQ_W), BF16),
            jax.ShapeDtypeStruct((b, lp, MK_W), BF16),
            jax.ShapeDtypeStruct((b, lp, MLA_WIDTH), BF16),
            jax.ShapeDtypeStruct((b, lp, CONV_CH), F32))
    return pl.pallas_call(
        _inproj_kernel,
        out_shape=outs,
        grid=(b, nt),
        in_specs=[pl.BlockSpec((1, tm, d), tok3),
                  pl.BlockSpec(w_in_p.shape, const2),
                  pl.BlockSpec((1, MLA_Q_RANK), const2),
                  pl.BlockSpec(wuq_p.shape, const2),
                  pl.BlockSpec((1, MLA_KV_RANK), const2),
                  pl.BlockSpec(wukv_p.shape, const2),
                  pl.BlockSpec((tm, LANES), lambda bi, ti: (ti, 0)),
                  pl.BlockSpec((tm, LANES), lambda bi, ti: (ti, 0))],
        out_specs=tuple(pl.BlockSpec((1, tm, s.shape[2]), tok3) for s in outs),
        compiler_params=pltpu.CompilerParams(dimension_semantics=("parallel", "parallel"),
                                             vmem_limit_bytes=VMEM_LIMIT),
        name="in_proj",
    )(h3, w_in_p, qg.reshape(1, -1), wuq_p, kvg.reshape(1, -1), wukv_p, cos_t, sin_t)


def _log_stay_take2(z):
    nz = -z
    soft = jnp.log2(1.0 + jnp.exp2(jnp.minimum(z, nz)))
    return jnp.minimum(nz, 0.0) - soft, jnp.minimum(z, 0.0) - soft


def _sb_kernel(q_ref, k_ref, vt_ref, vmeta_ref, o_ref, acc_ref, c_ref, *, tq, seq):
    i = pl.program_id(1)
    nq = seq // tq
    npair = SB_HEADS // HEAD_PAIR
    lane = lax.broadcasted_iota(jnp.int32, (1, LANES), 1)
    head_sel = (lane < SB_HEAD_DIM, lane >= SB_HEAD_DIM)
    top_rows = lax.broadcasted_iota(jnp.int32, (LANES, 1), 0) < SB_HEAD_DIM
    meta = pl.ds(seq, N_META)

    def iota2(rows, cols):
        return (lax.broadcasted_iota(jnp.int32, (rows, cols), 0),
                lax.broadcasted_iota(jnp.int32, (rows, cols), 1))

    def tri_down(w):
        r, c = iota2(w, w)
        return jnp.where(c > r, 1.0, 0.0).astype(BF16)

    def tile_all(q, krows, mask, ut, first):
        zs = []
        for p in range(npair):
            cols = slice(p * LANES, (p + 1) * LANES)
            kt = k_ref[0, krows, cols]
            for hd in range(HEAD_PAIR):
                qh = jnp.where(head_sel[hd], q[:, cols], jnp.zeros((1, 1), BF16))
                zs.append(lax.dot_general(kt, qh, (((1,), (1,)), ((), ())),
                                          preferred_element_type=F32))
        stays, takes = [], []
        for h in range(SB_HEADS):
            ls, lt = _log_stay_take2(zs[h])
            if mask is not None:
                ls = jnp.where(mask, ls, 0.0)
            stays.append(ls.astype(BF16))
            takes.append(lt)
        newer = [jnp.dot(ut, st, preferred_element_type=F32) for st in stays]
        ws = []
        for h in range(SB_HEADS):
            lw = takes[h] + newer[h] if first else takes[h] + newer[h] + c_ref[h]
            w = jnp.exp2(lw)
            if mask is not None:
                w = jnp.where(mask, w, 0.0)
            ws.append(w.astype(BF16))
            total = newer[h][0:1, :] + stays[h][0:1, :].astype(F32)
            c_ref[h] = total if first else c_ref[h] + total
        for p in range(npair):
            vt = vt_ref[0, p * LANES:(p + 1) * LANES, krows]
            res = [jnp.dot(vt, ws[p * HEAD_PAIR + hd], preferred_element_type=F32)
                   for hd in range(HEAD_PAIR)]
            contrib = jnp.where(top_rows, res[0], res[1])
            acc_ref[p] = contrib if first else acc_ref[p] + contrib

    @pl.when(i < nq)
    def _():
        q = q_ref[0]
        ut = tri_down(ATTN_TK)
        r, c = iota2(ATTN_TK, tq)
        per_q = tq // ATTN_TK
        for d in range(per_q - 1, -1, -1):
            tile_all(q, pl.ds(pl.multiple_of(i * tq + d * ATTN_TK, ATTN_TK), ATTN_TK),
                     r + d * ATTN_TK < c, ut, d == per_q - 1)

        def body(t, carry):
            start = pl.multiple_of((i * per_q - 1 - t) * ATTN_TK, ATTN_TK)
            tile_all(q, pl.ds(start, ATTN_TK), None, ut, False)
            return carry

        lax.fori_loop(0, i * per_q, body, 0)
        tile_all(q, meta, None, tri_down(N_META), False)
        for p in range(npair):
            o_ref[0, :, p * LANES:(p + 1) * LANES] = acc_ref[p].T.astype(o_ref.dtype)

    @pl.when(i == nq)
    def _():
        r, c = iota2(N_META, N_META)
        mask = c < r
        ut = jnp.where(r > c, 1.0, 0.0).astype(BF16)
        for p in range(npair):
            cols = slice(p * LANES, (p + 1) * LANES)
            q = q_ref[0, 0:N_META, cols]
            kt = k_ref[0, meta, cols]
            vm = vmeta_ref[0, :, cols]
            res = []
            for hd in range(HEAD_PAIR):
                qh = jnp.where(head_sel[hd], q, jnp.zeros((1, 1), BF16))
                z = lax.dot_general(qh, kt, (((1,), (1,)), ((), ())), preferred_element_type=F32)
                ls, lt = _log_stay_take2(z)
                newer = jnp.dot(jnp.where(mask, ls, 0.0).astype(BF16), ut, preferred_element_type=F32)
                w = jnp.where(mask, jnp.exp2(lt + newer), 0.0)
                res.append(jnp.dot(w.astype(BF16), vm, preferred_element_type=F32))
            o_ref[0, 0:N_META, cols] = jnp.where(head_sel[0], res[0], res[1]).astype(o_ref.dtype)


def _sb_call(sbqk, sbv_t, sbv_meta, tq):
    b, lp, _ = sbqk.shape
    seq = lp - N_META
    nq = seq // tq
    return pl.pallas_call(
        functools.partial(_sb_kernel, tq=tq, seq=seq),
        out_shape=jax.ShapeDtypeStruct((b, lp, SB_WIDTH), BF16),
        grid=(b, nq + 1),
        in_specs=[pl.BlockSpec((1, tq, SB_WIDTH), lambda bi, i: (bi, i, 0)),
                  pl.BlockSpec((1, lp, SB_WIDTH), lambda bi, i: (bi, 0, 1)),
                  pl.BlockSpec((1, SB_WIDTH, lp), lambda bi, i: (bi, 0, 0)),
                  pl.BlockSpec((1, N_META, SB_WIDTH), lambda bi, i: (bi, 0, 0))],
        out_specs=pl.BlockSpec((1, tq, SB_WIDTH), lambda bi, i: (bi, i, 0)),
        scratch_shapes=[pltpu.VMEM((SB_HEADS // HEAD_PAIR, LANES, tq), F32),
                        pltpu.VMEM((SB_HEADS, 1, tq), F32)],
        compiler_params=pltpu.CompilerParams(
            dimension_semantics=("parallel", "arbitrary"),
            vmem_limit_bytes=VMEM_LIMIT),
        name="sb_attn",
    )(sbqk, sbqk, sbv_t, sbv_meta)


def _mla_kernel(q_ref, k_ref, vt_ref, vmeta_ref, o_ref, acc_ref, m_ref, *, tq, seq):
    i = pl.program_id(1)
    nq = seq // tq
    npair = MLA_HEADS // HEAD_PAIR
    nope_w = MLA_HEADS * MLA_NOPE_DIM
    grp = LANES // 2 // ROPE_HALF
    lane = lax.broadcasted_iota(jnp.int32, (1, LANES), 1)
    lane2 = lax.broadcasted_iota(jnp.int32, (1, 2 * LANES), 1)
    top_rows = lax.broadcasted_iota(jnp.int32, (LANES, 1), 0) < MLA_V_DIM
    meta = pl.ds(seq, N_META)
    one = jnp.ones((1, 1), BF16)

    def head_lanes(h):
        n0 = (h % HEAD_PAIR) * MLA_NOPE_DIM
        r0 = LANES + (h % grp) * ROPE_HALF
        r1 = r0 + LANES // 2
        return (((lane2 >= n0) & (lane2 < n0 + MLA_NOPE_DIM))
                | ((lane2 >= r0) & (lane2 < r0 + ROPE_HALF))
                | ((lane2 >= r1) & (lane2 < r1 + ROPE_HALF)))

    def qcat(q, p):
        rope0 = nope_w + (p * HEAD_PAIR // grp) * LANES
        return jnp.concatenate([q[:, p * LANES:(p + 1) * LANES], q[:, rope0:rope0 + LANES]], axis=-1)

    def kcat(krows, p):
        return jnp.concatenate([k_ref[0, krows, p * LANES:(p + 1) * LANES],
                                k_ref[0, krows, nope_w:nope_w + LANES]], axis=-1)

    def iota2(rows, cols):
        return (lax.broadcasted_iota(jnp.int32, (rows, cols), 0),
                lax.broadcasted_iota(jnp.int32, (rows, cols), 1))

    def tile_all(q, krows, mask, first):
        scores = []
        for p in range(npair):
            kc = kcat(krows, p)
            qc = qcat(q, p)
            for hd in range(HEAD_PAIR):
                qh = jnp.where(head_lanes(p * HEAD_PAIR + hd), qc, jnp.zeros((1, 1), BF16))
                scores.append(lax.dot_general(kc, qh, (((1,), (1,)), ((), ())),
                                              preferred_element_type=F32))
        pexps, alphas = [], []
        for h in range(MLA_HEADS):
            s = scores[h] if mask is None else jnp.where(mask, scores[h], NEG_BIG)
            mx = jnp.max(s, axis=0, keepdims=True)
            if first:
                m_new = mx
                alphas.append(None)
            else:
                m_old = m_ref[h]
                m_new = jnp.maximum(m_old, mx)
                alphas.append(jnp.exp2(m_old - m_new))
            m_ref[h] = m_new
            pexps.append(jnp.exp2(s - m_new).astype(BF16))
        for h in range(MLA_HEADS):
            p, hd = divmod(h, HEAD_PAIR)
            vt = vt_ref[0, p * LANES:(p + 1) * LANES, krows]
            vte = jnp.where(top_rows, vt, one) if hd == 0 else jnp.where(top_rows, one, vt)
            pv = jnp.dot(vte, pexps[h], preferred_element_type=F32)
            acc_ref[h] = pv if first else acc_ref[h] * alphas[h] + pv

    @pl.when(i < nq)
    def _():
        q = q_ref[0]
        r, c = iota2(ATTN_TK, tq)
        per_q = tq // ATTN_TK
        for d in range(per_q):
            tile_all(q, pl.ds(pl.multiple_of(i * tq + d * ATTN_TK, ATTN_TK), ATTN_TK),
                     r + d * ATTN_TK <= c, d == 0)

        def body(t, carry):
            tile_all(q, pl.ds(pl.multiple_of(t * ATTN_TK, ATTN_TK), ATTN_TK), None, False)
            return carry

        lax.fori_loop(0, i * per_q, body, 0)
        tile_all(q, meta, None, False)
        for p in range(npair):
            a = acc_ref[p * HEAD_PAIR]
            b = acc_ref[p * HEAD_PAIR + 1]
            out_t = jnp.where(top_rows, a * (1.0 / a[MLA_V_DIM:MLA_V_DIM + 1, :]), b * (1.0 / b[0:1, :]))
            o_ref[0, :, p * LANES:(p + 1) * LANES] = out_t.T.astype(o_ref.dtype)

    @pl.when(i == nq)
    def _():
        q = q_ref[0, 0:N_META, :]
        r, c = iota2(N_META, N_META)
        for p in range(npair):
            kc = kcat(meta, p)
            qc = qcat(q, p)
            vm = vmeta_ref[0, :, p * LANES:(p + 1) * LANES]
            res = []
            for hd in range(HEAD_PAIR):
                qh = jnp.where(head_lanes(p * HEAD_PAIR + hd), qc, jnp.zeros((1, 1), BF16))
                s = lax.dot_general(qh, kc, (((1,), (1,)), ((), ())), preferred_element_type=F32)
                s = jnp.where(c <= r, s, NEG_BIG)
                pexp = jnp.exp2(s - jnp.max(s, axis=1, keepdims=True))
                pv = jnp.dot(pexp.astype(BF16), vm, preferred_element_type=F32)
                res.append(pv * (1.0 / jnp.sum(pexp, axis=1, keepdims=True)))
            o_ref[0, 0:N_META, p * LANES:(p + 1) * LANES] = jnp.where(
                lane < MLA_V_DIM, res[0], res[1]).astype(o_ref.dtype)


def _mla_call(mq, mk, mv_t, mv_meta, tq):
    b, lp, _ = mq.shape
    seq = lp - N_META
    nq = seq // tq
    return pl.pallas_call(
        functools.partial(_mla_kernel, tq=tq, seq=seq),
        out_shape=jax.ShapeDtypeStruct((b, lp, MLA_WIDTH), BF16),
        grid=(b, nq + 1),
        in_specs=[pl.BlockSpec((1, tq, MQ_W), lambda bi, i: (bi, i, 0)),
                  pl.BlockSpec((1, lp, MK_W), lambda bi, i: (bi, 0, 0)),
                  pl.BlockSpec((1, MLA_WIDTH, lp), lambda bi, i: (bi, 0, 0)),
                  pl.BlockSpec((1, N_META, MLA_WIDTH), lambda bi, i: (bi, 0, 0))],
        out_specs=pl.BlockSpec((1, tq, MLA_WIDTH), lambda bi, i: (bi, i, 0)),
        scratch_shapes=[pltpu.VMEM((MLA_HEADS, LANES, tq), F32),
                        pltpu.VMEM((MLA_HEADS, 1, tq), F32)],
        compiler_params=pltpu.CompilerParams(
            dimension_semantics=("parallel", "arbitrary"),
            vmem_limit_bytes=VMEM_LIMIT),
        name="mla_attn",
    )(mq, mk, mv_t, mv_meta)


CONV_PAD = 32
CONV_ROWS = 128
SUBLANES = 8
CONV_WIN_EXTRA = CONV_PAD


def _conv_kernel(u_ref, w_ref, b_ref, g_ref, beta_ref, o_ref, buf_ref, *, seq):
    buf_ref[0:CONV_PAD, :] = jnp.zeros((CONV_PAD, CONV_CH), F32)
    buf_ref[CONV_PAD:CONV_PAD + N_META, :] = u_ref[0, seq:seq + N_META, :]
    buf_ref[CONV_PAD + N_META:CONV_PAD + N_META + seq, :] = u_ref[0, 0:seq, :]
    tail = CONV_PAD + N_META + seq
    buf_ref[tail:tail + SUBLANES, :] = jnp.zeros((SUBLANES, CONV_CH), F32)
    w = w_ref[...]
    lead = CONV_PAD - (CONV_K - 1)

    def finish(acc):
        y = _layer_norm(acc + b_ref[...], g_ref[...], beta_ref[...])
        return (y * _sigmoid(y)).astype(o_ref.dtype)

    def conv_rows(first_pos, rows):
        win = buf_ref[pl.ds(first_pos, rows + CONV_WIN_EXTRA + SUBLANES), :]
        acc = jnp.zeros((rows, CONV_CH), F32)
        for sh in range(SUBLANES):
            part = None
            for o in range(lead, lead + CONV_K):
                if o % SUBLANES != sh:
                    continue
                k = o - lead
                term = win[o - sh:o - sh + rows + SUBLANES, :] * w[k:k + 1, :]
                part = term if part is None else part + term
            acc = acc + part[sh:sh + rows, :]
        return finish(acc)

    def body(c, _):
        r0 = pl.multiple_of(c * CONV_ROWS, CONV_ROWS)
        o_ref[0, pl.ds(r0, CONV_ROWS), :] = conv_rows(pl.multiple_of(r0 + N_META, SUBLANES), CONV_ROWS)
        return 0

    lax.fori_loop(0, seq // CONV_ROWS, body, 0)
    o_ref[0, seq:seq + N_META, :] = conv_rows(0, N_META)


def _conv_call(cu, conv_w, conv_b, ln_g, ln_b):
    b, lp, _ = cu.shape
    seq = lp - N_META
    const2 = lambda bi: (0, 0)
    return pl.pallas_call(
        functools.partial(_conv_kernel, seq=seq),
        out_shape=jax.ShapeDtypeStruct((b, lp, CONV_CH), BF16),
        grid=(b,),
        in_specs=[pl.BlockSpec((1, lp, CONV_CH), lambda bi: (bi, 0, 0)),
                  pl.BlockSpec((CONV_K, CONV_CH), const2),
                  pl.BlockSpec((1, CONV_CH), const2),
                  pl.BlockSpec((1, CONV_CH), const2),
                  pl.BlockSpec((1, CONV_CH), const2)],
        out_specs=pl.BlockSpec((1, lp, CONV_CH), lambda bi: (bi, 0, 0)),
        scratch_shapes=[pltpu.VMEM((CONV_PAD + lp + SUBLANES, CONV_CH), F32)],
        compiler_params=pltpu.CompilerParams(dimension_semantics=("parallel",),
                                             vmem_limit_bytes=VMEM_LIMIT),
        name="conv",
    )(cu, conv_w, conv_b.reshape(1, -1), ln_g.reshape(1, -1), ln_b.reshape(1, -1))


ROUTE_ROWS = 8


def _mix_kernel(sb_ref, mla_ref, cv_ref, h_ref, gg_ref, wo_ref, lg_ref, lb_ref, rw_ref, rb_ref,
                h1_ref, h1t_ref, idx_ref, gate_ref, rank_ref, cnt_ref, carry_ref, *, alpha, tm):
    step = pl.program_id(0)

    @pl.when(step == 0)
    def _():
        carry_ref[...] = jnp.zeros_like(carry_ref)

    gg = gg_ref[...]
    y = jnp.concatenate(
        [_rms_norm(sb_ref[...].astype(F32), gg[:, :SB_WIDTH]),
         _rms_norm(mla_ref[...].astype(F32), gg[:, SB_WIDTH:SB_WIDTH + MLA_WIDTH]),
         _rms_norm(cv_ref[...].astype(F32), gg[:, SB_WIDTH + MLA_WIDTH:])], axis=-1)
    mix = jnp.dot(y.astype(BF16), wo_ref[...], preferred_element_type=F32)
    h1 = _layer_norm(alpha * h_ref[...] + mix, lg_ref[...], lb_ref[...])
    h1_ref[...] = h1
    _store_token_tiles(h1t_ref, h1)

    logits = jnp.dot(h1.astype(BF16), rw_ref[...], preferred_element_type=F32) + rb_ref[...]
    vals = logits.T[:N_EXPERTS, :]
    eiota = lax.broadcasted_iota(jnp.int32, (N_EXPERTS, 1), 0).astype(F32)
    sels, tops, idxs = [], [], []
    for _ in range(TOP_K):
        m = jnp.max(vals, axis=0, keepdims=True)
        idx = jnp.min(jnp.where(vals == m, eiota, float(N_EXPERTS)), axis=0, keepdims=True)
        sel = eiota == idx
        vals = jnp.where(sel, -jnp.inf, vals)
        sels.append(sel)
        tops.append(m)
        idxs.append(idx)
    exps = [jnp.exp(t - tops[0]) for t in tops]
    denom = exps[0] + exps[1] + exps[2] + exps[3]

    chosen = jnp.zeros((N_EXPERTS, tm), F32)
    for sel in sels:
        chosen = chosen + jnp.where(sel, 1.0, 0.0)
    r = lax.broadcasted_iota(jnp.int32, (tm, tm), 0)
    c = lax.broadcasted_iota(jnp.int32, (tm, tm), 1)
    before = jnp.where(r < c, 1.0, 0.0).astype(BF16)
    earlier = jnp.dot(chosen.astype(BF16), before, preferred_element_type=F32) + carry_ref[...]

    row = lax.broadcasted_iota(jnp.int32, (ROUTE_ROWS, 1), 0)
    idx_out = jnp.zeros((ROUTE_ROWS, tm), jnp.int32)
    gate_out = jnp.zeros((ROUTE_ROWS, tm), F32)
    rank_out = jnp.zeros((ROUTE_ROWS, tm), jnp.int32)
    for k in range(TOP_K):
        rank_k = jnp.sum(jnp.where(sels[k], earlier, 0.0), axis=0, keepdims=True)
        idx_out = jnp.where(row == k, idxs[k].astype(jnp.int32), idx_out)
        gate_out = jnp.where(row == k, exps[k] / denom, gate_out)
        rank_out = jnp.where(row == k, rank_k.astype(jnp.int32), rank_out)
    idx_ref[0] = idx_out
    gate_ref[0] = gate_out
    rank_ref[0] = rank_out

    carry_ref[...] = carry_ref[...] + jnp.sum(chosen, axis=1, keepdims=True)
    cnt_ref[...] = carry_ref[...]


def _mix_call(sb_out, mla_out, conv_out, h2d, grp_g, w_out_b, ln_g, ln_b, router_w_b, router_b,
              alpha, tm):
    t, d = h2d.shape
    row = lambda i: (i, 0)
    const2 = lambda i: (0, 0)
    chunks = d // LANES
    steps = t // tm
    route = lambda i: (i, 0, 0)
    outs = (jax.ShapeDtypeStruct((t, d), F32),
            jax.ShapeDtypeStruct((t * chunks, LANES), F32),
            jax.ShapeDtypeStruct((steps, ROUTE_ROWS, tm), jnp.int32),
            jax.ShapeDtypeStruct((steps, ROUTE_ROWS, tm), F32),
            jax.ShapeDtypeStruct((steps, ROUTE_ROWS, tm), jnp.int32),
            jax.ShapeDtypeStruct((N_EXPERTS, 1), F32))
    rw_pad = jnp.pad(router_w_b, ((0, 0), (0, LANES - N_EXPERTS)))
    rb_pad = jnp.pad(router_b.reshape(1, N_EXPERTS), ((0, 0), (0, LANES - N_EXPERTS)))
    return pl.pallas_call(
        functools.partial(_mix_kernel, alpha=alpha, tm=tm),
        out_shape=outs,
        grid=(t // tm,),
        in_specs=[pl.BlockSpec((tm, SB_WIDTH), row),
                  pl.BlockSpec((tm, MLA_WIDTH), row),
                  pl.BlockSpec((tm, CONV_CH), row),
                  pl.BlockSpec((tm, d), row),
                  pl.BlockSpec((1, d), const2),
                  pl.BlockSpec((d, d), const2),
                  pl.BlockSpec((1, d), const2),
                  pl.BlockSpec((1, d), const2),
                  pl.BlockSpec((d, LANES), const2),
                  pl.BlockSpec((1, LANES), const2)],
        out_specs=(pl.BlockSpec((tm, d), row),
                   pl.BlockSpec((tm * chunks, LANES), row),
                   pl.BlockSpec((1, ROUTE_ROWS, tm), route),
                   pl.BlockSpec((1, ROUTE_ROWS, tm), route),
                   pl.BlockSpec((1, ROUTE_ROWS, tm), route),
                   pl.BlockSpec((N_EXPERTS, 1), const2)),
        scratch_shapes=[pltpu.VMEM((N_EXPERTS, 1), F32)],
        compiler_params=pltpu.CompilerParams(dimension_semantics=("arbitrary",),
                                             vmem_limit_bytes=VMEM_LIMIT),
        name="mix_router",
    )(sb_out, mla_out, conv_out, h2d, grp_g.reshape(1, d), w_out_b, ln_g.reshape(1, d),
      ln_b.reshape(1, d), rw_pad, rb_pad)


GATHER_AHEAD = 3
GATHER_BUFS = GATHER_AHEAD + 1


def _store_token_tiles(ref, val):
    tm, d = val.shape
    chunks = d // LANES
    for c in range(chunks):
        ref[pl.ds(c, tm, stride=chunks), :] = val[:, c * LANES:(c + 1) * LANES]


def _load_token_tiles(ref, tm, chunks):
    return jnp.concatenate([ref[pl.ds(c, tm, stride=chunks), :] for c in range(chunks)], axis=-1)


INV_CHUNK = 1024


def _invert_kernel(cnt_ref, start_ref, pos_hbm, tok_ref, chunk_ref, sem, *, n_chunks, n_out, tm):
    def fill(lo, hi):
        def body(r, _):
            tok_ref[r] = 0
            return 0
        lax.fori_loop(lo, hi, body, 0)

    def pad_rows(e, used_end):
        tiles_e = (cnt_ref[e] + (tm - 1)) // tm
        fill(start_ref[e] * tm + cnt_ref[e], (start_ref[e] + tiles_e) * tm)
        return jnp.maximum(used_end, (start_ref[e] + tiles_e) * tm)

    used_end = lax.fori_loop(0, N_EXPERTS, pad_rows, 0)
    fill(used_end, n_out)

    def chunk_copy(ch, slot):
        return pltpu.make_async_copy(pos_hbm.at[ch], chunk_ref.at[pl.ds(slot * INV_CHUNK, INV_CHUNK)],
                                     sem.at[slot])

    chunk_copy(0, 0).start()
    toks = INV_CHUNK // TOP_K

    def chunk(ch, _):
        slot = ch % 2
        chunk_copy(ch, slot).wait()

        @pl.when(ch + 1 < n_chunks)
        def _():
            chunk_copy(ch + 1, 1 - slot).start()

        first = slot * INV_CHUNK

        def body(tl, _):
            for k in range(TOP_K):
                tok_ref[chunk_ref[first + tl * TOP_K + k]] = ch * toks + tl
            return 0

        lax.fori_loop(0, toks, body, 0, unroll=16)
        return 0

    lax.fori_loop(0, n_chunks, chunk, 0)


def _invert_call(cnt, tile_start, pos_chunks, n_out, tm):
    n_chunks = pos_chunks.shape[0]
    return pl.pallas_call(
        functools.partial(_invert_kernel, n_chunks=n_chunks, n_out=n_out, tm=tm),
        out_shape=jax.ShapeDtypeStruct((n_out,), jnp.int32),
        grid_spec=pltpu.PrefetchScalarGridSpec(
            num_scalar_prefetch=2,
            grid=(1,),
            in_specs=[pl.BlockSpec(memory_space=pl.ANY)],
            out_specs=pl.BlockSpec(memory_space=pltpu.SMEM),
            scratch_shapes=[pltpu.SMEM((2 * INV_CHUNK,), jnp.int32),
                            pltpu.SemaphoreType.DMA((2,))]),
        compiler_params=pltpu.CompilerParams(dimension_semantics=("arbitrary",)),
        name="moe_row_table",
    )(cnt, tile_start, pos_chunks)


def _expert_kernel(te_ref, nxt_ref, par_ref, nv_ref, tok_ref, h_hbm, wgu_hbm, bgu_ref, wd_hbm,
                   bd_ref, y_ref, xbuf, wgu_f32, wd_f32, wgu_bf, wd_bf, sem, wsem,
                   *, layer, tm, d, d_ff):
    i = pl.program_id(0)
    nv = nv_ref[0]
    chunks = d // LANES

    def weight_copies(e, wslot):
        return (pltpu.make_async_copy(wgu_hbm.at[layer, e], wgu_f32.at[wslot], wsem.at[0, wslot]),
                pltpu.make_async_copy(wd_hbm.at[layer, e], wd_f32.at[wslot], wsem.at[1, wslot]))

    def row_copy(tile, r, slot):
        tok = tok_ref[tile * tm + r]
        return pltpu.make_async_copy(
            h_hbm.at[pl.ds(pl.multiple_of(tok * chunks, chunks), chunks), :],
            xbuf.at[slot, pl.ds(r * chunks, chunks), :], sem.at[slot])

    def wait_gather(slot):
        pltpu.make_async_copy(h_hbm.at[pl.ds(0, tm * chunks), :], xbuf.at[slot], sem.at[slot]).wait()

    @pl.when(i == 0)
    def _():
        for ahead in range(GATHER_AHEAD):
            def body(r, _):
                row_copy(ahead, r, ahead).start()
                return 0
            lax.fori_loop(0, tm, body, 0, unroll=8)
        for cp in weight_copies(te_ref[0], par_ref[0]):
            cp.start()

    @pl.when((i < nv) & ((i == 0) | (te_ref[i] != te_ref[jnp.maximum(i - 1, 0)])))
    def _():
        wslot = par_ref[i]
        for cp in weight_copies(te_ref[i], wslot):
            cp.wait()
        wgu_bf[...] = wgu_f32[wslot].astype(BF16)
        wd_bf[...] = wd_f32[wslot].astype(BF16)

        @pl.when(nxt_ref[i] >= 0)
        def _():
            for cp in weight_copies(nxt_ref[i], 1 - wslot):
                cp.start()

    for slot in range(GATHER_BUFS):
        @pl.when((i < nv) & (i % GATHER_BUFS == slot))
        def _():
            wait_gather(slot)
            x = _load_token_tiles(xbuf.at[slot], tm, chunks).astype(BF16)
            gu = jnp.dot(x, wgu_bf[...], preferred_element_type=F32) + bgu_ref[0]
            g = jnp.minimum(gu[:, :d_ff], SWIGLU_LIMIT)
            up = jnp.clip(gu[:, d_ff:], -SWIGLU_LIMIT, SWIGLU_LIMIT)
            act = (up + 1.0) * (g * _sigmoid(SWIGLU_ALPHA * g))
            y = jnp.dot(act.astype(BF16), wd_bf[...], preferred_element_type=F32) + bd_ref[0]
            _store_token_tiles(y_ref, y)
            for r in range(tm):
                row_copy(i + GATHER_AHEAD, r, (slot + GATHER_AHEAD) % GATHER_BUFS).start()

        @pl.when((i >= nv) & (i < nv + GATHER_AHEAD) & (i % GATHER_BUFS == slot))
        def _():
            wait_gather(slot)

    @pl.when(i >= nv)
    def _():
        y_ref[...] = jnp.zeros_like(y_ref)


def _expert_call(tile_expert, tile_next, tile_parity, n_valid, row_tok, h_tiles, wgu_all, bgu,
                 wd_all, bd, layer, n_tiles, tm):
    d_ff, d = wd_all.shape[2:]
    chunks = d // LANES
    bsel = lambda i, te, nxt, par, nv, tok: (te[i], 0, 0)
    return pl.pallas_call(
        functools.partial(_expert_kernel, layer=layer, tm=tm, d=d, d_ff=d_ff),
        out_shape=jax.ShapeDtypeStruct((n_tiles * tm * chunks, LANES), F32),
        grid_spec=pltpu.PrefetchScalarGridSpec(
            num_scalar_prefetch=5,
            grid=(n_tiles,),
            in_specs=[pl.BlockSpec(memory_space=pl.ANY),
                      pl.BlockSpec(memory_space=pl.ANY),
                      pl.BlockSpec((1, 1, 2 * d_ff), bsel),
                      pl.BlockSpec(memory_space=pl.ANY),
                      pl.BlockSpec((1, 1, d), bsel)],
            out_specs=pl.BlockSpec((tm * chunks, LANES), lambda i, te, nxt, par, nv, tok: (i, 0)),
            scratch_shapes=[pltpu.VMEM((GATHER_BUFS, tm * chunks, LANES), F32),
                            pltpu.VMEM((2, d, 2 * d_ff), F32),
                            pltpu.VMEM((2, d_ff, d), F32),
                            pltpu.VMEM((d, 2 * d_ff), BF16),
                            pltpu.VMEM((d_ff, d), BF16),
                            pltpu.SemaphoreType.DMA((GATHER_BUFS,)),
                            pltpu.SemaphoreType.DMA((2, 2))]),
        compiler_params=pltpu.CompilerParams(dimension_semantics=("arbitrary",),
                                             vmem_limit_bytes=EXPERT_VMEM_LIMIT),
        name="moe_experts",
    )(tile_expert, tile_next, tile_parity, n_valid, row_tok, h_tiles, wgu_all,
      bgu.reshape(N_EXPERTS, 1, -1), wd_all, bd.reshape(N_EXPERTS, 1, -1))


def _combine_kernel(pos_ref, ys_hbm, gate_ref, h_ref, lg_ref, lb_ref, o_ref, buf, sem,
                    *, alpha, tm, d):
    i = pl.program_id(0)
    chunks = d // LANES

    n_steps = pl.num_programs(0)

    def row_copy(step, r, k, slot):
        src = pos_ref[step * (tm * TOP_K) + r * TOP_K + k]
        return pltpu.make_async_copy(
            ys_hbm.at[pl.ds(pl.multiple_of(src * chunks, chunks), chunks), :],
            buf.at[slot, k, pl.ds(r * chunks, chunks), :], sem.at[slot])

    def wait_rows(slot):
        for k in range(TOP_K):
            pltpu.make_async_copy(ys_hbm.at[pl.ds(0, tm * chunks), :], buf.at[slot, k],
                                  sem.at[slot]).wait()

    @pl.when(i == 0)
    def _():
        for ahead in range(GATHER_AHEAD):
            def body(r, _):
                for k in range(TOP_K):
                    row_copy(jnp.minimum(ahead, n_steps - 1), r, k, ahead).start()
                return 0
            lax.fori_loop(0, tm, body, 0, unroll=2)

    for slot in range(GATHER_BUFS):
        @pl.when(i % GATHER_BUFS == slot)
        def _():
            wait_rows(slot)
            gate = gate_ref[...]
            ffn = _load_token_tiles(buf.at[slot, 0], tm, chunks) * gate[:, 0:1]
            for k in range(1, TOP_K):
                ffn = ffn + _load_token_tiles(buf.at[slot, k], tm, chunks) * gate[:, k:k + 1]
            o_ref[...] = _layer_norm(alpha * h_ref[...] + ffn, lg_ref[...], lb_ref[...])
            nxt = jnp.minimum(i + GATHER_AHEAD, n_steps - 1)
            for r in range(tm):
                for k in range(TOP_K):
                    row_copy(nxt, r, k, (slot + GATHER_AHEAD) % GATHER_BUFS).start()

        @pl.when((i % GATHER_BUFS == slot) & (i + 1 == n_steps))
        def _():
            for ahead in range(1, GATHER_BUFS):
                wait_rows((slot + ahead) % GATHER_BUFS)


def _combine_call(pos_flat, ys, gates, h2d, ln_g, ln_b, alpha, tm):
    t, d = h2d.shape
    chunks = d // LANES
    row = lambda i, pos: (i, 0)
    const2 = lambda i, pos: (0, 0)
    return pl.pallas_call(
        functools.partial(_combine_kernel, alpha=alpha, tm=tm, d=d),
        out_shape=jax.ShapeDtypeStruct((t, d), F32),
        grid_spec=pltpu.PrefetchScalarGridSpec(
            num_scalar_prefetch=1,
            grid=(t // tm,),
            in_specs=[pl.BlockSpec(memory_space=pl.ANY),
                      pl.BlockSpec((tm, LANES), row),
                      pl.BlockSpec((tm, d), row),
                      pl.BlockSpec((1, d), const2),
                      pl.BlockSpec((1, d), const2)],
            out_specs=pl.BlockSpec((tm, d), row),
            scratch_shapes=[pltpu.VMEM((GATHER_BUFS, TOP_K, tm * chunks, LANES), F32),
                            pltpu.SemaphoreType.DMA((GATHER_BUFS,))]),
        compiler_params=pltpu.CompilerParams(dimension_semantics=("arbitrary",),
                                             vmem_limit_bytes=VMEM_LIMIT),
        name="moe_combine",
    )(pos_flat, ys, gates, h2d, ln_g.reshape(1, d), ln_b.reshape(1, d))


def _in_proj_columns():
    o_cq = 3 * SB_WIDTH
    o_ckv = o_cq + MLA_Q_RANK
    o_kpe = o_ckv + MLA_KV_RANK
    o_ca = o_kpe + MLA_ROPE_DIM
    o_cg = o_ca + CONV_CH
    reps = LANES // 2 // ROPE_HALF
    kpe = np.concatenate([np.tile(o_kpe + np.arange(ROPE_HALF), reps),
                          np.tile(o_kpe + ROPE_HALF + np.arange(ROPE_HALF), reps)])
    return np.concatenate([np.arange(o_cq), o_cq + np.arange(MLA_Q_RANK),
                           o_ckv + np.arange(MLA_KV_RANK), kpe,
                           o_ca + np.arange(CONV_CH), o_cg + np.arange(CONV_CH)])


def _uq_columns():
    per = MLA_NOPE_DIM + MLA_ROPE_DIM
    nope = np.concatenate([h * per + np.arange(MLA_NOPE_DIM) for h in range(MLA_HEADS)])
    rope = []
    grp = LANES // 2 // ROPE_HALF
    for g0 in range(0, MLA_HEADS, grp):
        for half in range(2):
            for h in range(g0, g0 + grp):
                rope.append(h * per + MLA_NOPE_DIM + half * ROPE_HALF + np.arange(ROPE_HALF))
    return np.concatenate([nope] + rope)


def _ukv_columns():
    per = MLA_NOPE_DIM + MLA_V_DIM
    kn = np.concatenate([h * per + np.arange(MLA_NOPE_DIM) for h in range(MLA_HEADS)])
    vv = np.concatenate([h * per + MLA_NOPE_DIM + np.arange(MLA_V_DIM) for h in range(MLA_HEADS)])
    return np.concatenate([kn, vv])


def _rope_tables(seq):
    lp = seq + N_META
    inv = 1.0 / (ROPE_THETA ** (jnp.arange(0, MLA_ROPE_DIM, 2, dtype=F32) / MLA_ROPE_DIM))
    pos = jnp.concatenate([N_META + jnp.arange(seq, dtype=F32), jnp.arange(N_META, dtype=F32)])
    ang = pos[:, None] * inv[None, :]
    reps = LANES // 2 // ROPE_HALF
    cos = jnp.tile(jnp.cos(ang), (1, 2 * reps))
    sin = jnp.tile(jnp.sin(ang), (1, reps))
    assert cos.shape == (lp, LANES)
    return cos, jnp.concatenate([-sin, sin], axis=1)


ATTN_TQ = 512
EXPERT_TM = 256
ROUTE_TM = 128


def kernel(x, meta_tokens, ln_in_g, ln_in_b, w_in, q_norm_g, w_uq, kv_norm_g, w_ukv, conv_w, conv_b,
           conv_ln_g, conv_ln_b, grp_norm_g, w_out, ln_mix_g, ln_mix_b, router_w, router_b,
           w_gate_up, b_gate_up, w_down, b_down, ln_ffn_g, ln_ffn_b):
    b, seq, d = x.shape
    depth = w_in.shape[0]
    lp = seq + N_META
    t = b * lp
    alpha = float((2 * depth) ** 0.25)
    tq = min(ATTN_TQ, seq)
    assert seq % tq == 0 and tq % ATTN_TK == 0 and seq % CONV_ROWS == 0
    tm_tok = _row_tile(t, 512, LANES)
    tm_seq = _row_tile(lp, 1024)
    tm_route = _row_tile(t, ROUTE_TM, 8)
    n_assign = t * TOP_K
    n_tiles = -(-n_assign // EXPERT_TM) + N_EXPERTS + GATHER_AHEAD
    n_rows = n_tiles * EXPERT_TM
    n_chunks = -(-n_assign // INV_CHUNK)

    h = _ln_call(x, meta_tokens, ln_in_g, ln_in_b, tq).reshape(t, d)
    cos_t, sin_t = _rope_tables(seq)

    in_cols = _in_proj_columns()
    uq_cols, ukv_cols = _uq_columns(), _ukv_columns()

    for l in range(depth):
        w_in_p = w_in[l][:, in_cols].astype(BF16)
        sbqk, sbv, mq, mk, mv, cu = _inproj_call(
            h.reshape(b, lp, d), w_in_p, q_norm_g[l], w_uq[l][:, uq_cols].astype(BF16),
            kv_norm_g[l], w_ukv[l][:, ukv_cols].astype(BF16), cos_t, sin_t, tm_seq)
        sb_out = _sb_call(sbqk, sbv.transpose(0, 2, 1), sbv[:, seq:, :], tq)
        mla_out = _mla_call(mq, mk, mv.transpose(0, 2, 1), mv[:, seq:, :], tq)
        conv_out = _conv_call(cu, conv_w[l], conv_b[l], conv_ln_g[l], conv_ln_b[l])
        h1, h1_tiles, idx, gates, rank, counts = _mix_call(
            sb_out.reshape(t, -1), mla_out.reshape(t, -1), conv_out.reshape(t, -1), h,
            grp_norm_g[l], w_out[l].astype(BF16), ln_mix_g[l], ln_mix_b[l],
            router_w[l].astype(BF16), router_b[l], alpha, tm_tok)

        def per_token(a):
            return a[:, :TOP_K, :].transpose(0, 2, 1).reshape(t, TOP_K)

        idx, rank = per_token(idx), per_token(rank)
        gates = jnp.pad(per_token(gates), ((0, 0), (0, LANES - TOP_K)))
        cnt = counts[:, 0].astype(jnp.int32)
        tiles_e = (cnt + EXPERT_TM - 1) // EXPERT_TM
        tile_end = jnp.cumsum(tiles_e)
        tile_start = tile_end - tiles_e
        pos = tile_start[idx] * EXPERT_TM + rank
        pos_flat = pos.reshape(-1).astype(jnp.int32)
        n_valid = tile_end[-1:].astype(jnp.int32)
        tile_ids = jnp.minimum(jnp.arange(n_tiles, dtype=jnp.int32), n_valid[0] - 1)
        tile_expert = jnp.minimum(jnp.sum(tile_end[None, :] <= tile_ids[:, None], axis=1),
                                  N_EXPERTS - 1).astype(jnp.int32)

        pos_chunks = jnp.concatenate(
            [pos_flat, jnp.full((n_chunks * INV_CHUNK - n_assign,), n_rows, jnp.int32)]
        ).reshape(n_chunks, INV_CHUNK)
        row_tok = _invert_call(cnt, tile_start.astype(jnp.int32), pos_chunks, n_rows + INV_CHUNK,
                               EXPERT_TM)
        experts = jnp.arange(N_EXPERTS, dtype=jnp.int32)
        used = tiles_e > 0
        later_used = used[None, :] & (experts[None, :] > experts[:, None])
        next_e = jnp.min(jnp.where(later_used, experts[None, :], N_EXPERTS), axis=1)
        next_e = jnp.where(next_e == N_EXPERTS, -1, next_e)
        run_e = jnp.cumsum(used) - used
        tile_next = next_e[tile_expert].astype(jnp.int32)
        tile_parity = (run_e[tile_expert] % 2).astype(jnp.int32)
        ys = _expert_call(tile_expert, tile_next, tile_parity, n_valid, row_tok, h1_tiles,
                          w_gate_up, b_gate_up[l], w_down, b_down[l], l, n_tiles, EXPERT_TM)
        h = _combine_call(pos_flat, ys, gates, h1, ln_ffn_g[l], ln_ffn_b[l], alpha, tm_route)

    return h.reshape(b, lp, d)[:, :seq, :]
```
